```python
import math, functools
import jax, jax.numpy as jnp
from jax import lax
import numpy as np

D_MODEL = 2048
BATCH = 4
SEQ = 2048
DEPTH = 4
DEC_BATCH = 8
DEC_SEQ = 8
PAST_LEN = 16384
PAGE_SIZE = 128

MIX_W = D_MODEL
HD_A = 64
H_A = (MIX_W // 4) // HD_A
A_W = H_A * HD_A
H_IDX = 8
D_IDX = 64
IDX_SCALE = (D_IDX ** -0.5) * (H_IDX ** -0.5)
TOPK_MAX = 256
Q_BLOCK = 128
N_BUCKETS = 32
MAX_DISTANCE = 128
HD_B = 64
B_W = MIX_W // 2
H_B = B_W // HD_B
W_LORA = 64
A_LORA = 64
G_LORA = 160
RWKV_PROJ_W = 3 * B_W + W_LORA + A_LORA + G_LORA
RWKV_SPLITS = (B_W, 2 * B_W, 3 * B_W, 3 * B_W + W_LORA, 3 * B_W + W_LORA + A_LORA)
GN_EPS = 64e-5
C_W = MIX_W - A_W - B_W
CHUNK = 128
CD = 64
G_C = C_W // CD
PROJ_SIZES = (A_W, A_W, A_W, H_IDX * D_IDX, D_IDX, H_IDX, RWKV_PROJ_W, C_W, C_W)
PROJ_W = sum(PROJ_SIZES)
PROJ_SPLITS = tuple(int(s) for s in np.cumsum(PROJ_SIZES)[:-1])
D_FF = -(-8 * D_MODEL // (3 * 256)) * 256
NORM_EPS = 1e-6
NEG = -1e30

kernel_name = 'hymba_dsa_rwkv7_gmlp_decode_step'


def rms_norm(x, g):
    xf = x.astype(jnp.float32)
    y = xf * lax.rsqrt(jnp.mean(xf * xf, axis=-1, keepdims=True) + NORM_EPS)
    return (y * g).astype(x.dtype)


def rel_bucket(dist):
    max_exact = N_BUCKETS // 2
    n = jnp.maximum(dist, 0)
    nf = jnp.maximum(n, 1).astype(jnp.float32)
    large = max_exact + (jnp.log(nf / max_exact) / math.log(MAX_DISTANCE / max_exact)
                         * (N_BUCKETS - max_exact)).astype(jnp.int32)
    return jnp.where(n < max_exact, n, jnp.minimum(large, N_BUCKETS - 1))


def gather_rows(a, idx):
    return jax.vmap(lambda ab, ib: ab[ib])(a, idx)


def indexer_scores(iq, iw, ik):
    s = jnp.einsum('bthd,bld->bthl', iq.astype(jnp.float32), ik.astype(jnp.float32))
    return jnp.einsum('bth,bthl->btl', iw.astype(jnp.float32), jax.nn.relu(s)) * IDX_SCALE


def sparse_attention(q, ksel, vsel, qpos, kpos, rel_bias):
    b, t = q.shape[:2]
    logits = jnp.einsum('bthd,btkhd->bthk', q.astype(jnp.float32),
                        ksel.astype(jnp.float32)) * (HD_A ** -0.5)
    dist = qpos[None, :, None] - kpos
    bias = rel_bias.astype(jnp.float32)[rel_bucket(dist)]
    logits = logits + jnp.swapaxes(bias, -1, -2)
    logits = jnp.where((dist >= 0)[:, :, None, :], logits, NEG)
    p = jax.nn.softmax(logits, axis=-1)
    out = jnp.einsum('bthk,btkhd->bthd', p, vsel.astype(jnp.float32))
    return out.reshape(b, t, A_W).astype(q.dtype)


def dsa_prompt(q, k, v, iq, ik, iw, rel_bias):
    b, s = q.shape[:2]
    topk = min(TOPK_MAX, s // 4)
    n_blk = s // Q_BLOCK

    def blocks(a):
        return a.reshape((b, n_blk, Q_BLOCK) + a.shape[2:]).swapaxes(0, 1)

    qpos_blk = jnp.arange(s, dtype=jnp.int32).reshape(n_blk, Q_BLOCK)
    kpos = jnp.arange(s, dtype=jnp.int32)

    def one_block(args):
        qb, iqb, iwb, qpos = args
        sc = indexer_scores(iqb, iwb, ik)
        sc = jnp.where(kpos[None, None, :] <= qpos[None, :, None], sc, NEG)
        _, idx = lax.top_k(sc, topk)
        return sparse_attention(qb, gather_rows(k, idx), gather_rows(v, idx), qpos, idx, rel_bias)

    out = lax.map(one_block, (blocks(q), blocks(iq), blocks(iw), qpos_blk))
    return out.swapaxes(0, 1).reshape(b, s, A_W)


def dsa_sample(q, k, v, iq, ik, iw, layer, cache_k, cache_v, cache_idx_k, page_table, rel_bias):
    nb, t = q.shape[:2]
    past = page_table.shape[1] * PAGE_SIZE
    n_keys = past + t
    topk = min(TOPK_MAX, n_keys // 4)
    ik_past = cache_idx_k[layer, page_table].reshape(nb, past, D_IDX)
    ik_all = jnp.concatenate([ik_past.astype(ik.dtype), ik], axis=1)
    qpos = past + jnp.arange(t, dtype=jnp.int32)
    sc = indexer_scores(iq, iw, ik_all)
    sc = jnp.where(jnp.arange(n_keys, dtype=jnp.int32)[None, None, :] <= qpos[None, :, None], sc, NEG)
    _, idx = lax.top_k(sc, topk)
    in_past = (idx < past)[..., None, None]
    pidx = jnp.minimum(idx, past - 1)
    phys = page_table[jnp.arange(nb)[:, None, None], pidx // PAGE_SIZE]
    off = pidx % PAGE_SIZE
    nidx = jnp.clip(idx - past, 0, t - 1)
    ksel = jnp.where(in_past, cache_k[layer, phys, off].astype(k.dtype), gather_rows(k, nidx))
    vsel = jnp.where(in_past, cache_v[layer, phys, off].astype(v.dtype), gather_rows(v, nidx))
    return sparse_attention(q, ksel, vsel, qpos, idx, rel_bias)


def rwkv_step(S, inp):
    r_t, w_t, k_t, v_t, kk_t, a_t = inp
    sa = jnp.einsum('bhvk,bhk->bhv', S, -kk_t)
    S = S * w_t[:, :, None, :] + sa[..., None] * (kk_t * a_t)[:, :, None, :] + v_t[..., None] * k_t[:, :, None, :]
    return S, jnp.einsum('bhvk,bhk->bhv', S, r_t)


def rwkv_mix(p, prev, S0, lp):
    b, t, _ = p.shape
    f32 = jnp.float32
    p_shift = jnp.concatenate([prev[:, None, :].astype(p.dtype), p[:, :-1]], axis=1)
    xs = p + (p_shift - p) * lp['rwkv_mu']
    r, k, v, xw, xa, xg = jnp.split(xs, RWKV_SPLITS, axis=-1)
    w_log = -jax.nn.softplus(-(lp['rwkv_w0'] + jnp.tanh(xw) @ lp['rwkv_w2'])) - 0.5
    decay = jnp.exp(-jnp.exp(w_log.astype(f32)))
    a = jax.nn.sigmoid(lp['rwkv_a0'] + xa @ lp['rwkv_a2'])
    g = jax.nn.sigmoid(xg) @ lp['rwkv_g2']

    def heads(z):
        return z.reshape(b, t, H_B, HD_B).astype(f32)

    r, k, v, a, decay = heads(r), heads(k), heads(v), heads(a), heads(decay)
    kk = k * lp['rwkv_kk'].reshape(H_B, HD_B).astype(f32)
    kk = kk / jnp.maximum(jnp.sqrt(jnp.sum(kk * kk, axis=-1, keepdims=True)), 1e-12)
    k = k * (1.0 + (a - 1.0) * lp['rwkv_ka'].reshape(H_B, HD_B).astype(f32))

    def tm(z):
        return jnp.moveaxis(z, 1, 0)

    S_fin, ys = lax.scan(rwkv_step, S0.astype(f32), (tm(r), tm(decay), tm(k), tm(v), tm(kk), tm(a)))
    y = jnp.moveaxis(ys, 0, 1)
    mu = jnp.mean(y, axis=-1, keepdims=True)
    var = jnp.mean(jnp.square(y - mu), axis=-1, keepdims=True)
    y = ((y - mu) * lax.rsqrt(var + GN_EPS)).reshape(b, t, B_W)
    y = y * lp['rwkv_ln_w'].astype(f32) + lp['rwkv_ln_b'].astype(f32)
    bonus = jnp.sum(r * k * lp['rwkv_rk'].reshape(H_B, HD_B).astype(f32), axis=-1, keepdims=True) * v
    y = (y + bonus.reshape(b, t, B_W)) * g.astype(f32)
    return y.astype(p.dtype), p[:, -1], S_fin


def chunk_spatial(v, ws, bs):
    b, t, _ = v.shape
    tp = -(-t // CHUNK) * CHUNK
    vp = jnp.pad(v, ((0, 0), (0, tp - t), (0, 0))).reshape(b, tp // CHUNK, CHUNK, G_C, CD)
    wm = jnp.where(jnp.tril(jnp.ones((CHUNK, CHUNK), dtype=bool)), ws, 0.0)
    out = jnp.einsum('gts,bcsgd->bctgd', wm, vp) + bs.T[:, :, None]
    return out.reshape(b, tp, C_W)[:, :t]


def gmlp_mix(pu, pv, lp):
    u = jax.nn.gelu(pu)
    vf = jax.nn.gelu(pv).astype(jnp.float32)
    mu = jnp.mean(vf, axis=-1, keepdims=True)
    var = jnp.mean(jnp.square(vf - mu), axis=-1, keepdims=True)
    v = ((vf - mu) * lax.rsqrt(var + 1e-5) * lp['gmlp_ln_w'] + lp['gmlp_ln_b']).astype(pu.dtype)
    return u * chunk_spatial(v, lp['gmlp_ws'], lp['gmlp_b']), v


def hybrid_layer(x, lp, attend, shift_prev, wkv0):
    b, t, _ = x.shape
    xn = rms_norm(x, lp['norm_mix'])
    proj = xn @ lp['w_in']
    q, k, v, iq, ik, iw, pb, pu, pv = jnp.split(proj, PROJ_SPLITS, axis=-1)
    q = q.reshape(b, t, H_A, HD_A)
    k = k.reshape(b, t, H_A, HD_A)
    v = v.reshape(b, t, H_A, HD_A)
    iq = iq.reshape(b, t, H_IDX, D_IDX)
    ya = attend(q, k, v, iq, ik, iw)
    yb, shift_new, wkv_new = rwkv_mix(pb, shift_prev, wkv0, lp)
    yc, gv = gmlp_mix(pu, pv, lp)
    h = x + jnp.concatenate([ya, yb, yc], axis=-1) @ lp['w_out']
    hn = rms_norm(h, lp['norm_ffn'])
    y = h + (jax.nn.silu(hn @ lp['ffn_gate']) * (hn @ lp['ffn_up'])) @ lp['ffn_down']
    return y, k, v, ik, shift_new, wkv_new, gv


def setup_inputs(seed: int = 0) -> dict:
    key = jax.random.key(seed)
    ks = iter(jax.random.split(key, 48))
    f32 = jnp.float32

    def nrm(shape, scale):
        return jax.random.normal(next(ks), shape, f32) * scale

    def unif(shape, lo, hi):
        return jax.random.uniform(next(ks), shape, f32, lo, hi)

    n_pages = PAST_LEN // PAGE_SIZE
    n_pool = (5 * DEC_BATCH * n_pages + 3) // 4
    page_table = jax.random.permutation(next(ks), n_pool)[:DEC_BATCH * n_pages]
    page_table = page_table.reshape(DEC_BATCH, n_pages).astype(jnp.int32)
    return {
        'x_prompt': nrm((BATCH, SEQ, D_MODEL), 1.0),
        'x_sample': nrm((DEC_BATCH, DEC_SEQ, D_MODEL), 1.0),
        'cache_k': nrm((DEPTH, n_pool, PAGE_SIZE, H_A, HD_A), 1.0),
        'cache_v': nrm((DEPTH, n_pool, PAGE_SIZE, H_A, HD_A), 1.0),
        'cache_idx_k': nrm((DEPTH, n_pool, PAGE_SIZE, D_IDX), 1.0),
        'state_rwkv_shift': nrm((DEPTH, DEC_BATCH, RWKV_PROJ_W), 1.0),
        'state_rwkv_wkv': nrm((DEPTH, DEC_BATCH, H_B, HD_B, HD_B), 0.1),
        'page_table': page_table,
        'rel_bias': nrm((N_BUCKETS, H_A), 0.2),
        'norm_mix': 1.0 + nrm((DEPTH, D_MODEL), 0.02),
        'w_in': nrm((DEPTH, D_MODEL, PROJ_W), D_MODEL ** -0.5),
        'rwkv_mu': unif((DEPTH, RWKV_PROJ_W), 0.0, 1.0),
        'rwkv_w0': unif((DEPTH, B_W), -4.0, 1.0),
        'rwkv_w2': nrm((DEPTH, W_LORA, B_W), 0.1 * W_LORA ** -0.5),
        'rwkv_a0': nrm((DEPTH, B_W), 0.1),
        'rwkv_a2': nrm((DEPTH, A_LORA, B_W), 0.5 * A_LORA ** -0.5),
        'rwkv_g2': nrm((DEPTH, G_LORA, B_W), G_LORA ** -0.5),
        'rwkv_kk': 0.85 + nrm((DEPTH, B_W), 0.05),
        'rwkv_ka': 1.0 + nrm((DEPTH, B_W), 0.05),
        'rwkv_rk': nrm((DEPTH, B_W), 0.1),
        'rwkv_ln_w': 1.0 + nrm((DEPTH, B_W), 0.02),
        'rwkv_ln_b': nrm((DEPTH, B_W), 0.02),
        'gmlp_ln_w': 1.0 + nrm((DEPTH, C_W), 0.02),
        'gmlp_ln_b': nrm((DEPTH, C_W), 0.02),
        'gmlp_ws': nrm((DEPTH, G_C, CHUNK, CHUNK), CHUNK ** -0.5),
        'gmlp_b': 1.0 + nrm((DEPTH, G_C, CHUNK), 0.02),
        'w_out': nrm((DEPTH, MIX_W, D_MODEL), MIX_W ** -0.5),
        'norm_ffn': 1.0 + nrm((DEPTH, D_MODEL), 0.02),
        'ffn_gate': nrm((DEPTH, D_MODEL, D_FF), D_MODEL ** -0.5),
        'ffn_up': nrm((DEPTH, D_MODEL, D_FF), D_MODEL ** -0.5),
        'ffn_down': nrm((DEPTH, D_FF, D_MODEL), D_FF ** -0.5),
        'norm_final': 1.0 + nrm((D_MODEL,), 0.02),
    }


def reference(x_prompt, x_sample, cache_k, cache_v, cache_idx_k, state_rwkv_shift, state_rwkv_wkv,
              page_table, rel_bias, norm_mix, w_in, rwkv_mu, rwkv_w0, rwkv_w2, rwkv_a0, rwkv_a2,
              rwkv_g2, rwkv_kk, rwkv_ka, rwkv_rk, rwkv_ln_w, rwkv_ln_b, gmlp_ln_w, gmlp_ln_b,
              gmlp_ws, gmlp_b, w_out, norm_ffn, ffn_gate, ffn_up, ffn_down, norm_final):
    xp, xs = x_prompt, x_sample
    nbp = xp.shape[0]
    zero_shift = jnp.zeros((nbp, RWKV_PROJ_W), xp.dtype)
    zero_wkv = jnp.zeros((nbp, H_B, HD_B, HD_B), jnp.float32)
    attend_p = functools.partial(dsa_prompt, rel_bias=rel_bias)
    kp_l, vp_l, ikp_l, shp_l, wkvp_l = [], [], [], [], []
    ks_l, vs_l, iks_l, shs_l, wkvs_l, gvs_l = [], [], [], [], [], []
    for l in range(DEPTH):
        lp = {
            'norm_mix': norm_mix[l], 'w_in': w_in[l], 'rwkv_mu': rwkv_mu[l], 'rwkv_w0': rwkv_w0[l],
            'rwkv_w2': rwkv_w2[l], 'rwkv_a0': rwkv_a0[l], 'rwkv_a2': rwkv_a2[l], 'rwkv_g2': rwkv_g2[l],
            'rwkv_kk': rwkv_kk[l], 'rwkv_ka': rwkv_ka[l], 'rwkv_rk': rwkv_rk[l],
            'rwkv_ln_w': rwkv_ln_w[l], 'rwkv_ln_b': rwkv_ln_b[l], 'gmlp_ln_w': gmlp_ln_w[l],
            'gmlp_ln_b': gmlp_ln_b[l], 'gmlp_ws': gmlp_ws[l], 'gmlp_b': gmlp_b[l], 'w_out': w_out[l],
            'norm_ffn': norm_ffn[l], 'ffn_gate': ffn_gate[l], 'ffn_up': ffn_up[l], 'ffn_down': ffn_down[l],
        }
        xp, kp, vp, ikp, shp, wkvp, _ = hybrid_layer(xp, lp, attend_p, zero_shift, zero_wkv)
        kp_l.append(kp); vp_l.append(vp); ikp_l.append(ikp); shp_l.append(shp); wkvp_l.append(wkvp)
        attend_s = functools.partial(dsa_sample, layer=l, cache_k=cache_k, cache_v=cache_v,
                                     cache_idx_k=cache_idx_k, page_table=page_table, rel_bias=rel_bias)
        xs, k_s, v_s, ik_s, sh_s, wkv_s, gv_s = hybrid_layer(xs, lp, attend_s, state_rwkv_shift[l],
                                                            state_rwkv_wkv[l])
        ks_l.append(k_s); vs_l.append(v_s); iks_l.append(ik_s); shs_l.append(sh_s)
        wkvs_l.append(wkv_s); gvs_l.append(gv_s)
    y_prompt = rms_norm(xp, norm_final)
    y_sample = rms_norm(xs, norm_final)
    return (y_prompt, y_sample,
            jnp.stack(kp_l), jnp.stack(vp_l), jnp.stack(ikp_l), jnp.stack(shp_l), jnp.stack(wkvp_l),
            jnp.stack(ks_l), jnp.stack(vs_l), jnp.stack(iks_l), jnp.stack(shs_l), jnp.stack(wkvs_l),
            jnp.stack(gvs_l))
```

```python
import functools
import math

import numpy as np
import jax
import jax.numpy as jnp
from jax import lax
from jax.experimental import pallas as pl
from jax.experimental.pallas import tpu as pltpu

F32 = jnp.float32
BF16 = jnp.bfloat16
I32 = jnp.int32

LANES = 128
SUBLANES = 8
VMEM_LIMIT = 56 * 1024 * 1024

D_MODEL = 2048
HD_A = 64
H_A = 8
A_W = H_A * HD_A
H_IDX = 8
D_IDX = 64
IDX_SCALE = (D_IDX ** -0.5) * (H_IDX ** -0.5)
TOPK_MAX = 256
N_BUCKETS = 32
MAX_DISTANCE = 128
HD_B = 64
B_W = 1024
H_B = B_W // HD_B
W_LORA = 64
A_LORA = 64
G_LORA = 160
RWKV_PROJ_W = 3 * B_W + W_LORA + A_LORA + G_LORA
GN_EPS = 64e-5
C_W = 512
CHUNK = 128
CD = 64
G_C = C_W // CD
D_FF = 5632
NORM_EPS = 1e-6
NEG = -1e30
PAGE = 128
INT_MIN = -(2 ** 31)

SEG_A_END = 3 * A_W + H_IDX * D_IDX + D_IDX + H_IDX
OFF_IK = 2048
OFF_R = 2176
PAD_A = OFF_R - SEG_A_END
OFF_GM = 5632
PAD_R = OFF_GM - OFF_R - RWKV_PROJ_W
PROJ_P = OFF_GM + 2 * C_W
CB_Q, CB_K, CB_V, CB_IQ = 0, 1, 2, 3
CB_IK = OFF_IK // LANES
CB_RR = OFF_R // LANES
CB_RK = CB_RR + B_W // LANES
CB_RV = CB_RK + B_W // LANES
CB_WA = CB_RV + B_W // LANES
CB_G = (OFF_R + 3 * B_W + W_LORA + A_LORA) // 256
CB_U = OFF_GM // C_W
CB_PV = CB_U + 1
RW_C = 64


def _cparams(sem):
    return pltpu.CompilerParams(dimension_semantics=sem, vmem_limit_bytes=VMEM_LIMIT)


def _pad_cols(a):
    z = lambda n: jnp.zeros(a.shape[:-1] + (n,), a.dtype)
    return jnp.concatenate([a[..., :SEG_A_END], z(PAD_A), a[..., SEG_A_END:SEG_A_END + RWKV_PROJ_W], z(PAD_R),
                            a[..., SEG_A_END + RWKV_PROJ_W:]], axis=-1)


def _pad_rwkv_cols(a):
    z = lambda n: jnp.zeros(a.shape[:-1] + (n,), a.dtype)
    return jnp.concatenate([z(OFF_R), a, z(PROJ_P - OFF_R - RWKV_PROJ_W)], axis=-1)


def _norm_matmul_kernel(x_ref, g_ref, w_ref, o_ref, xn_ref):
    @pl.when(pl.program_id(1) == 0)
    def _():
        x = x_ref[...]
        y = x * lax.rsqrt(jnp.mean(x * x, axis=-1, keepdims=True) + NORM_EPS)
        xn_ref[...] = (y * g_ref[...]).astype(BF16)

    o_ref[...] = jnp.dot(xn_ref[...], w_ref[...], preferred_element_type=F32)


def norm_matmul(x, g, w, tm, tn):
    m, d = x.shape
    n = w.shape[1]
    return pl.pallas_call(
        _norm_matmul_kernel,
        grid=(m // tm, n // tn),
        in_specs=[pl.BlockSpec((tm, d), lambda i, j: (i, 0)),
                  pl.BlockSpec((1, d), lambda i, j: (0, 0)),
                  pl.BlockSpec((d, tn), lambda i, j: (0, j))],
        out_specs=pl.BlockSpec((tm, tn), lambda i, j: (i, j)),
        out_shape=jax.ShapeDtypeStruct((m, n), F32),
        scratch_shapes=[pltpu.VMEM((tm, d), BF16)],
        compiler_params=_cparams(("parallel", "arbitrary")),
        name="norm_matmul",
    )(x, g.reshape(1, d), w)


def _out_proj_kernel(x_ref, ya_ref, yb_ref, yc_ref, w_ref, o_ref):
    acc = jnp.dot(ya_ref[...], w_ref[0:A_W, :], preferred_element_type=F32)
    acc += jnp.dot(yb_ref[...], w_ref[A_W:A_W + B_W, :], preferred_element_type=F32)
    acc += jnp.dot(yc_ref[...], w_ref[A_W + B_W:, :], preferred_element_type=F32)
    o_ref[...] = x_ref[...] + acc


def out_proj(x, ya, yb, yc, w, tm):
    m, d = x.shape
    return pl.pallas_call(
        _out_proj_kernel,
        grid=(m // tm,),
        in_specs=[pl.BlockSpec((tm, d), lambda i: (i, 0)),
                  pl.BlockSpec((tm, A_W), lambda i: (i, 0)),
                  pl.BlockSpec((tm, B_W), lambda i: (i, 0)),
                  pl.BlockSpec((tm, C_W), lambda i: (i, 0)),
                  pl.BlockSpec(w.shape, lambda i: (0, 0))],
        out_specs=pl.BlockSpec((tm, d), lambda i: (i, 0)),
        out_shape=jax.ShapeDtypeStruct((m, d), F32),
        compiler_params=_cparams(("parallel",)),
        name="out_proj",
    )(x, ya, yb, yc, w)


def _ffn_kernel(h_ref, g_ref, wg_ref, wu_ref, wd_ref, o_ref, hn_ref, acc_ref):
    f = pl.program_id(1)

    @pl.when(f == 0)
    def _():
        x = h_ref[...]
        y = x * lax.rsqrt(jnp.mean(x * x, axis=-1, keepdims=True) + NORM_EPS)
        hn_ref[...] = (y * g_ref[...]).astype(BF16)
        acc_ref[...] = jnp.zeros_like(acc_ref)

    hn = hn_ref[...]
    gate = jnp.dot(hn, wg_ref[...], preferred_element_type=F32)
    up = jnp.dot(hn, wu_ref[...], preferred_element_type=F32)
    act = (gate / (1.0 + jnp.exp(-gate))) * up
    acc_ref[...] += jnp.dot(act.astype(BF16), wd_ref[...], preferred_element_type=F32)

    @pl.when(f == pl.num_programs(1) - 1)
    def _():
        o_ref[...] = h_ref[...] + acc_ref[...]


def ffn(h, g, wg, wu, wd, tm, tf):
    m, d = h.shape
    nf = wg.shape[1]
    return pl.pallas_call(
        _ffn_kernel,
        grid=(m // tm, nf // tf),
        in_specs=[pl.BlockSpec((tm, d), lambda i, f: (i, 0)),
                  pl.BlockSpec((1, d), lambda i, f: (0, 0)),
                  pl.BlockSpec((d, tf), lambda i, f: (0, f)),
                  pl.BlockSpec((d, tf), lambda i, f: (0, f)),
                  pl.BlockSpec((tf, d), lambda i, f: (f, 0))],
        out_specs=pl.BlockSpec((tm, d), lambda i, f: (i, 0)),
        out_shape=jax.ShapeDtypeStruct((m, d), F32),
        scratch_shapes=[pltpu.VMEM((tm, d), BF16), pltpu.VMEM((tm, d), F32)],
        compiler_params=_cparams(("parallel", "arbitrary")),
        name="ffn",
    )(h, g.reshape(1, d), wg, wu, wd)


def _rms_kernel(x_ref, g_ref, o_ref):
    x = x_ref[...]
    o_ref[...] = x * lax.rsqrt(jnp.mean(x * x, axis=-1, keepdims=True) + NORM_EPS) * g_ref[...]


def rms_final(x, g, tm):
    m, d = x.shape
    return pl.pallas_call(
        _rms_kernel,
        grid=(m // tm,),
        in_specs=[pl.BlockSpec((tm, d), lambda i: (i, 0)), pl.BlockSpec((1, d), lambda i: (0, 0))],
        out_specs=pl.BlockSpec((tm, d), lambda i: (i, 0)),
        out_shape=jax.ShapeDtypeStruct((m, d), F32),
        compiler_params=_cparams(("parallel",)),
        name="rms_final",
    )(x, g.reshape(1, d))


def _bucket_np(dist):
    max_exact = N_BUCKETS // 2
    n = np.maximum(dist, 0)
    nf = np.maximum(n, 1).astype(np.float64)
    large = max_exact + (np.log(nf / max_exact) / math.log(MAX_DISTANCE / max_exact)
                         * (N_BUCKETS - max_exact)).astype(np.int32)
    return np.where(n < max_exact, n, np.minimum(large, N_BUCKETS - 1)).astype(np.int32)


def _bias_tiles_kernel(rb_ref, bp_ref, bs_ref, op_ref, os_ref):
    for d in range(2):
        bk = bp_ref[d]
        for h in range(H_A):
            t = jnp.zeros(bk.shape, F32)
            for b in range(N_BUCKETS):
                t = jnp.where(bk == b, rb_ref[b, h], t)
            op_ref[d, h] = t
    for d in range(3):
        bk = bs_ref[d]
        for h in range(H_A):
            t = jnp.zeros(bk.shape, F32)
            for b in range(N_BUCKETS):
                t = jnp.where(bk == b, rb_ref[b, h], t)
            os_ref[d, h * SUBLANES:(h + 1) * SUBLANES, :] = t


def bias_tiles(rel_bias, dec_seq):
    r = np.arange(LANES)[:, None]
    c = np.arange(LANES)[None, :]
    bp = np.stack([_bucket_np(r - c), _bucket_np(LANES + r - c)])
    t = np.arange(dec_seq)[:, None]
    bs = np.stack([_bucket_np(np.full((dec_seq, LANES), 4 * MAX_DISTANCE)),
                   _bucket_np(PAGE + t - c), _bucket_np(t - c)])
    vm = pl.BlockSpec(memory_space=pltpu.VMEM)
    return pl.pallas_call(
        _bias_tiles_kernel,
        in_specs=[pl.BlockSpec(memory_space=pltpu.SMEM), vm, vm],
        out_specs=[vm, vm],
        out_shape=[jax.ShapeDtypeStruct((2, H_A, LANES, LANES), F32),
                   jax.ShapeDtypeStruct((3, H_A * dec_seq, LANES), F32)],
        name="bias_tiles",
    )(rel_bias, jnp.asarray(bp), jnp.asarray(bs))


def _score_key(s):
    s = jnp.where(s == 0.0, 0.0, s)
    bits = lax.bitcast_convert_type(s, I32)
    return jnp.where(bits < 0, bits ^ 0x7FFFFFFF, bits)


def _kth_largest(sc_ref, n_chunks, kf, shape):
    def count_ge(cand):
        def body(kc, a):
            return a + (sc_ref[kc] >= cand).astype(F32)

        a = lax.fori_loop(0, n_chunks, body, jnp.zeros(shape, F32))
        return jnp.sum(a, axis=1, keepdims=True)

    zero = jnp.zeros(shape, I32)
    t0 = jnp.where(count_ge(zero) >= kf, zero, jnp.full(shape, INT_MIN, I32))

    def bit_body(it, t):
        cand = t | jnp.left_shift(jnp.int32(1), 30 - it)
        return jnp.where(count_ge(cand) >= kf, cand, t)

    return lax.fori_loop(0, 31, bit_body, t0)


def _select_mask(sc_ref, mb_ref, n_chunks, kf, thr, shape, causal_fn):
    def cnt_body(kc, a):
        return a + (sc_ref[kc] > thr).astype(F32)

    cgt = jnp.sum(lax.fori_loop(0, n_chunks, cnt_body, jnp.zeros(shape, F32)), axis=1, keepdims=True)
    need = kf - cgt
    rr = lax.broadcasted_iota(I32, (LANES, LANES), 0)
    cc = lax.broadcasted_iota(I32, (LANES, LANES), 1)
    tri = (rr <= cc).astype(BF16)

    def body(kc, offs):
        key = sc_ref[kc]
        eq = key == thr
        pre = jnp.dot(eq.astype(BF16), tri, preferred_element_type=F32) + offs
        sel = (key > thr) | (eq & (pre <= need))
        mb_ref[kc] = jnp.where(sel & causal_fn(kc), 0.0, NEG)
        return jnp.broadcast_to(pre[:, LANES - 1:LANES], shape)

    lax.fori_loop(0, n_chunks, body, jnp.zeros(shape, F32))


def _dsa_prompt_kernel(topk, q_ref, iq_ref, iw_ref, k_ref, v_ref, ik_ref, bias_ref, rb_ref, o_ref,
                       kb, vb, ikb, sc, mb, acc, m_s, l_s):
    i = pl.program_id(1)
    tq = q_ref.shape[0]
    shape = (tq, LANES)

    @pl.when(i == 0)
    def _():
        kb[...] = k_ref[...].astype(BF16)
        vb[...] = v_ref[...].astype(BF16)
        ikb[...] = ik_ref[:, 0:D_IDX].astype(BF16)

    row = lax.broadcasted_iota(I32, shape, 0) + i * tq
    col = lax.broadcasted_iota(I32, shape, 1)
    n_chunks = i + 1
    iq = iq_ref[...].astype(BF16)
    iw = iw_ref[:, D_IDX:D_IDX + H_IDX]

    def causal(kc):
        return (col + kc * LANES) <= row

    def score_body(kc, carry):
        ikc = ikb[pl.ds(pl.multiple_of(kc * LANES, LANES), LANES), :]
        tot = jnp.zeros(shape, F32)
        for h in range(H_IDX):
            s = lax.dot_general(iq[:, h * D_IDX:(h + 1) * D_IDX], ikc, (((1,), (1,)), ((), ())),
                                preferred_element_type=F32)
            tot = tot + iw[:, h:h + 1] * jnp.maximum(s, 0.0)
        tot = jnp.where(causal(kc), tot * IDX_SCALE, NEG)
        sc[kc] = _score_key(tot)
        return carry

    lax.fori_loop(0, n_chunks, score_body, 0)
    kf = float(topk)
    thr = _kth_largest(sc, n_chunks, kf, shape)
    _select_mask(sc, mb, n_chunks, kf, thr, shape, causal)

    m_s[...] = jnp.full(m_s.shape, NEG, F32)
    l_s[...] = jnp.zeros(l_s.shape, F32)
    acc[...] = jnp.zeros(acc.shape, F32)
    q = (q_ref[...] * (HD_A ** -0.5)).astype(BF16)

    def attn_body(kc, carry):
        off = pl.multiple_of(kc * LANES, LANES)
        kcb = kb[pl.ds(off, LANES), :]
        vcb = vb[pl.ds(off, LANES), :]
        mbc = mb[kc]
        d = i - kc
        for h in range(H_A):
            hs = slice(h * HD_A, (h + 1) * HD_A)
            lg = lax.dot_general(q[:, hs], kcb[:, hs], (((1,), (1,)), ((), ())), preferred_element_type=F32)
            bias = jnp.where(d == 0, bias_ref[0, h], jnp.where(d == 1, bias_ref[1, h], rb_ref[N_BUCKETS - 1, h]))
            lg = lg + bias + mbc
            m_old = m_s[h]
            m_new = jnp.maximum(m_old, jnp.max(lg, axis=1, keepdims=True))
            p = jnp.exp(lg - m_new)
            alpha = jnp.exp(m_old - m_new)
            l_s[h] = alpha * l_s[h] + jnp.sum(p, axis=1, keepdims=True)
            acc[:, hs] = alpha * acc[:, hs] + jnp.dot(p.astype(BF16), vcb[:, hs], preferred_element_type=F32)
            m_s[h] = m_new
        return carry

    lax.fori_loop(0, n_chunks, attn_body, 0)
    for h in range(H_A):
        hs = slice(h * HD_A, (h + 1) * HD_A)
        o_ref[:, hs] = (acc[:, hs] / l_s[h]).astype(o_ref.dtype)


def dsa_prompt(proj, bias_p, rel_bias, tq=LANES):
    b, s, _ = proj.shape
    topk = min(TOPK_MAX, s // 4)
    nc = s // LANES
    kern = functools.partial(_dsa_prompt_kernel, topk)
    return pl.pallas_call(
        kern,
        grid=(b, s // tq),
        in_specs=[pl.BlockSpec((None, tq, A_W), lambda bi, i: (bi, i, CB_Q)),
                  pl.BlockSpec((None, tq, A_W), lambda bi, i: (bi, i, CB_IQ)),
                  pl.BlockSpec((None, tq, LANES), lambda bi, i: (bi, i, CB_IK)),
                  pl.BlockSpec((None, s, A_W), lambda bi, i: (bi, 0, CB_K)),
                  pl.BlockSpec((None, s, A_W), lambda bi, i: (bi, 0, CB_V)),
                  pl.BlockSpec((None, s, LANES), lambda bi, i: (bi, 0, CB_IK)),
                  pl.BlockSpec(bias_p.shape, lambda bi, i: (0, 0, 0, 0)),
                  pl.BlockSpec(memory_space=pltpu.SMEM)],
        out_specs=pl.BlockSpec((None, tq, A_W), lambda bi, i: (bi, i, 0)),
        out_shape=jax.ShapeDtypeStruct((b, s, A_W), BF16),
        scratch_shapes=[pltpu.VMEM((s, A_W), BF16), pltpu.VMEM((s, A_W), BF16), pltpu.VMEM((s, D_IDX), BF16),
                        pltpu.VMEM((nc, tq, LANES), I32), pltpu.VMEM((nc, tq, LANES), F32),
                        pltpu.VMEM((tq, A_W), F32), pltpu.VMEM((H_A, tq, 1), F32), pltpu.VMEM((H_A, tq, 1), F32)],
        compiler_params=_cparams(("parallel", "arbitrary")),
        name="dsa_prompt",
    )(proj, proj, proj, proj, proj, proj, bias_p, rel_bias)


def _dsa_sample_select_kernel(pps, n_pages, topk, pt_ref, iq_ref, ikiw_ref, *rest):
    page_refs = rest[:pps]
    mb_ref = rest[pps]
    iq2, wb, sc = rest[pps + 1:]
    j = pl.program_id(1)
    t = iq_ref.shape[0]
    shape = (t, LANES)
    nc = n_pages + 1

    @pl.when(j == 0)
    def _():
        iq = iq_ref[...]
        ikiw = ikiw_ref[...]
        for h in range(H_IDX):
            iq2[h * t:(h + 1) * t, :] = iq[:, h * D_IDX:(h + 1) * D_IDX].astype(BF16)
            wb[h * t:(h + 1) * t, :] = jnp.broadcast_to(ikiw[:, D_IDX + h:D_IDX + h + 1], shape)

    def scores(ik_bf):
        s = lax.dot_general(iq2[...], ik_bf, (((1,), (1,)), ((), ())), preferred_element_type=F32)
        s = jnp.maximum(s, 0.0) * wb[...]
        return jnp.sum(s.reshape(H_IDX, t, LANES), axis=0) * IDX_SCALE

    for u in range(pps):
        sc[j * pps + u] = _score_key(scores(page_refs[u][...].astype(BF16)))

    @pl.when(j == pl.num_programs(1) - 1)
    def _():
        row = lax.broadcasted_iota(I32, shape, 0)
        col = lax.broadcasted_iota(I32, shape, 1)
        ik_new = jnp.concatenate([ikiw_ref[:, 0:D_IDX], jnp.zeros((LANES - t, D_IDX), F32)], axis=0)
        s_new = jnp.where(col <= row, scores(ik_new.astype(BF16)), NEG)
        sc[n_pages] = jnp.where(col < t, _score_key(s_new), INT_MIN)
        kf = float(topk)
        thr = _kth_largest(sc, nc, kf, shape)

        def causal(kc):
            return (kc < n_pages) | (col <= row)

        _select_mask(sc, mb_ref, nc, kf, thr, shape, causal)


def _dsa_sample_attn_kernel(pps, n_pages, pt_ref, q_ref, kn_ref, vn_ref, mb_ref, mbn_ref, bias_ref, *rest):
    k_refs = rest[:pps]
    v_refs = rest[pps:2 * pps]
    o_ref = rest[2 * pps]
    qbd, acc, m_s, l_s = rest[2 * pps + 1:]
    j = pl.program_id(1)
    t = q_ref.shape[0]
    rows = H_A * t

    def blockdiag(x):
        r = lax.broadcasted_iota(I32, (rows, A_W), 0) // t
        c = lax.broadcasted_iota(I32, (rows, A_W), 1) // HD_A
        return jnp.where(r == c, jnp.tile(x, (H_A, 1)), 0.0)

    @pl.when(j == 0)
    def _():
        qbd[...] = blockdiag(q_ref[...] * (HD_A ** -0.5)).astype(BF16)
        m_s[...] = jnp.full(m_s.shape, NEG, F32)
        l_s[...] = jnp.zeros(l_s.shape, F32)
        acc[...] = jnp.zeros(acc.shape, F32)

    def step(kp, vp, bias, mbc):
        lg = lax.dot_general(qbd[...], kp, (((1,), (1,)), ((), ())), preferred_element_type=F32)
        lg = lg + bias + jnp.tile(mbc, (H_A, 1))
        m_old = m_s[...]
        m_new = jnp.maximum(m_old, jnp.max(lg, axis=1, keepdims=True))
        p = jnp.exp(lg - m_new)
        alpha = jnp.exp(m_old - m_new)
        l_s[...] = alpha * l_s[...] + jnp.sum(p, axis=1, keepdims=True)
        acc[...] = alpha * acc[...] + jnp.dot(p.astype(BF16), vp, preferred_element_type=F32)
        m_s[...] = m_new

    for u in range(pps):
        bias = jnp.where(j * pps + u == n_pages - 1, bias_ref[1], bias_ref[0])
        step(k_refs[u][...].astype(BF16), v_refs[u][...].astype(BF16), bias, mb_ref[u])

    @pl.when(j == pl.num_programs(1) - 1)
    def _():
        pad = jnp.zeros((LANES - t, A_W), F32)
        kn = jnp.concatenate([kn_ref[...], pad], axis=0).astype(BF16)
        vn = jnp.concatenate([vn_ref[...], pad], axis=0).astype(BF16)
        step(kn, vn, bias_ref[2], mbn_ref[0])
        res = blockdiag_sum(acc[...] / l_s[...], t)
        o_ref[...] = res.astype(o_ref.dtype)


def blockdiag_sum(x, t):
    c = lax.broadcasted_iota(I32, (t, A_W), 1) // HD_A
    out = jnp.zeros((t, A_W), F32)
    for h in range(H_A):
        out = out + jnp.where(c == h, x[h * t:(h + 1) * t, :], 0.0)
    return out


def dsa_sample(proj, layer, cache_k, cache_v, cache_idx_k, page_table, bias_s, pps_sel=16, pps_att=8):
    db, t, _ = proj.shape
    n_pages = page_table.shape[1]
    nc = n_pages + 1
    topk = min(TOPK_MAX, (n_pages * PAGE + t) // 4)
    n_pool = cache_k.shape[1]
    ck = cache_k.reshape(cache_k.shape[0], n_pool, PAGE, A_W)
    cv = cache_v.reshape(cache_v.shape[0], n_pool, PAGE, A_W)
    pt = page_table.reshape(-1)

    def page_spec(width, pps, u):
        return pl.BlockSpec((None, None, PAGE, width),
                            lambda b, j, ptr: (layer, ptr[b * n_pages + j * pps + u], 0, 0))

    sel = pl.pallas_call(
        functools.partial(_dsa_sample_select_kernel, pps_sel, n_pages, topk),
        grid_spec=pltpu.PrefetchScalarGridSpec(
            num_scalar_prefetch=1,
            grid=(db, n_pages // pps_sel),
            in_specs=[pl.BlockSpec((None, t, A_W), lambda b, j, ptr: (b, 0, CB_IQ)),
                      pl.BlockSpec((None, t, LANES), lambda b, j, ptr: (b, 0, CB_IK))]
                     + [page_spec(D_IDX, pps_sel, u) for u in range(pps_sel)],
            out_specs=pl.BlockSpec((None, nc, t, LANES), lambda b, j, ptr: (b, 0, 0, 0)),
            scratch_shapes=[pltpu.VMEM((H_IDX * t, D_IDX), BF16), pltpu.VMEM((H_IDX * t, LANES), F32),
                            pltpu.VMEM((nc, t, LANES), I32)]),
        out_shape=jax.ShapeDtypeStruct((db, nc, t, LANES), F32),
        compiler_params=_cparams(("parallel", "arbitrary")),
        name="dsa_sample_select",
    )(pt, proj, proj, *([cache_idx_k] * pps_sel))

    return pl.pallas_call(
        functools.partial(_dsa_sample_attn_kernel, pps_att, n_pages),
        grid_spec=pltpu.PrefetchScalarGridSpec(
            num_scalar_prefetch=1,
            grid=(db, n_pages // pps_att),
            in_specs=[pl.BlockSpec((None, t, A_W), lambda b, j, ptr: (b, 0, CB_Q)),
                      pl.BlockSpec((None, t, A_W), lambda b, j, ptr: (b, 0, CB_K)),
                      pl.BlockSpec((None, t, A_W), lambda b, j, ptr: (b, 0, CB_V)),
                      pl.BlockSpec((None, pps_att, t, LANES), lambda b, j, ptr: (b, j, 0, 0)),
                      pl.BlockSpec((None, 1, t, LANES), lambda b, j, ptr: (b, n_pages, 0, 0)),
                      pl.BlockSpec(bias_s.shape, lambda b, j, ptr: (0, 0, 0))]
                     + [page_spec(A_W, pps_att, u) for u in range(pps_att)] * 2,
            out_specs=pl.BlockSpec((None, t, A_W), lambda b, j, ptr: (b, 0, 0)),
            scratch_shapes=[pltpu.VMEM((H_A * t, A_W), BF16), pltpu.VMEM((H_A * t, A_W), F32),
                            pltpu.VMEM((H_A * t, 1), F32), pltpu.VMEM((H_A * t, 1), F32)]),
        out_shape=jax.ShapeDtypeStruct((db, t, A_W), BF16),
        compiler_params=_cparams(("parallel", "arbitrary")),
        name="dsa_sample_attn",
    )(pt, proj, proj, proj, sel, sel, bias_s, *([ck] * pps_att), *([cv] * pps_att))


def _mm(a, b):
    return jnp.dot(a.astype(BF16), b.astype(BF16), preferred_element_type=F32)


def _mm_nt(a, b):
    return lax.dot_general(a.astype(BF16), b.astype(BF16), (((1,), (1,)), ((), ())), preferred_element_type=F32)


def _split3(x):
    hi = x.astype(BF16)
    r1 = x - hi.astype(F32)
    mid = r1.astype(BF16)
    lo = (r1 - mid.astype(F32)).astype(BF16)
    return hi, mid, lo


def _rwkv_kernel(has_state, t_valid, r_ref, k_ref, v_ref, wa_ref, g_ref, sr_ref, sk_ref, sv_ref, swa_ref, sg_ref,
                 mr_ref, mk_ref, mv_ref, mwa_ref, mg_ref, w2_ref, a2_ref, g2_ref, vec_ref, *rest):
    if has_state:
        s0_ref, y_ref, so_ref, pr, pk, pv, pwa, pg, st = rest
    else:
        y_ref, so_ref, pr, pk, pv, pwa, pg, st = rest
    c = pl.program_id(2)
    C = RW_C
    tb = r_ref.shape[0]

    @pl.when(c == 0)
    def _():
        if has_state:
            st[...] = s0_ref[...]
        else:
            st[...] = jnp.zeros(st.shape, F32)

    def shifted(p_ref, prev_scr, shift_ref, mu_ref):
        p = p_ref[...]
        if tb < C:
            p = jnp.concatenate([p, jnp.zeros((C - tb, p.shape[1]), F32)], axis=0)
        prev = jnp.where(c == 0, shift_ref[...], prev_scr[...])
        rowi = lax.broadcasted_iota(I32, p.shape, 0)
        ps = jnp.where(rowi == 0, prev, pltpu.roll(p, 1, 0))
        prev_scr[...] = p[C - 1:C, :]
        return p + (ps - p) * mu_ref[...]

    xr = shifted(r_ref, pr, sr_ref, mr_ref)
    xk = shifted(k_ref, pk, sk_ref, mk_ref)
    xv = shifted(v_ref, pv, sv_ref, mv_ref)
    xwa = shifted(wa_ref, pwa, swa_ref, mwa_ref)
    xg = shifted(g_ref, pg, sg_ref, mg_ref)

    vec = vec_ref[...]
    w0, a0, kkp, kap, rkp, lnw, lnb = (vec[n:n + 1, :] for n in range(7))
    zw = w0 + _mm(jnp.tanh(xwa[:, 0:W_LORA]), w2_ref[...])
    w_log = -(jnp.maximum(-zw, 0.0) + jnp.log(1.0 + jnp.exp(-jnp.abs(zw)))) - 0.5
    ld = -jnp.exp(w_log)
    za = a0 + _mm(xwa[:, W_LORA:W_LORA + A_LORA], a2_ref[...])
    a = 1.0 / (1.0 + jnp.exp(-za))
    g = _mm(1.0 / (1.0 + jnp.exp(-xg)), g2_ref[...])

    valid = lax.broadcasted_iota(I32, (C, LANES), 0) < t_valid
    valid_h = lax.broadcasted_iota(I32, (C, HD_B), 0) < t_valid
    if t_valid < C:
        ld = jnp.where(valid, ld, 0.0)
        xv = jnp.where(valid, xv, 0.0)
    ri = lax.broadcasted_iota(I32, (C, C), 0)
    ci = lax.broadcasted_iota(I32, (C, C), 1)
    strict = ci < ri
    incl = ci <= ri
    tril = incl.astype(BF16)
    hi, mid, lo = _split3(ld)
    lgc = (jnp.dot(tril, hi, preferred_element_type=F32) + jnp.dot(tril, mid, preferred_element_type=F32)
           + jnp.dot(tril, lo, preferred_element_type=F32))

    for j in range(2):
        sl = slice(j * HD_B, (j + 1) * HD_B)
        r_h, k_h, v_h, a_h, ld_h, lg_h = xr[:, sl], xk[:, sl], xv[:, sl], a[:, sl], ld[:, sl], lgc[:, sl]
        kk = k_h * kkp[:, sl]
        kk = kk / jnp.maximum(jnp.sqrt(jnp.sum(kk * kk, axis=1, keepdims=True)), 1e-12)
        k2 = k_h * (1.0 + (a_h - 1.0) * kap[:, sl])
        if t_valid < C:
            kk = jnp.where(valid_h, kk, 0.0)
            k2m = jnp.where(valid_h, k2, 0.0)
        else:
            k2m = k2
        lg_end = lg_h[C - 1:C, :]
        ginv = jnp.exp(-lg_h)
        e_end = jnp.exp(lg_end - lg_h)
        kb_ = kk * a_h
        al = kk * jnp.exp(lg_h - ld_h)
        be = kb_ * ginv
        kt = k2m * ginv
        rt = r_h * jnp.exp(lg_h)
        lhs = jnp.concatenate([al, rt], axis=0)
        gb = _mm_nt(lhs, be)
        gk = _mm_nt(lhs, kt)
        lb = jnp.where(strict, gb[0:C], 0.0)
        lk = jnp.where(strict, gk[0:C], 0.0)
        mb_ = jnp.where(incl, gb[C:2 * C], 0.0)
        mk_ = jnp.where(incl, gk[C:2 * C], 0.0)
        s0 = st[j]
        p0 = _mm_nt(lhs, s0)
        u = p0[0:C] + _mm(lk, v_h)
        qn = -lb
        u = u + _mm(qn, u)
        for _ in range(int(math.log2(C)) - 1):
            qn = _mm(qn, qn)
            u = u + _mm(qn, u)
        y = p0[C:2 * C] - _mm(mb_, u) + _mm(mk_, v_h)
        upd = lax.dot_general(jnp.concatenate([-u, v_h], axis=0).astype(BF16),
                              jnp.concatenate([kb_ * e_end, k2m * e_end], axis=0).astype(BF16),
                              (((0,), (0,)), ((), ())), preferred_element_type=F32)
        st[j] = s0 * jnp.exp(lg_end) + upd
        mu = jnp.mean(y, axis=1, keepdims=True)
        var = jnp.mean(jnp.square(y - mu), axis=1, keepdims=True)
        yn = (y - mu) * lax.rsqrt(var + GN_EPS) * lnw[:, sl] + lnb[:, sl]
        bonus = jnp.sum(r_h * k2 * rkp[:, sl], axis=1, keepdims=True) * v_h
        out = (yn + bonus) * g[:, sl]
        y_ref[:, sl] = out[0:tb].astype(y_ref.dtype)

    @pl.when(c == pl.num_programs(2) - 1)
    def _():
        so_ref[...] = st[...]


def rwkv_mix(proj, shift_full, mu_full, w2, a2, g2p, vecs, s0):
    b, t, _ = proj.shape
    tb = min(t, RW_C)
    nch = t // tb
    hp = H_B // 2
    has_state = s0 is not None

    def pspec(cb0, width=LANES):
        return pl.BlockSpec((None, tb, width), lambda bi, h, c: (bi, c, cb0 + (h if width == LANES and cb0 < CB_WA else 0)))

    def sspec(cb0, width=LANES):
        return pl.BlockSpec((None, 1, width), lambda bi, h, c: (bi, 0, cb0 + (h if width == LANES and cb0 < CB_WA else 0)))

    def mspec(cb0, width=LANES):
        return pl.BlockSpec((1, width), lambda bi, h, c: (0, cb0 + (h if width == LANES and cb0 < CB_WA else 0)))

    cbs = [(CB_RR, LANES), (CB_RK, LANES), (CB_RV, LANES), (CB_WA, LANES), (CB_G, 256)]
    in_specs = ([pspec(*x) for x in cbs] + [sspec(*x) for x in cbs] + [mspec(*x) for x in cbs]
                + [pl.BlockSpec((W_LORA, LANES), lambda bi, h, c: (0, h)),
                   pl.BlockSpec((A_LORA, LANES), lambda bi, h, c: (0, h)),
                   pl.BlockSpec((256, LANES), lambda bi, h, c: (0, h)),
                   pl.BlockSpec((SUBLANES, LANES), lambda bi, h, c: (0, h))])
    args = [proj] * 5 + [shift_full] * 5 + [mu_full] * 5 + [w2, a2, g2p, vecs]
    if has_state:
        in_specs.append(pl.BlockSpec((None, 2, HD_B, HD_B), lambda bi, h, c: (bi, h, 0, 0)))
        args.append(s0)
    return pl.pallas_call(
        functools.partial(_rwkv_kernel, has_state, tb),
        grid=(b, hp, nch),
        in_specs=in_specs,
        out_specs=[pl.BlockSpec((None, tb, LANES), lambda bi, h, c: (bi, c, h)),
                   pl.BlockSpec((None, 2, HD_B, HD_B), lambda bi, h, c: (bi, h, 0, 0))],
        out_shape=[jax.ShapeDtypeStruct((b, t, B_W), BF16),
                   jax.ShapeDtypeStruct((b, H_B, HD_B, HD_B), F32)],
        scratch_shapes=[pltpu.VMEM((1, LANES), F32)] * 4 + [pltpu.VMEM((1, 256), F32),
                                                            pltpu.VMEM((2, HD_B, HD_B), F32)],
        compiler_params=_cparams(("parallel", "parallel", "arbitrary")),
        name="rwkv_mix",
    )(*args)


def _gelu(x):
    return 0.5 * x * (1.0 + jnp.tanh(math.sqrt(2.0 / math.pi) * (x + 0.044715 * (x * x * x))))


def _gmlp_kernel(pu_ref, pv_ref, lnw_ref, lnb_ref, ws_ref, bst_ref, y_ref, gv_ref):
    tc = pu_ref.shape[0]
    u = _gelu(pu_ref[...])
    vf = _gelu(pv_ref[...])
    mu = jnp.mean(vf, axis=1, keepdims=True)
    var = jnp.mean(jnp.square(vf - mu), axis=1, keepdims=True)
    v = (vf - mu) * lax.rsqrt(var + 1e-5) * lnw_ref[...] + lnb_ref[...]
    gv_ref[...] = v
    ri = lax.broadcasted_iota(I32, (CHUNK, CHUNK), 0)
    ci = lax.broadcasted_iota(I32, (CHUNK, CHUNK), 1)
    if tc < CHUNK:
        v = jnp.concatenate([v, jnp.zeros((CHUNK - tc, C_W), F32)], axis=0)
    vb = v.astype(BF16)
    for g in range(G_C):
        sl = slice(g * CD, (g + 1) * CD)
        wm = jnp.where(ci <= ri, ws_ref[g], 0.0).astype(BF16)
        sp = jnp.dot(wm, vb[:, sl], preferred_element_type=F32)[0:tc] + bst_ref[0:tc, g:g + 1]
        y_ref[:, sl] = (u[:, sl] * sp).astype(y_ref.dtype)


def gmlp_mix(proj, lnw, lnb, ws, bst):
    b, t, _ = proj.shape
    tc = min(t, CHUNK)
    return pl.pallas_call(
        _gmlp_kernel,
        grid=(b, t // tc),
        in_specs=[pl.BlockSpec((None, tc, C_W), lambda bi, c: (bi, c, CB_U)),
                  pl.BlockSpec((None, tc, C_W), lambda bi, c: (bi, c, CB_PV)),
                  pl.BlockSpec((1, C_W), lambda bi, c: (0, 0)),
                  pl.BlockSpec((1, C_W), lambda bi, c: (0, 0)),
                  pl.BlockSpec(ws.shape, lambda bi, c: (0, 0, 0)),
                  pl.BlockSpec(bst.shape, lambda bi, c: (0, 0))],
        out_specs=[pl.BlockSpec((None, tc, C_W), lambda bi, c: (bi, c, 0)),
                   pl.BlockSpec((None, tc, C_W), lambda bi, c: (bi, c, 0))],
        out_shape=[jax.ShapeDtypeStruct((b, t, C_W), BF16), jax.ShapeDtypeStruct((b, t, C_W), F32)],
        compiler_params=_cparams(("parallel", "parallel")),
        name="gmlp_mix",
    )(proj, proj, lnw.reshape(1, C_W), lnb.reshape(1, C_W), ws, bst)


def _layer(x, lw, attend, shift_full, s0, tm):
    b, t, d = x.shape
    x2 = x.reshape(b * t, d)
    proj = norm_matmul(x2, lw['norm_mix'], lw['w_in'], tm, 512).reshape(b, t, PROJ_P)
    ya = attend(proj)
    yb, wkv = rwkv_mix(proj, shift_full, lw['mu_full'], lw['w2'], lw['a2'], lw['g2p'], lw['vecs'], s0)
    yc, gv = gmlp_mix(proj, lw['gmlp_ln_w'], lw['gmlp_ln_b'], lw['gmlp_ws'], lw['gmlp_bst'])
    h = out_proj(x2, ya.reshape(b * t, A_W), yb.reshape(b * t, B_W), yc.reshape(b * t, C_W), lw['w_out'], tm)
    y = ffn(h, lw['norm_ffn'], lw['ffn_gate'], lw['ffn_up'], lw['ffn_down'], tm, 512)
    k = proj[:, :, A_W:2 * A_W].reshape(b, t, H_A, HD_A)
    v = proj[:, :, 2 * A_W:3 * A_W].reshape(b, t, H_A, HD_A)
    ik = proj[:, :, OFF_IK:OFF_IK + D_IDX]
    shift_new = proj[:, t - 1, OFF_R:OFF_R + RWKV_PROJ_W]
    return y.reshape(b, t, d), k, v, ik, shift_new, wkv, gv


def kernel(x_prompt, x_sample, cache_k, cache_v, cache_idx_k, state_rwkv_shift, state_rwkv_wkv, page_table, rel_bias, norm_mix, w_in, rwkv_mu, rwkv_w0, rwkv_w2, rwkv_a0, rwkv_a2, rwkv_g2, rwkv_kk, rwkv_ka, rwkv_rk, rwkv_ln_w, rwkv_ln_b, gmlp_ln_w, gmlp_ln_b, gmlp_ws, gmlp_b, w_out, norm_ffn, ffn_gate, ffn_up, ffn_down, norm_final):
    depth = w_in.shape[0]
    nbp, seq, d = x_prompt.shape
    db, dec_seq, _ = x_sample.shape
    assert d == D_MODEL and seq % 512 == 0 and dec_seq == SUBLANES and w_in.shape[2] == SEG_A_END + RWKV_PROJ_W + 2 * C_W

    w_in_p = _pad_cols(w_in).astype(BF16)
    w_out_b = w_out.astype(BF16)
    wg_b, wu_b, wd_b = ffn_gate.astype(BF16), ffn_up.astype(BF16), ffn_down.astype(BF16)
    mu_full = _pad_rwkv_cols(rwkv_mu)
    g2p = jnp.concatenate([rwkv_g2, jnp.zeros((depth, 256 - G_LORA, B_W), F32)], axis=1)
    vecs = jnp.stack([rwkv_w0, rwkv_a0, rwkv_kk, rwkv_ka, rwkv_rk, rwkv_ln_w, rwkv_ln_b,
                      jnp.zeros_like(rwkv_w0)], axis=1)
    bst = jnp.swapaxes(gmlp_b, 1, 2)
    shift_s = _pad_rwkv_cols(state_rwkv_shift)[:, :, None, :]
    shift_p = jnp.zeros((nbp, 1, PROJ_P), F32)
    bias_p, bias_s = bias_tiles(rel_bias, dec_seq)

    xp, xs = x_prompt, x_sample
    outs = [[] for _ in range(11)]
    for l in range(depth):
        lw = {'norm_mix': norm_mix[l], 'w_in': w_in_p[l], 'mu_full': mu_full[l:l + 1], 'w2': rwkv_w2[l],
              'a2': rwkv_a2[l], 'g2p': g2p[l], 'vecs': vecs[l], 'gmlp_ln_w': gmlp_ln_w[l],
              'gmlp_ln_b': gmlp_ln_b[l], 'gmlp_ws': gmlp_ws[l], 'gmlp_bst': bst[l], 'w_out': w_out_b[l],
              'norm_ffn': norm_ffn[l], 'ffn_gate': wg_b[l], 'ffn_up': wu_b[l], 'ffn_down': wd_b[l]}
        attend_p = functools.partial(dsa_prompt, bias_p=bias_p, rel_bias=rel_bias)
        xp, kp, vp, ikp, shp, wkvp, _ = _layer(xp, lw, attend_p, shift_p, None, 512)
        attend_s = functools.partial(dsa_sample, layer=l, cache_k=cache_k, cache_v=cache_v,
                                     cache_idx_k=cache_idx_k, page_table=page_table, bias_s=bias_s)
        xs, k_s, v_s, ik_s, sh_s, wkv_s, gv_s = _layer(xs, lw, attend_s, shift_s[l], state_rwkv_wkv[l],
                                                       db * dec_seq)
        for lst, val in zip(outs, (kp, vp, ikp, shp, wkvp, k_s, v_s, ik_s, sh_s, wkv_s, gv_s)):
            lst.append(val)
    y_prompt = rms_final(xp.reshape(nbp * seq, d), norm_final, 512).reshape(nbp, seq, d)
    y_sample = rms_final(xs.reshape(db * dec_seq, d), norm_final, db * dec_seq).reshape(db, dec_seq, d)
    return (y_prompt, y_sample) + tuple(jnp.stack(o) for o in outs)
```

```python
import functools
import math

import numpy as np
import jax
import jax.numpy as jnp
from jax import lax
from jax.experimental import pallas as pl
from jax.experimental.pallas import tpu as pltpu

F32 = jnp.float32
BF16 = jnp.bfloat16
I32 = jnp.int32

LANES = 128
SUBLANES = 8
VMEM_LIMIT = 56 * 1024 * 1024

D_MODEL = 2048
HD_A = 64
H_A = 8
A_W = H_A * HD_A
H_IDX = 8
D_IDX = 64
IDX_SCALE = (D_IDX ** -0.5) * (H_IDX ** -0.5)
TOPK_MAX = 256
N_BUCKETS = 32
MAX_DISTANCE = 128
HD_B = 64
B_W = 1024
H_B = B_W // HD_B
W_LORA = 64
A_LORA = 64
G_LORA = 160
RWKV_PROJ_W = 3 * B_W + W_LORA + A_LORA + G_LORA
GN_EPS = 64e-5
C_W = 512
CHUNK = 128
CD = 64
G_C = C_W // CD
D_FF = 5632
NORM_EPS = 1e-6
NEG = -1e30
PAGE = 128
INT_MIN = -(2 ** 31)

ORIG_IK = 3 * A_W + H_IDX * D_IDX
ORIG_R = ORIG_IK + D_IDX + H_IDX
ORIG_GM = ORIG_R + RWKV_PROJ_W
OFF_R = 2048
OFF_IK = OFF_R + 3 * B_W
OFF_WA = OFF_IK + LANES
OFF_G = OFF_WA + W_LORA + A_LORA
OFF_GM = OFF_G + 256
PROJ_P = OFF_GM + 2 * C_W
CB_Q, CB_K, CB_V, CB_IQ = 0, 1, 2, 3
CB_IK = OFF_IK // LANES
CB_WA = OFF_WA // LANES
CB_G = OFF_G // 256
CB_U = OFF_GM // C_W
CB_PV = CB_U + 1
RW_C = 64
assert OFF_R % B_W == 0 and OFF_G % 256 == 0 and OFF_GM % C_W == 0


def _cparams(sem):
    return pltpu.CompilerParams(dimension_semantics=sem, vmem_limit_bytes=VMEM_LIMIT)


def _pad_cols(a):
    z = lambda n: jnp.zeros(a.shape[:-1] + (n,), a.dtype)
    lora = ORIG_R + 3 * B_W
    return jnp.concatenate([a[..., :ORIG_IK], a[..., ORIG_R:lora], a[..., ORIG_IK:ORIG_R], z(OFF_WA - OFF_IK - D_IDX - H_IDX),
                            a[..., lora:ORIG_GM], z(OFF_GM - OFF_G - G_LORA), a[..., ORIG_GM:]], axis=-1)


def _pad_rwkv_cols(a):
    z = lambda n: jnp.zeros(a.shape[:-1] + (n,), a.dtype)
    return jnp.concatenate([z(OFF_R), a[..., :3 * B_W], z(LANES), a[..., 3 * B_W:], z(PROJ_P - OFF_G - G_LORA)], axis=-1)


def _rwkv_cols(p):
    return jnp.concatenate([p[..., OFF_R:OFF_R + 3 * B_W], p[..., OFF_WA:OFF_WA + RWKV_PROJ_W - 3 * B_W]], axis=-1)


def _norm_matmul_kernel(x_ref, g_ref, w_ref, o_ref, xn_ref):
    @pl.when(pl.program_id(1) == 0)
    def _():
        x = x_ref[...]
        y = x * lax.rsqrt(jnp.mean(x * x, axis=-1, keepdims=True) + NORM_EPS)
        xn_ref[...] = (y * g_ref[...]).astype(BF16)

    o_ref[...] = jnp.dot(xn_ref[...], w_ref[...], preferred_element_type=F32)


def norm_matmul(x, g, w, tm, tn):
    m, d = x.shape
    n = w.shape[1]
    return pl.pallas_call(
        _norm_matmul_kernel,
        grid=(m // tm, n // tn),
        in_specs=[pl.BlockSpec((tm, d), lambda i, j: (i, 0)),
                  pl.BlockSpec((1, d), lambda i, j: (0, 0)),
                  pl.BlockSpec((d, tn), lambda i, j: (0, j))],
        out_specs=pl.BlockSpec((tm, tn), lambda i, j: (i, j)),
        out_shape=jax.ShapeDtypeStruct((m, n), F32),
        scratch_shapes=[pltpu.VMEM((tm, d), BF16)],
        compiler_params=_cparams(("parallel", "arbitrary")),
        name="norm_matmul",
    )(x, g.reshape(1, d), w)


def _out_proj_kernel(x_ref, ya_ref, yb_ref, yc_ref, w_ref, o_ref):
    acc = jnp.dot(ya_ref[...], w_ref[0:A_W, :], preferred_element_type=F32)
    acc += jnp.dot(yb_ref[...], w_ref[A_W:A_W + B_W, :], preferred_element_type=F32)
    acc += jnp.dot(yc_ref[...], w_ref[A_W + B_W:, :], preferred_element_type=F32)
    o_ref[...] = x_ref[...] + acc


def out_proj(x, ya, yb, yc, w, tm):
    m, d = x.shape
    return pl.pallas_call(
        _out_proj_kernel,
        grid=(m // tm,),
        in_specs=[pl.BlockSpec((tm, d), lambda i: (i, 0)),
                  pl.BlockSpec((tm, A_W), lambda i: (i, 0)),
                  pl.BlockSpec((tm, B_W), lambda i: (i, 0)),
                  pl.BlockSpec((tm, C_W), lambda i: (i, 0)),
                  pl.BlockSpec(w.shape, lambda i: (0, 0))],
        out_specs=pl.BlockSpec((tm, d), lambda i: (i, 0)),
        out_shape=jax.ShapeDtypeStruct((m, d), F32),
        compiler_params=_cparams(("parallel",)),
        name="out_proj",
    )(x, ya, yb, yc, w)


def _ffn_kernel(h_ref, g_ref, wg_ref, wu_ref, wd_ref, o_ref, hn_ref, acc_ref):
    f = pl.program_id(1)

    @pl.when(f == 0)
    def _():
        x = h_ref[...]
        y = x * lax.rsqrt(jnp.mean(x * x, axis=-1, keepdims=True) + NORM_EPS)
        hn_ref[...] = (y * g_ref[...]).astype(BF16)
        acc_ref[...] = jnp.zeros_like(acc_ref)

    hn = hn_ref[...]
    gate = jnp.dot(hn, wg_ref[...], preferred_element_type=F32)
    up = jnp.dot(hn, wu_ref[...], preferred_element_type=F32)
    act = (gate / (1.0 + jnp.exp(-gate))) * up
    acc_ref[...] += jnp.dot(act.astype(BF16), wd_ref[...], preferred_element_type=F32)

    @pl.when(f == pl.num_programs(1) - 1)
    def _():
        o_ref[...] = h_ref[...] + acc_ref[...]


def ffn(h, g, wg, wu, wd, tm, tf):
    m, d = h.shape
    nf = wg.shape[1]
    return pl.pallas_call(
        _ffn_kernel,
        grid=(m // tm, nf // tf),
        in_specs=[pl.BlockSpec((tm, d), lambda i, f: (i, 0)),
                  pl.BlockSpec((1, d), lambda i, f: (0, 0)),
                  pl.BlockSpec((d, tf), lambda i, f: (0, f)),
                  pl.BlockSpec((d, tf), lambda i, f: (0, f)),
                  pl.BlockSpec((tf, d), lambda i, f: (f, 0))],
        out_specs=pl.BlockSpec((tm, d), lambda i, f: (i, 0)),
        out_shape=jax.ShapeDtypeStruct((m, d), F32),
        scratch_shapes=[pltpu.VMEM((tm, d), BF16), pltpu.VMEM((tm, d), F32)],
        compiler_params=_cparams(("parallel", "arbitrary")),
        name="ffn",
    )(h, g.reshape(1, d), wg, wu, wd)


def _rms_kernel(x_ref, g_ref, o_ref):
    x = x_ref[...]
    o_ref[...] = x * lax.rsqrt(jnp.mean(x * x, axis=-1, keepdims=True) + NORM_EPS) * g_ref[...]


def rms_final(x, g, tm):
    m, d = x.shape
    return pl.pallas_call(
        _rms_kernel,
        grid=(m // tm,),
        in_specs=[pl.BlockSpec((tm, d), lambda i: (i, 0)), pl.BlockSpec((1, d), lambda i: (0, 0))],
        out_specs=pl.BlockSpec((tm, d), lambda i: (i, 0)),
        out_shape=jax.ShapeDtypeStruct((m, d), F32),
        compiler_params=_cparams(("parallel",)),
        name="rms_final",
    )(x, g.reshape(1, d))


def _bucket_np(dist):
    max_exact = N_BUCKETS // 2
    n = np.maximum(dist, 0)
    nf = np.maximum(n, 1).astype(np.float64)
    large = max_exact + (np.log(nf / max_exact) / math.log(MAX_DISTANCE / max_exact)
                         * (N_BUCKETS - max_exact)).astype(np.int32)
    return np.where(n < max_exact, n, np.minimum(large, N_BUCKETS - 1)).astype(np.int32)


def _bias_tiles_kernel(rb_ref, bp_ref, bs_ref, op_ref, os_ref):
    for d in range(2):
        bk = bp_ref[d]
        for h in range(H_A):
            t = jnp.zeros(bk.shape, F32)
            for b in range(N_BUCKETS):
                t = jnp.where(bk == b, rb_ref[b, h], t)
            op_ref[d, h] = t
    for d in range(3):
        bk = bs_ref[d]
        for h in range(H_A):
            t = jnp.zeros(bk.shape, F32)
            for b in range(N_BUCKETS):
                t = jnp.where(bk == b, rb_ref[b, h], t)
            os_ref[d, h * SUBLANES:(h + 1) * SUBLANES, :] = t


def bias_tiles(rel_bias, dec_seq):
    r = np.arange(LANES)[:, None]
    c = np.arange(LANES)[None, :]
    bp = np.stack([_bucket_np(r - c), _bucket_np(LANES + r - c)])
    t = np.arange(dec_seq)[:, None]
    bs = np.stack([_bucket_np(np.full((dec_seq, LANES), 4 * MAX_DISTANCE)),
                   _bucket_np(PAGE + t - c), _bucket_np(t - c)])
    vm = pl.BlockSpec(memory_space=pltpu.VMEM)
    return pl.pallas_call(
        _bias_tiles_kernel,
        in_specs=[pl.BlockSpec(memory_space=pltpu.SMEM), vm, vm],
        out_specs=[vm, vm],
        out_shape=[jax.ShapeDtypeStruct((2, H_A, LANES, LANES), F32),
                   jax.ShapeDtypeStruct((3, H_A * dec_seq, LANES), F32)],
        name="bias_tiles",
    )(rel_bias, jnp.asarray(bp), jnp.asarray(bs))


def _score_key(s):
    s = jnp.where(s == 0.0, 0.0, s)
    bits = lax.bitcast_convert_type(s, I32)
    return jnp.where(bits < 0, bits ^ 0x7FFFFFFF, bits)


def _kth_largest(sc_ref, n_chunks, kf, shape):
    def count_ge(cand):
        def body(kc, a):
            return a + (sc_ref[kc] >= cand).astype(F32)

        a = lax.fori_loop(0, n_chunks, body, jnp.zeros(shape, F32))
        return jnp.sum(a, axis=1, keepdims=True)

    zero = jnp.zeros(shape, I32)
    t0 = jnp.where(count_ge(zero) >= kf, zero, jnp.full(shape, INT_MIN, I32))

    def bit_body(it, t):
        cand = t | jnp.left_shift(jnp.int32(1), 30 - it)
        return jnp.where(count_ge(cand) >= kf, cand, t)

    return lax.fori_loop(0, 31, bit_body, t0)


def _select_mask(sc_ref, mb_ref, n_chunks, kf, thr, shape, causal_fn):
    def cnt_body(kc, carry):
        key = sc_ref[kc]
        return carry[0] + (key > thr).astype(F32), carry[1] + (key >= thr).astype(F32)

    zeros = jnp.zeros(shape, F32)
    cgt, cge = lax.fori_loop(0, n_chunks, cnt_body, (zeros, zeros))
    need = kf - jnp.sum(cgt, axis=1, keepdims=True)
    cge = jnp.sum(cge, axis=1, keepdims=True)

    def exact_k():
        def body(kc, carry):
            mb_ref[kc] = jnp.where((sc_ref[kc] >= thr) & causal_fn(kc), 0.0, NEG)
            return carry

        lax.fori_loop(0, n_chunks, body, 0)

    def with_ties():
        rr = lax.broadcasted_iota(I32, (LANES, LANES), 0)
        cc = lax.broadcasted_iota(I32, (LANES, LANES), 1)
        tri = (rr <= cc).astype(BF16)

        def body(kc, offs):
            key = sc_ref[kc]
            eq = key == thr
            pre = jnp.dot(eq.astype(BF16), tri, preferred_element_type=F32) + offs
            sel = (key > thr) | (eq & (pre <= need))
            mb_ref[kc] = jnp.where(sel & causal_fn(kc), 0.0, NEG)
            return jnp.broadcast_to(pre[:, LANES - 1:LANES], shape)

        lax.fori_loop(0, n_chunks, body, zeros)

    lax.cond(jnp.max(jnp.abs(cge - kf)) > 0.5, with_ties, exact_k)


def _dsa_prompt_kernel(topk, q_ref, iq_ref, iw_ref, k_ref, v_ref, ik_ref, bias_ref, rb_ref, o_ref,
                       kb, vb, ikd, qm, iqm, iwb, sc, mb, lgs, acc, m_s, l_s):
    i = pl.program_id(1)
    tq = q_ref.shape[0]
    shape = (tq, LANES)
    nt = (((1,), (1,)), ((), ()))

    @pl.when(i == 0)
    def _():
        kb[...] = k_ref[...].astype(BF16)
        vb[...] = v_ref[...].astype(BF16)
        ik = ik_ref[:, 0:D_IDX].astype(BF16)
        ikd[...] = jnp.concatenate([ik, ik], axis=1)

    row = lax.broadcasted_iota(I32, shape, 0) + i * tq
    col = lax.broadcasted_iota(I32, shape, 1)
    lane_lo = col < HD_A
    n_chunks = i + 1
    iw = iw_ref[:, D_IDX:D_IDX + H_IDX]
    for h in range(H_A):
        ps = slice((h // 2) * LANES, (h // 2 + 1) * LANES)
        keep = lane_lo if h % 2 == 0 else jnp.logical_not(lane_lo)
        qm[h] = jnp.where(keep, q_ref[:, ps] * (HD_A ** -0.5), 0.0).astype(BF16)
        iqm[h] = jnp.where(keep, iq_ref[:, ps], 0.0).astype(BF16)
        iwb[h] = jnp.broadcast_to(iw[:, h:h + 1], shape)

    def causal(kc):
        return (col + kc * LANES) <= row

    def score_body(kc, carry):
        ikc = ikd[pl.ds(pl.multiple_of(kc * LANES, LANES), LANES), :]
        tot = jnp.zeros(shape, F32)
        for h in range(H_IDX):
            s = lax.dot_general(iqm[h], ikc, nt, preferred_element_type=F32)
            tot = tot + iwb[h] * jnp.maximum(s, 0.0)
        tot = jnp.where(causal(kc), tot * IDX_SCALE, NEG)
        sc[kc] = _score_key(tot)
        return carry

    lax.fori_loop(0, n_chunks, score_body, 0)
    kf = float(topk)
    thr = _kth_largest(sc, n_chunks, kf, shape)
    _select_mask(sc, mb, n_chunks, kf, thr, shape, causal)

    m_s[...] = jnp.full(m_s.shape, NEG, F32)

    def logit_body(kc, carry):
        off = pl.multiple_of(kc * LANES, LANES)
        mbc = mb[kc]
        d = i - kc
        for h in range(H_A):
            ps = slice((h // 2) * LANES, (h // 2 + 1) * LANES)
            lg = lax.dot_general(qm[h], kb[pl.ds(off, LANES), ps], nt, preferred_element_type=F32)
            bias = jnp.where(d == 0, bias_ref[0, h], jnp.where(d == 1, bias_ref[1, h], rb_ref[N_BUCKETS - 1, h]))
            lg = lg + bias + mbc
            lgs[h, kc] = lg
            m_s[h] = jnp.maximum(m_s[h], lg)
        return carry

    lax.fori_loop(0, n_chunks, logit_body, 0)
    for h in range(H_A):
        m_s[h] = jnp.broadcast_to(jnp.max(m_s[h], axis=1, keepdims=True), shape)
    l_s[...] = jnp.zeros(l_s.shape, F32)
    acc[...] = jnp.zeros(acc.shape, F32)

    def attn_body(kc, carry):
        off = pl.multiple_of(kc * LANES, LANES)
        for h in range(H_A):
            ps = slice((h // 2) * LANES, (h // 2 + 1) * LANES)
            p = jnp.exp(lgs[h, kc] - m_s[h])
            l_s[h] += p
            acc[h] += jnp.dot(p.astype(BF16), vb[pl.ds(off, LANES), ps], preferred_element_type=F32)
        return carry

    lax.fori_loop(0, n_chunks, attn_body, 0)
    for j in range(H_A // 2):
        lo = acc[2 * j] / jnp.sum(l_s[2 * j], axis=1, keepdims=True)
        hi = acc[2 * j + 1] / jnp.sum(l_s[2 * j + 1], axis=1, keepdims=True)
        o_ref[:, j * LANES:(j + 1) * LANES] = jnp.where(lane_lo, lo, hi).astype(o_ref.dtype)


def dsa_prompt(proj, bias_p, rel_bias, tq=LANES):
    b, s, _ = proj.shape
    topk = min(TOPK_MAX, s // 4)
    nc = s // LANES
    kern = functools.partial(_dsa_prompt_kernel, topk)
    return pl.pallas_call(
        kern,
        grid=(b, s // tq),
        in_specs=[pl.BlockSpec((None, tq, A_W), lambda bi, i: (bi, i, CB_Q)),
                  pl.BlockSpec((None, tq, A_W), lambda bi, i: (bi, i, CB_IQ)),
                  pl.BlockSpec((None, tq, LANES), lambda bi, i: (bi, i, CB_IK)),
                  pl.BlockSpec((None, s, A_W), lambda bi, i: (bi, 0, CB_K)),
                  pl.BlockSpec((None, s, A_W), lambda bi, i: (bi, 0, CB_V)),
                  pl.BlockSpec((None, s, LANES), lambda bi, i: (bi, 0, CB_IK)),
                  pl.BlockSpec(bias_p.shape, lambda bi, i: (0, 0, 0, 0)),
                  pl.BlockSpec(memory_space=pltpu.SMEM)],
        out_specs=pl.BlockSpec((None, tq, A_W), lambda bi, i: (bi, i, 0)),
        out_shape=jax.ShapeDtypeStruct((b, s, A_W), BF16),
        scratch_shapes=[pltpu.VMEM((s, A_W), BF16), pltpu.VMEM((s, A_W), BF16), pltpu.VMEM((s, LANES), BF16),
                        pltpu.VMEM((H_A, tq, LANES), BF16), pltpu.VMEM((H_IDX, tq, LANES), BF16),
                        pltpu.VMEM((H_IDX, tq, LANES), F32),
                        pltpu.VMEM((nc, tq, LANES), I32), pltpu.VMEM((nc, tq, LANES), F32),
                        pltpu.VMEM((H_A, nc, tq, LANES), F32),
                        pltpu.VMEM((H_A, tq, LANES), F32), pltpu.VMEM((H_A, tq, LANES), F32),
                        pltpu.VMEM((H_A, tq, LANES), F32)],
        compiler_params=_cparams(("parallel", "arbitrary")),
        name="dsa_prompt",
    )(proj, proj, proj, proj, proj, proj, bias_p, rel_bias)


def _dsa_sample_select_kernel(pps, n_pages, topk, pt_ref, iq_ref, ikiw_ref, *rest):
    page_refs = rest[:pps]
    mb_ref = rest[pps]
    iq2, wb, sc = rest[pps + 1:]
    j = pl.program_id(1)
    t = iq_ref.shape[0]
    shape = (t, LANES)
    nc = n_pages + 1

    @pl.when(j == 0)
    def _():
        iq = iq_ref[...]
        ikiw = ikiw_ref[...]
        for h in range(H_IDX):
            iq2[h * t:(h + 1) * t, :] = iq[:, h * D_IDX:(h + 1) * D_IDX].astype(BF16)
            wb[h * t:(h + 1) * t, :] = jnp.broadcast_to(ikiw[:, D_IDX + h:D_IDX + h + 1], shape)

    def scores(ik_bf):
        n = ik_bf.shape[0]
        s = lax.dot_general(iq2[...], ik_bf, (((1,), (1,)), ((), ())), preferred_element_type=F32)
        s = jnp.maximum(s, 0.0) * jnp.tile(wb[...], (1, n // LANES))
        return jnp.sum(s.reshape(H_IDX, t, n), axis=0) * IDX_SCALE

    keys = _score_key(scores(jnp.concatenate([r[...] for r in page_refs], axis=0).astype(BF16)))
    for u in range(pps):
        sc[j * pps + u] = keys[:, u * LANES:(u + 1) * LANES]

    @pl.when(j == pl.num_programs(1) - 1)
    def _():
        row = lax.broadcasted_iota(I32, shape, 0)
        col = lax.broadcasted_iota(I32, shape, 1)
        ik_new = jnp.concatenate([ikiw_ref[:, 0:D_IDX], jnp.zeros((LANES - t, D_IDX), F32)], axis=0)
        s_new = jnp.where(col <= row, scores(ik_new.astype(BF16)), NEG)
        sc[n_pages] = jnp.where(col < t, _score_key(s_new), INT_MIN)
        kf = float(topk)
        thr = _kth_largest(sc, nc, kf, shape)

        def causal(kc):
            return (kc < n_pages) | (col <= row)

        _select_mask(sc, mb_ref, nc, kf, thr, shape, causal)


def _dsa_sample_attn_kernel(pps, n_pages, pt_ref, q_ref, kn_ref, vn_ref, mb_ref, mbn_ref, bias_ref, *rest):
    k_refs = rest[:pps]
    v_refs = rest[pps:2 * pps]
    o_ref = rest[2 * pps]
    qbd, acc, m_s, l_s = rest[2 * pps + 1:]
    j = pl.program_id(1)
    t = q_ref.shape[0]
    rows = H_A * t

    def blockdiag(x):
        r = lax.broadcasted_iota(I32, (rows, A_W), 0) // t
        c = lax.broadcasted_iota(I32, (rows, A_W), 1) // HD_A
        return jnp.where(r == c, jnp.tile(x, (H_A, 1)), 0.0)

    @pl.when(j == 0)
    def _():
        qbd[...] = blockdiag(q_ref[...] * (HD_A ** -0.5)).astype(BF16)
        m_s[...] = jnp.full(m_s.shape, NEG, F32)
        l_s[...] = jnp.zeros(l_s.shape, F32)
        acc[...] = jnp.zeros(acc.shape, F32)

    def step(kp, vp, bias, mbc):
        lg = lax.dot_general(qbd[...], kp, (((1,), (1,)), ((), ())), preferred_element_type=F32)
        lg = lg + bias + jnp.tile(mbc, (H_A, 1))
        m_old = m_s[...]
        m_new = jnp.maximum(m_old, jnp.max(lg, axis=1, keepdims=True))
        p = jnp.exp(lg - m_new)
        alpha = jnp.exp(m_old - m_new)
        l_s[...] = alpha * l_s[...] + jnp.sum(p, axis=1, keepdims=True)
        acc[...] = alpha * acc[...] + jnp.dot(p.astype(BF16), vp, preferred_element_type=F32)
        m_s[...] = m_new

    for u in range(pps):
        bias = jnp.where(j * pps + u == n_pages - 1, bias_ref[1], bias_ref[0])
        step(k_refs[u][...].astype(BF16), v_refs[u][...].astype(BF16), bias, mb_ref[u])

    @pl.when(j == pl.num_programs(1) - 1)
    def _():
        pad = jnp.zeros((LANES - t, A_W), F32)
        kn = jnp.concatenate([kn_ref[...], pad], axis=0).astype(BF16)
        vn = jnp.concatenate([vn_ref[...], pad], axis=0).astype(BF16)
        step(kn, vn, bias_ref[2], mbn_ref[0])
        res = blockdiag_sum(acc[...] / l_s[...], t)
        o_ref[...] = res.astype(o_ref.dtype)


def blockdiag_sum(x, t):
    c = lax.broadcasted_iota(I32, (t, A_W), 1) // HD_A
    out = jnp.zeros((t, A_W), F32)
    for h in range(H_A):
        out = out + jnp.where(c == h, x[h * t:(h + 1) * t, :], 0.0)
    return out


def dsa_sample(proj, layer, cache_k, cache_v, cache_idx_k, page_table, bias_s, pps_sel=16, pps_att=8):
    db, t, _ = proj.shape
    n_pages = page_table.shape[1]
    nc = n_pages + 1
    topk = min(TOPK_MAX, (n_pages * PAGE + t) // 4)
    n_pool = cache_k.shape[1]
    ck = cache_k.reshape(cache_k.shape[0], n_pool, PAGE, A_W)
    cv = cache_v.reshape(cache_v.shape[0], n_pool, PAGE, A_W)
    pt = page_table.reshape(-1)

    def page_spec(width, pps, u):
        return pl.BlockSpec((None, None, PAGE, width),
                            lambda b, j, ptr: (layer, ptr[b * n_pages + j * pps + u], 0, 0))

    sel = pl.pallas_call(
        functools.partial(_dsa_sample_select_kernel, pps_sel, n_pages, topk),
        grid_spec=pltpu.PrefetchScalarGridSpec(
            num_scalar_prefetch=1,
            grid=(db, n_pages // pps_sel),
            in_specs=[pl.BlockSpec((None, t, A_W), lambda b, j, ptr: (b, 0, CB_IQ)),
                      pl.BlockSpec((None, t, LANES), lambda b, j, ptr: (b, 0, CB_IK))]
                     + [page_spec(D_IDX, pps_sel, u) for u in range(pps_sel)],
            out_specs=pl.BlockSpec((None, nc, t, LANES), lambda b, j, ptr: (b, 0, 0, 0)),
            scratch_shapes=[pltpu.VMEM((H_IDX * t, D_IDX), BF16), pltpu.VMEM((H_IDX * t, LANES), F32),
                            pltpu.VMEM((nc, t, LANES), I32)]),
        out_shape=jax.ShapeDtypeStruct((db, nc, t, LANES), F32),
        compiler_params=_cparams(("parallel", "arbitrary")),
        name="dsa_sample_select",
    )(pt, proj, proj, *([cache_idx_k] * pps_sel))

    return pl.pallas_call(
        functools.partial(_dsa_sample_attn_kernel, pps_att, n_pages),
        grid_spec=pltpu.PrefetchScalarGridSpec(
            num_scalar_prefetch=1,
            grid=(db, n_pages // pps_att),
            in_specs=[pl.BlockSpec((None, t, A_W), lambda b, j, ptr: (b, 0, CB_Q)),
                      pl.BlockSpec((None, t, A_W), lambda b, j, ptr: (b, 0, CB_K)),
                      pl.BlockSpec((None, t, A_W), lambda b, j, ptr: (b, 0, CB_V)),
                      pl.BlockSpec((None, pps_att, t, LANES), lambda b, j, ptr: (b, j, 0, 0)),
                      pl.BlockSpec((None, 1, t, LANES), lambda b, j, ptr: (b, n_pages, 0, 0)),
                      pl.BlockSpec(bias_s.shape, lambda b, j, ptr: (0, 0, 0))]
                     + [page_spec(A_W, pps_att, u) for u in range(pps_att)] * 2,
            out_specs=pl.BlockSpec((None, t, A_W), lambda b, j, ptr: (b, 0, 0)),
            scratch_shapes=[pltpu.VMEM((H_A * t, A_W), BF16), pltpu.VMEM((H_A * t, A_W), F32),
                            pltpu.VMEM((H_A * t, 1), F32), pltpu.VMEM((H_A * t, 1), F32)]),
        out_shape=jax.ShapeDtypeStruct((db, t, A_W), BF16),
        compiler_params=_cparams(("parallel", "arbitrary")),
        name="dsa_sample_attn",
    )(pt, proj, proj, proj, sel, sel, bias_s, *([ck] * pps_att), *([cv] * pps_att))


def _mm(a, b):
    return jnp.dot(a.astype(BF16), b.astype(BF16), preferred_element_type=F32)


def _mm_nt(a, b):
    return lax.dot_general(a.astype(BF16), b.astype(BF16), (((1,), (1,)), ((), ())), preferred_element_type=F32)


def _split3(x):
    hi = x.astype(BF16)
    r1 = x - hi.astype(F32)
    mid = r1.astype(BF16)
    lo = (r1 - mid.astype(F32)).astype(BF16)
    return hi, mid, lo


def _rwkv_kernel(has_state, t_valid, r_ref, k_ref, v_ref, wa_ref, g_ref, sr_ref, sk_ref, sv_ref, swa_ref, sg_ref,
                 mr_ref, mk_ref, mv_ref, mwa_ref, mg_ref, w2_ref, a2_ref, g2_ref, vec_ref, *rest):
    if has_state:
        s0_ref, y_ref, so_ref, pr, pk, pv, pwa, pg, st = rest
    else:
        y_ref, so_ref, pr, pk, pv, pwa, pg, st = rest
    c = pl.program_id(1)
    C = RW_C
    tb = r_ref.shape[0]
    n_pairs = H_B // 2
    zero_blk = jnp.zeros((HD_B, HD_B), F32)

    @pl.when(c == 0)
    def _():
        if has_state:
            for p in range(n_pairs):
                st[p] = jnp.concatenate([jnp.concatenate([s0_ref[2 * p], zero_blk], axis=1),
                                         jnp.concatenate([zero_blk, s0_ref[2 * p + 1]], axis=1)], axis=0)
        else:
            st[...] = jnp.zeros(st.shape, F32)

    def shifted(p_ref, prev_scr, shift_ref, mu_ref):
        p = p_ref[...]
        if tb < C:
            p = jnp.concatenate([p, jnp.zeros((C - tb, p.shape[1]), F32)], axis=0)
        prev = jnp.where(c == 0, shift_ref[...], prev_scr[...])
        rowi = lax.broadcasted_iota(I32, p.shape, 0)
        ps = jnp.where(rowi == 0, prev, pltpu.roll(p, 1, 0))
        prev_scr[...] = p[C - 1:C, :]
        return p + (ps - p) * mu_ref[...]

    xr = shifted(r_ref, pr, sr_ref, mr_ref)
    xk = shifted(k_ref, pk, sk_ref, mk_ref)
    xv = shifted(v_ref, pv, sv_ref, mv_ref)
    xwa = shifted(wa_ref, pwa, swa_ref, mwa_ref)
    xg = shifted(g_ref, pg, sg_ref, mg_ref)

    vec = vec_ref[...]
    w0, a0, kkp, kap, rkp, lnw, lnb = (vec[n:n + 1, :] for n in range(7))
    zw = w0 + _mm(jnp.tanh(xwa[:, 0:W_LORA]), w2_ref[...])
    w_log = -(jnp.maximum(-zw, 0.0) + jnp.log(1.0 + jnp.exp(-jnp.abs(zw)))) - 0.5
    ld = -jnp.exp(w_log)
    za = a0 + _mm(xwa[:, W_LORA:W_LORA + A_LORA], a2_ref[...])
    a = 1.0 / (1.0 + jnp.exp(-za))
    g = _mm(1.0 / (1.0 + jnp.exp(-xg)), g2_ref[...])

    r2 = lax.broadcasted_iota(I32, (2 * C, 2 * C), 0)
    c2 = lax.broadcasted_iota(I32, (2 * C, 2 * C), 1)
    same_head = (r2 // C) == (c2 // C)
    head_ones = same_head.astype(BF16)
    strict = same_head & ((c2 % C) < (r2 % C))
    incl = same_head & ((c2 % C) <= (r2 % C))

    def pairs(x):
        return jnp.concatenate([x[:, p * LANES:(p + 1) * LANES] for p in range(n_pairs)], axis=0)

    def unpairs(x):
        return jnp.concatenate([x[p * C:(p + 1) * C] for p in range(n_pairs)], axis=1)

    def head_sum(x):
        xs = pairs(x)
        hi = xs.astype(BF16)
        mid = (xs - hi.astype(F32)).astype(BF16)
        return unpairs(jnp.dot(hi, head_ones, preferred_element_type=F32)
                       + jnp.dot(mid, head_ones, preferred_element_type=F32))

    kk = xk * kkp
    kk = kk / jnp.maximum(jnp.sqrt(head_sum(kk * kk)), 1e-12)
    k2 = xk * (1.0 + (a - 1.0) * kap)
    bonus_w = head_sum(xr * k2 * rkp)
    if t_valid < C:
        valid = lax.broadcasted_iota(I32, (C, B_W), 0) < t_valid
        ld = jnp.where(valid, ld, 0.0)
        xv = jnp.where(valid, xv, 0.0)
        kk = jnp.where(valid, kk, 0.0)
        k2m = jnp.where(valid, k2, 0.0)
    else:
        k2m = k2
    rc = lax.broadcasted_iota(I32, (C, C), 0)
    cc = lax.broadcasted_iota(I32, (C, C), 1)
    tril = (cc <= rc).astype(BF16)
    hi, mid, lo = _split3(ld)
    lgc = (jnp.dot(tril, hi, preferred_element_type=F32) + jnp.dot(tril, mid, preferred_element_type=F32)
           + jnp.dot(tril, lo, preferred_element_type=F32))
    lg_end = lgc[C - 1:C, :]
    ginv = jnp.exp(-lgc)
    e_end = jnp.exp(lg_end - lgc)
    g_end = jnp.exp(lg_end)
    kb_ = kk * a
    al = kk * jnp.exp(lgc - ld)
    be = kb_ * ginv
    kt = k2m * ginv
    rt = xr * jnp.exp(lgc)
    bp = kb_ * e_end
    kp = k2m * e_end

    lane_lo = lax.broadcasted_iota(I32, (C, LANES), 1) < HD_B

    def stack(x, p):
        xp = x[:, p * LANES:(p + 1) * LANES]
        return jnp.concatenate([jnp.where(lane_lo, xp, 0.0), jnp.where(lane_lo, 0.0, xp)], axis=0)

    P = range(n_pairs)
    bf = lambda x: x.astype(BF16)
    lhs = [bf(jnp.concatenate([stack(al, p), stack(rt, p)], axis=0)) for p in P]
    vs = [bf(stack(xv, p)) for p in P]
    gb = [_mm_nt(lhs[p], stack(be, p)) for p in P]
    gk = [_mm_nt(lhs[p], stack(kt, p)) for p in P]
    p0 = [_mm_nt(lhs[p], st[p]) for p in P]
    qn = [bf(jnp.where(strict, -gb[p][0:2 * C], 0.0)) for p in P]
    u = [p0[p][0:2 * C] + _mm(jnp.where(strict, gk[p][0:2 * C], 0.0), vs[p]) for p in P]
    u = [u[p] + _mm(qn[p], u[p]) for p in P]
    for _ in range(int(math.log2(C)) - 1):
        qn = [bf(_mm(qn[p], qn[p])) for p in P]
        u = [u[p] + _mm(qn[p], u[p]) for p in P]
    m2 = [jnp.concatenate([bf(jnp.where(incl, -gb[p][2 * C:], 0.0)), bf(jnp.where(incl, gk[p][2 * C:], 0.0))], axis=1)
          for p in P]
    uv = [jnp.concatenate([bf(u[p]), vs[p]], axis=0) for p in P]
    yp = [p0[p][2 * C:] + _mm(m2[p], uv[p]) for p in P]
    upd = [lax.dot_general(jnp.concatenate([bf(-u[p]), vs[p]], axis=0),
                           bf(jnp.concatenate([stack(bp, p), stack(kp, p)], axis=0)),
                           (((0,), (0,)), ((), ())), preferred_element_type=F32) for p in P]
    for p in P:
        st[p] = st[p] * g_end[:, p * LANES:(p + 1) * LANES] + upd[p]
    y = jnp.concatenate([yp[p][0:C] + yp[p][C:2 * C] for p in P], axis=1)
    mu = head_sum(y) * (1.0 / HD_B)
    dy = y - mu
    var = head_sum(dy * dy) * (1.0 / HD_B)
    yn = dy * lax.rsqrt(var + GN_EPS) * lnw + lnb
    out = (yn + bonus_w * xv) * g
    y_ref[...] = out[0:tb].astype(y_ref.dtype)

    @pl.when(c == pl.num_programs(1) - 1)
    def _():
        for p in range(n_pairs):
            s = st[p]
            so_ref[2 * p] = s[0:HD_B, 0:HD_B]
            so_ref[2 * p + 1] = s[HD_B:, HD_B:]


def rwkv_mix(proj, shift_full, mu_full, w2, a2, g2p, vecs, s0):
    b, t, _ = proj.shape
    tb = min(t, RW_C)
    nch = t // tb
    has_state = s0 is not None
    cbs = [(B_W, OFF_R // B_W), (B_W, OFF_R // B_W + 1), (B_W, OFF_R // B_W + 2), (LANES, CB_WA), (256, CB_G)]

    in_specs = ([pl.BlockSpec((None, tb, w), functools.partial(lambda cb, bi, c: (bi, c, cb), cb)) for w, cb in cbs]
                + [pl.BlockSpec((None, 1, w), functools.partial(lambda cb, bi, c: (bi, 0, cb), cb)) for w, cb in cbs]
                + [pl.BlockSpec((1, w), functools.partial(lambda cb, bi, c: (0, cb), cb)) for w, cb in cbs]
                + [pl.BlockSpec(w2.shape, lambda bi, c: (0, 0)),
                   pl.BlockSpec(a2.shape, lambda bi, c: (0, 0)),
                   pl.BlockSpec(g2p.shape, lambda bi, c: (0, 0)),
                   pl.BlockSpec(vecs.shape, lambda bi, c: (0, 0))])
    args = [proj] * 5 + [shift_full] * 5 + [mu_full] * 5 + [w2, a2, g2p, vecs]
    if has_state:
        in_specs.append(pl.BlockSpec((None, H_B, HD_B, HD_B), lambda bi, c: (bi, 0, 0, 0)))
        args.append(s0)
    return pl.pallas_call(
        functools.partial(_rwkv_kernel, has_state, tb),
        grid=(b, nch),
        in_specs=in_specs,
        out_specs=[pl.BlockSpec((None, tb, B_W), lambda bi, c: (bi, c, 0)),
                   pl.BlockSpec((None, H_B, HD_B, HD_B), lambda bi, c: (bi, 0, 0, 0))],
        out_shape=[jax.ShapeDtypeStruct((b, t, B_W), BF16),
                   jax.ShapeDtypeStruct((b, H_B, HD_B, HD_B), F32)],
        scratch_shapes=[pltpu.VMEM((1, B_W), F32)] * 3 + [pltpu.VMEM((1, LANES), F32), pltpu.VMEM((1, 256), F32),
                                                          pltpu.VMEM((H_B // 2, 2 * HD_B, 2 * HD_B), F32)],
        compiler_params=_cparams(("parallel", "arbitrary")),
        name="rwkv_mix",
    )(*args)


def _gelu(x):
    return 0.5 * x * (1.0 + jnp.tanh(math.sqrt(2.0 / math.pi) * (x + 0.044715 * (x * x * x))))


def _gmlp_kernel(pu_ref, pv_ref, lnw_ref, lnb_ref, ws_ref, bst_ref, y_ref, gv_ref):
    tc = pu_ref.shape[0]
    u = _gelu(pu_ref[...])
    vf = _gelu(pv_ref[...])
    mu = jnp.mean(vf, axis=1, keepdims=True)
    var = jnp.mean(jnp.square(vf - mu), axis=1, keepdims=True)
    v = (vf - mu) * lax.rsqrt(var + 1e-5) * lnw_ref[...] + lnb_ref[...]
    gv_ref[...] = v
    ri = lax.broadcasted_iota(I32, (CHUNK, CHUNK), 0)
    ci = lax.broadcasted_iota(I32, (CHUNK, CHUNK), 1)
    if tc < CHUNK:
        v = jnp.concatenate([v, jnp.zeros((CHUNK - tc, C_W), F32)], axis=0)
    vb = v.astype(BF16)
    for g in range(G_C):
        sl = slice(g * CD, (g + 1) * CD)
        wm = jnp.where(ci <= ri, ws_ref[g], 0.0).astype(BF16)
        sp = jnp.dot(wm, vb[:, sl], preferred_element_type=F32)[0:tc] + bst_ref[0:tc, g:g + 1]
        y_ref[:, sl] = (u[:, sl] * sp).astype(y_ref.dtype)


def gmlp_mix(proj, lnw, lnb, ws, bst):
    b, t, _ = proj.shape
    tc = min(t, CHUNK)
    return pl.pallas_call(
        _gmlp_kernel,
        grid=(b, t // tc),
        in_specs=[pl.BlockSpec((None, tc, C_W), lambda bi, c: (bi, c, CB_U)),
                  pl.BlockSpec((None, tc, C_W), lambda bi, c: (bi, c, CB_PV)),
                  pl.BlockSpec((1, C_W), lambda bi, c: (0, 0)),
                  pl.BlockSpec((1, C_W), lambda bi, c: (0, 0)),
                  pl.BlockSpec(ws.shape, lambda bi, c: (0, 0, 0)),
                  pl.BlockSpec(bst.shape, lambda bi, c: (0, 0))],
        out_specs=[pl.BlockSpec((None, tc, C_W), lambda bi, c: (bi, c, 0)),
                   pl.BlockSpec((None, tc, C_W), lambda bi, c: (bi, c, 0))],
        out_shape=[jax.ShapeDtypeStruct((b, t, C_W), BF16), jax.ShapeDtypeStruct((b, t, C_W), F32)],
        compiler_params=_cparams(("parallel", "parallel")),
        name="gmlp_mix",
    )(proj, proj, lnw.reshape(1, C_W), lnb.reshape(1, C_W), ws, bst)


def _layer(x, lw, attend, shift_full, s0, tm):
    b, t, d = x.shape
    x2 = x.reshape(b * t, d)
    proj = norm_matmul(x2, lw['norm_mix'], lw['w_in'], tm, 512).reshape(b, t, PROJ_P)
    ya = attend(proj)
    yb, wkv = rwkv_mix(proj, shift_full, lw['mu_full'], lw['w2'], lw['a2'], lw['g2p'], lw['vecs'], s0)
    yc, gv = gmlp_mix(proj, lw['gmlp_ln_w'], lw['gmlp_ln_b'], lw['gmlp_ws'], lw['gmlp_bst'])
    h = out_proj(x2, ya.reshape(b * t, A_W), yb.reshape(b * t, B_W), yc.reshape(b * t, C_W), lw['w_out'], tm)
    y = ffn(h, lw['norm_ffn'], lw['ffn_gate'], lw['ffn_up'], lw['ffn_down'], tm, 512)
    k = proj[:, :, A_W:2 * A_W].reshape(b, t, H_A, HD_A)
    v = proj[:, :, 2 * A_W:3 * A_W].reshape(b, t, H_A, HD_A)
    ik = proj[:, :, OFF_IK:OFF_IK + D_IDX]
    shift_new = _rwkv_cols(proj[:, t - 1])
    return y.reshape(b, t, d), k, v, ik, shift_new, wkv, gv


def kernel(x_prompt, x_sample, cache_k, cache_v, cache_idx_k, state_rwkv_shift, state_rwkv_wkv, page_table, rel_bias, norm_mix, w_in, rwkv_mu, rwkv_w0, rwkv_w2, rwkv_a0, rwkv_a2, rwkv_g2, rwkv_kk, rwkv_ka, rwkv_rk, rwkv_ln_w, rwkv_ln_b, gmlp_ln_w, gmlp_ln_b, gmlp_ws, gmlp_b, w_out, norm_ffn, ffn_gate, ffn_up, ffn_down, norm_final):
    depth = w_in.shape[0]
    nbp, seq, d = x_prompt.shape
    db, dec_seq, _ = x_sample.shape
    assert d == D_MODEL and seq % 512 == 0 and dec_seq == SUBLANES and w_in.shape[2] == ORIG_GM + 2 * C_W

    w_in_p = _pad_cols(w_in).astype(BF16)
    w_out_b = w_out.astype(BF16)
    wg_b, wu_b, wd_b = ffn_gate.astype(BF16), ffn_up.astype(BF16), ffn_down.astype(BF16)
    mu_full = _pad_rwkv_cols(rwkv_mu)
    g2p = jnp.concatenate([rwkv_g2, jnp.zeros((depth, 256 - G_LORA, B_W), F32)], axis=1)
    vecs = jnp.stack([rwkv_w0, rwkv_a0, rwkv_kk, rwkv_ka, rwkv_rk, rwkv_ln_w, rwkv_ln_b,
                      jnp.zeros_like(rwkv_w0)], axis=1)
    bst = jnp.swapaxes(gmlp_b, 1, 2)
    shift_s = _pad_rwkv_cols(state_rwkv_shift)[:, :, None, :]
    shift_p = jnp.zeros((nbp, 1, PROJ_P), F32)
    bias_p, bias_s = bias_tiles(rel_bias, dec_seq)

    xp, xs = x_prompt, x_sample
    outs = [[] for _ in range(11)]
    for l in range(depth):
        lw = {'norm_mix': norm_mix[l], 'w_in': w_in_p[l], 'mu_full': mu_full[l:l + 1], 'w2': rwkv_w2[l],
              'a2': rwkv_a2[l], 'g2p': g2p[l], 'vecs': vecs[l], 'gmlp_ln_w': gmlp_ln_w[l],
              'gmlp_ln_b': gmlp_ln_b[l], 'gmlp_ws': gmlp_ws[l], 'gmlp_bst': bst[l], 'w_out': w_out_b[l],
              'norm_ffn': norm_ffn[l], 'ffn_gate': wg_b[l], 'ffn_up': wu_b[l], 'ffn_down': wd_b[l]}
        attend_p = functools.partial(dsa_prompt, bias_p=bias_p, rel_bias=rel_bias)
        xp, kp, vp, ikp, shp, wkvp, _ = _layer(xp, lw, attend_p, shift_p, None, 512)
        attend_s = functools.partial(dsa_sample, layer=l, cache_k=cache_k, cache_v=cache_v,
                                     cache_idx_k=cache_idx_k, page_table=page_table, bias_s=bias_s)
        xs, k_s, v_s, ik_s, sh_s, wkv_s, gv_s = _layer(xs, lw, attend_s, shift_s[l], state_rwkv_wkv[l],
                                                       db * dec_seq)
        for lst, val in zip(outs, (kp, vp, ikp, shp, wkvp, k_s, v_s, ik_s, sh_s, wkv_s, gv_s)):
            lst.append(val)
    y_prompt = rms_final(xp.reshape(nbp * seq, d), norm_final, 512).reshape(nbp, seq, d)
    y_sample = rms_final(xs.reshape(db * dec_seq, d), norm_final, db * dec_seq).reshape(db, dec_seq, d)
    return (y_prompt, y_sample) + tuple(jnp.stack(o) for o in outs)
```

```python
import functools
import math

import numpy as np
import jax
import jax.numpy as jnp
from jax import lax
from jax.experimental import pallas as pl
from jax.experimental.pallas import tpu as pltpu

F32 = jnp.float32
BF16 = jnp.bfloat16
I32 = jnp.int32

LANES = 128
SUBLANES = 8
VMEM_LIMIT = 56 * 1024 * 1024

D_MODEL = 2048
HD_A = 64
H_A = 8
A_W = H_A * HD_A
H_IDX = 8
D_IDX = 64
IDX_SCALE = (D_IDX ** -0.5) * (H_IDX ** -0.5)
TOPK_MAX = 256
N_BUCKETS = 32
MAX_DISTANCE = 128
HD_B = 64
B_W = 1024
H_B = B_W // HD_B
W_LORA = 64
A_LORA = 64
G_LORA = 160
RWKV_PROJ_W = 3 * B_W + W_LORA + A_LORA + G_LORA
GN_EPS = 64e-5
C_W = 512
CHUNK = 128
CD = 64
G_C = C_W // CD
D_FF = 5632
NORM_EPS = 1e-6
NEG = -1e30
PAGE = 128
INT_MIN = -(2 ** 31)

ORIG_IK = 3 * A_W + H_IDX * D_IDX
ORIG_R = ORIG_IK + D_IDX + H_IDX
ORIG_GM = ORIG_R + RWKV_PROJ_W
OFF_R = 2048
OFF_IK = OFF_R + 3 * B_W
OFF_WA = OFF_IK + LANES
OFF_G = OFF_WA + W_LORA + A_LORA
OFF_GM = OFF_G + 256
PROJ_P = OFF_GM + 2 * C_W
CB_Q, CB_K, CB_V, CB_IQ = 0, 1, 2, 3
CB_IK = OFF_IK // LANES
CB_WA = OFF_WA // LANES
CB_G = OFF_G // 256
CB_U = OFF_GM // C_W
CB_PV = CB_U + 1
RW_C = 64
assert OFF_R % B_W == 0 and OFF_G % 256 == 0 and OFF_GM % C_W == 0


def _cparams(sem):
    return pltpu.CompilerParams(dimension_semantics=sem, vmem_limit_bytes=VMEM_LIMIT)


def _pad_cols(a):
    z = lambda n: jnp.zeros(a.shape[:-1] + (n,), a.dtype)
    lora = ORIG_R + 3 * B_W
    return jnp.concatenate([a[..., :ORIG_IK], a[..., ORIG_R:lora], a[..., ORIG_IK:ORIG_R], z(OFF_WA - OFF_IK - D_IDX - H_IDX),
                            a[..., lora:ORIG_GM], z(OFF_GM - OFF_G - G_LORA), a[..., ORIG_GM:]], axis=-1)


def _pad_rwkv_cols(a):
    z = lambda n: jnp.zeros(a.shape[:-1] + (n,), a.dtype)
    return jnp.concatenate([z(OFF_R), a[..., :3 * B_W], z(LANES), a[..., 3 * B_W:], z(PROJ_P - OFF_G - G_LORA)], axis=-1)


def _rwkv_cols(p):
    return jnp.concatenate([p[..., OFF_R:OFF_R + 3 * B_W], p[..., OFF_WA:OFF_WA + RWKV_PROJ_W - 3 * B_W]], axis=-1)


def _norm_matmul_kernel(x_ref, g_ref, w_ref, o_ref, xn_ref):
    @pl.when(pl.program_id(1) == 0)
    def _():
        x = x_ref[...]
        y = x * lax.rsqrt(jnp.mean(x * x, axis=-1, keepdims=True) + NORM_EPS)
        xn_ref[...] = (y * g_ref[...]).astype(BF16)

    o_ref[...] = jnp.dot(xn_ref[...], w_ref[...], preferred_element_type=F32)


def norm_matmul(x, g, w, tm, tn):
    m, d = x.shape
    n = w.shape[1]
    return pl.pallas_call(
        _norm_matmul_kernel,
        grid=(m // tm, n // tn),
        in_specs=[pl.BlockSpec((tm, d), lambda i, j: (i, 0)),
                  pl.BlockSpec((1, d), lambda i, j: (0, 0)),
                  pl.BlockSpec((d, tn), lambda i, j: (0, j))],
        out_specs=pl.BlockSpec((tm, tn), lambda i, j: (i, j)),
        out_shape=jax.ShapeDtypeStruct((m, n), F32),
        scratch_shapes=[pltpu.VMEM((tm, d), BF16)],
        compiler_params=_cparams(("parallel", "arbitrary")),
        name="norm_matmul",
    )(x, g.reshape(1, d), w)


def _out_proj_kernel(x_ref, ya_ref, yb_ref, yc_ref, w_ref, o_ref):
    acc = jnp.dot(ya_ref[...], w_ref[0:A_W, :], preferred_element_type=F32)
    acc += jnp.dot(yb_ref[...], w_ref[A_W:A_W + B_W, :], preferred_element_type=F32)
    acc += jnp.dot(yc_ref[...], w_ref[A_W + B_W:, :], preferred_element_type=F32)
    o_ref[...] = x_ref[...] + acc


def out_proj(x, ya, yb, yc, w, tm):
    m, d = x.shape
    return pl.pallas_call(
        _out_proj_kernel,
        grid=(m // tm,),
        in_specs=[pl.BlockSpec((tm, d), lambda i: (i, 0)),
                  pl.BlockSpec((tm, A_W), lambda i: (i, 0)),
                  pl.BlockSpec((tm, B_W), lambda i: (i, 0)),
                  pl.BlockSpec((tm, C_W), lambda i: (i, 0)),
                  pl.BlockSpec(w.shape, lambda i: (0, 0))],
        out_specs=pl.BlockSpec((tm, d), lambda i: (i, 0)),
        out_shape=jax.ShapeDtypeStruct((m, d), F32),
        compiler_params=_cparams(("parallel",)),
        name="out_proj",
    )(x, ya, yb, yc, w)


def _ffn_kernel(h_ref, g_ref, wg_ref, wu_ref, wd_ref, o_ref, hn_ref, acc_ref):
    f = pl.program_id(1)

    @pl.when(f == 0)
    def _():
        x = h_ref[...]
        y = x * lax.rsqrt(jnp.mean(x * x, axis=-1, keepdims=True) + NORM_EPS)
        hn_ref[...] = (y * g_ref[...]).astype(BF16)
        acc_ref[...] = jnp.zeros_like(acc_ref)

    hn = hn_ref[...]
    gate = jnp.dot(hn, wg_ref[...], preferred_element_type=F32)
    up = jnp.dot(hn, wu_ref[...], preferred_element_type=F32)
    act = (gate / (1.0 + jnp.exp(-gate))) * up
    acc_ref[...] += jnp.dot(act.astype(BF16), wd_ref[...], preferred_element_type=F32)

    @pl.when(f == pl.num_programs(1) - 1)
    def _():
        o_ref[...] = h_ref[...] + acc_ref[...]


def ffn(h, g, wg, wu, wd, tm, tf):
    m, d = h.shape
    nf = wg.shape[1]
    return pl.pallas_call(
        _ffn_kernel,
        grid=(m // tm, nf // tf),
        in_specs=[pl.BlockSpec((tm, d), lambda i, f: (i, 0)),
                  pl.BlockSpec((1, d), lambda i, f: (0, 0)),
                  pl.BlockSpec((d, tf), lambda i, f: (0, f)),
                  pl.BlockSpec((d, tf), lambda i, f: (0, f)),
                  pl.BlockSpec((tf, d), lambda i, f: (f, 0))],
        out_specs=pl.BlockSpec((tm, d), lambda i, f: (i, 0)),
        out_shape=jax.ShapeDtypeStruct((m, d), F32),
        scratch_shapes=[pltpu.VMEM((tm, d), BF16), pltpu.VMEM((tm, d), F32)],
        compiler_params=_cparams(("parallel", "arbitrary")),
        name="ffn",
    )(h, g.reshape(1, d), wg, wu, wd)


def _rms_kernel(x_ref, g_ref, o_ref):
    x = x_ref[...]
    o_ref[...] = x * lax.rsqrt(jnp.mean(x * x, axis=-1, keepdims=True) + NORM_EPS) * g_ref[...]


def rms_final(x, g, tm):
    m, d = x.shape
    return pl.pallas_call(
        _rms_kernel,
        grid=(m // tm,),
        in_specs=[pl.BlockSpec((tm, d), lambda i: (i, 0)), pl.BlockSpec((1, d), lambda i: (0, 0))],
        out_specs=pl.BlockSpec((tm, d), lambda i: (i, 0)),
        out_shape=jax.ShapeDtypeStruct((m, d), F32),
        compiler_params=_cparams(("parallel",)),
        name="rms_final",
    )(x, g.reshape(1, d))


def _bucket_np(dist):
    max_exact = N_BUCKETS // 2
    n = np.maximum(dist, 0)
    nf = np.maximum(n, 1).astype(np.float64)
    large = max_exact + (np.log(nf / max_exact) / math.log(MAX_DISTANCE / max_exact)
                         * (N_BUCKETS - max_exact)).astype(np.int32)
    return np.where(n < max_exact, n, np.minimum(large, N_BUCKETS - 1)).astype(np.int32)


def _bias_tiles_kernel(rb_ref, bp_ref, bs_ref, op_ref, os_ref):
    for d in range(2):
        bk = bp_ref[d]
        for h in range(H_A):
            t = jnp.zeros(bk.shape, F32)
            for b in range(N_BUCKETS):
                t = jnp.where(bk == b, rb_ref[b, h], t)
            op_ref[d, h] = t
    for d in range(3):
        bk = bs_ref[d]
        for h in range(H_A):
            t = jnp.zeros(bk.shape, F32)
            for b in range(N_BUCKETS):
                t = jnp.where(bk == b, rb_ref[b, h], t)
            os_ref[d, h * SUBLANES:(h + 1) * SUBLANES, :] = t


def bias_tiles(rel_bias, dec_seq):
    r = np.arange(LANES)[:, None]
    c = np.arange(LANES)[None, :]
    bp = np.stack([_bucket_np(r - c), _bucket_np(LANES + r - c)])
    t = np.arange(dec_seq)[:, None]
    bs = np.stack([_bucket_np(np.full((dec_seq, LANES), 4 * MAX_DISTANCE)),
                   _bucket_np(PAGE + t - c), _bucket_np(t - c)])
    vm = pl.BlockSpec(memory_space=pltpu.VMEM)
    return pl.pallas_call(
        _bias_tiles_kernel,
        in_specs=[pl.BlockSpec(memory_space=pltpu.SMEM), vm, vm],
        out_specs=[vm, vm],
        out_shape=[jax.ShapeDtypeStruct((2, H_A, LANES, LANES), F32),
                   jax.ShapeDtypeStruct((3, H_A * dec_seq, LANES), F32)],
        name="bias_tiles",
    )(rel_bias, jnp.asarray(bp), jnp.asarray(bs))


def _score_key(s):
    s = jnp.where(s == 0.0, 0.0, s)
    bits = lax.bitcast_convert_type(s, I32)
    return jnp.where(bits < 0, bits ^ 0x7FFFFFFF, bits)


def _kth_largest(sc_ref, n_chunks, kf, shape):
    def count_ge(cand):
        if isinstance(n_chunks, int):
            a = jnp.sum((sc_ref[...] >= cand[None]).astype(F32), axis=0)
        else:
            def body(kc, a):
                return a + (sc_ref[kc] >= cand).astype(F32)

            a = lax.fori_loop(0, n_chunks, body, jnp.zeros(shape, F32))
        return jnp.sum(a, axis=1, keepdims=True)

    zero = jnp.zeros(shape, I32)
    t0 = jnp.where(count_ge(zero) >= kf, zero, jnp.full(shape, INT_MIN, I32))

    def bit_body(it, t):
        cand = t | jnp.left_shift(jnp.int32(1), 30 - it)
        return jnp.where(count_ge(cand) >= kf, cand, t)

    return lax.fori_loop(0, 31, bit_body, t0)


def _select_mask(sc_ref, mb_ref, n_chunks, kf, thr, shape, causal_fn):
    def cnt_body(kc, carry):
        key = sc_ref[kc]
        return carry[0] + (key > thr).astype(F32), carry[1] + (key >= thr).astype(F32)

    zeros = jnp.zeros(shape, F32)
    if isinstance(n_chunks, int):
        keys = sc_ref[...]
        cgt = jnp.sum((keys > thr[None]).astype(F32), axis=0)
        cge = jnp.sum((keys >= thr[None]).astype(F32), axis=0)
    else:
        cgt, cge = lax.fori_loop(0, n_chunks, cnt_body, (zeros, zeros))
    need = kf - jnp.sum(cgt, axis=1, keepdims=True)
    cge = jnp.sum(cge, axis=1, keepdims=True)

    def exact_k():
        def body(kc, carry):
            mb_ref[kc] = jnp.where((sc_ref[kc] >= thr) & causal_fn(kc), 0.0, NEG)
            return carry

        lax.fori_loop(0, n_chunks, body, 0, unroll=8 if isinstance(n_chunks, int) else 1)

    def with_ties():
        rr = lax.broadcasted_iota(I32, (LANES, LANES), 0)
        cc = lax.broadcasted_iota(I32, (LANES, LANES), 1)
        tri = (rr <= cc).astype(BF16)

        def body(kc, offs):
            key = sc_ref[kc]
            eq = key == thr
            pre = jnp.dot(eq.astype(BF16), tri, preferred_element_type=F32) + offs
            sel = (key > thr) | (eq & (pre <= need))
            mb_ref[kc] = jnp.where(sel & causal_fn(kc), 0.0, NEG)
            return jnp.broadcast_to(pre[:, LANES - 1:LANES], shape)

        lax.fori_loop(0, n_chunks, body, zeros)

    lax.cond(jnp.max(jnp.abs(cge - kf)) > 0.5, with_ties, exact_k)


def _dsa_prompt_kernel(topk, q_ref, iq_ref, iw_ref, k_ref, v_ref, ik_ref, bias_ref, rb_ref, o_ref,
                       kb, vb, ikd, qm, iqm, iwb, sc, mb, lgs, acc, m_s, l_s):
    i = pl.program_id(1)
    tq = q_ref.shape[0]
    shape = (tq, LANES)
    nt = (((1,), (1,)), ((), ()))

    @pl.when(i == 0)
    def _():
        kb[...] = k_ref[...].astype(BF16)
        vb[...] = v_ref[...].astype(BF16)
        ik = ik_ref[:, 0:D_IDX].astype(BF16)
        ikd[...] = jnp.concatenate([ik, ik], axis=1)

    row = lax.broadcasted_iota(I32, shape, 0) + i * tq
    col = lax.broadcasted_iota(I32, shape, 1)
    lane_lo = col < HD_A
    n_chunks = i + 1
    iw = iw_ref[:, D_IDX:D_IDX + H_IDX]
    for h in range(H_A):
        ps = slice((h // 2) * LANES, (h // 2 + 1) * LANES)
        keep = lane_lo if h % 2 == 0 else jnp.logical_not(lane_lo)
        qm[h] = jnp.where(keep, q_ref[:, ps] * (HD_A ** -0.5), 0.0).astype(BF16)
        iqm[h] = jnp.where(keep, iq_ref[:, ps], 0.0).astype(BF16)
        iwb[h] = jnp.broadcast_to(iw[:, h:h + 1], shape)

    def causal(kc):
        return (col + kc * LANES) <= row

    def score_body(kc, carry):
        ikc = ikd[pl.ds(pl.multiple_of(kc * LANES, LANES), LANES), :]
        ss = [lax.dot_general(iqm[h], ikc, nt, preferred_element_type=F32) for h in range(H_IDX)]
        ws = [iwb[h] * jnp.maximum(ss[h], 0.0) for h in range(H_IDX)]
        tot = ((ws[0] + ws[1]) + (ws[2] + ws[3])) + ((ws[4] + ws[5]) + (ws[6] + ws[7]))
        tot = jnp.where(causal(kc), tot * IDX_SCALE, NEG)
        sc[kc] = _score_key(tot)
        return carry

    lax.fori_loop(0, n_chunks, score_body, 0)
    kf = float(topk)
    thr = _kth_largest(sc, n_chunks, kf, shape)
    _select_mask(sc, mb, n_chunks, kf, thr, shape, causal)

    m_s[...] = jnp.full(m_s.shape, NEG, F32)

    def logit_body(kc, carry):
        off = pl.multiple_of(kc * LANES, LANES)
        mbc = mb[kc]
        d = i - kc
        for h in range(H_A):
            ps = slice((h // 2) * LANES, (h // 2 + 1) * LANES)
            lg = lax.dot_general(qm[h], kb[pl.ds(off, LANES), ps], nt, preferred_element_type=F32)
            bias = jnp.where(d == 0, bias_ref[0, h], jnp.where(d == 1, bias_ref[1, h], rb_ref[N_BUCKETS - 1, h]))
            lg = lg + bias + mbc
            lgs[h, kc] = lg
            m_s[h] = jnp.maximum(m_s[h], lg)
        return carry

    lax.fori_loop(0, n_chunks, logit_body, 0)
    for h in range(H_A):
        m_s[h] = jnp.broadcast_to(jnp.max(m_s[h], axis=1, keepdims=True), shape)
    l_s[...] = jnp.zeros(l_s.shape, F32)
    acc[...] = jnp.zeros(acc.shape, F32)

    def attn_body(kc, carry):
        off = pl.multiple_of(kc * LANES, LANES)
        for h in range(H_A):
            ps = slice((h // 2) * LANES, (h // 2 + 1) * LANES)
            p = jnp.exp(lgs[h, kc] - m_s[h])
            l_s[h] += p
            acc[h] += jnp.dot(p.astype(BF16), vb[pl.ds(off, LANES), ps], preferred_element_type=F32)
        return carry

    lax.fori_loop(0, n_chunks, attn_body, 0)
    for j in range(H_A // 2):
        lo = acc[2 * j] / jnp.sum(l_s[2 * j], axis=1, keepdims=True)
        hi = acc[2 * j + 1] / jnp.sum(l_s[2 * j + 1], axis=1, keepdims=True)
        o_ref[:, j * LANES:(j + 1) * LANES] = jnp.where(lane_lo, lo, hi).astype(o_ref.dtype)


def dsa_prompt(proj, bias_p, rel_bias, tq=LANES):
    b, s, _ = proj.shape
    topk = min(TOPK_MAX, s // 4)
    nc = s // LANES
    kern = functools.partial(_dsa_prompt_kernel, topk)
    return pl.pallas_call(
        kern,
        grid=(b, s // tq),
        in_specs=[pl.BlockSpec((None, tq, A_W), lambda bi, i: (bi, i, CB_Q)),
                  pl.BlockSpec((None, tq, A_W), lambda bi, i: (bi, i, CB_IQ)),
                  pl.BlockSpec((None, tq, LANES), lambda bi, i: (bi, i, CB_IK)),
                  pl.BlockSpec((None, s, A_W), lambda bi, i: (bi, 0, CB_K)),
                  pl.BlockSpec((None, s, A_W), lambda bi, i: (bi, 0, CB_V)),
                  pl.BlockSpec((None, s, LANES), lambda bi, i: (bi, 0, CB_IK)),
                  pl.BlockSpec(bias_p.shape, lambda bi, i: (0, 0, 0, 0)),
                  pl.BlockSpec(memory_space=pltpu.SMEM)],
        out_specs=pl.BlockSpec((None, tq, A_W), lambda bi, i: (bi, i, 0)),
        out_shape=jax.ShapeDtypeStruct((b, s, A_W), BF16),
        scratch_shapes=[pltpu.VMEM((s, A_W), BF16), pltpu.VMEM((s, A_W), BF16), pltpu.VMEM((s, LANES), BF16),
                        pltpu.VMEM((H_A, tq, LANES), BF16), pltpu.VMEM((H_IDX, tq, LANES), BF16),
                        pltpu.VMEM((H_IDX, tq, LANES), F32),
                        pltpu.VMEM((nc, tq, LANES), I32), pltpu.VMEM((nc, tq, LANES), F32),
                        pltpu.VMEM((H_A, nc, tq, LANES), F32),
                        pltpu.VMEM((H_A, tq, LANES), F32), pltpu.VMEM((H_A, tq, LANES), F32),
                        pltpu.VMEM((H_A, tq, LANES), F32)],
        compiler_params=_cparams(("parallel", "arbitrary")),
        name="dsa_prompt",
    )(proj, proj, proj, proj, proj, proj, bias_p, rel_bias)


def _dsa_sample_select_kernel(pps, n_pages, topk, pt_ref, iq_ref, ikiw_ref, *rest):
    page_refs = rest[:pps]
    mb_ref = rest[pps]
    iq2, wb, sc = rest[pps + 1:]
    j = pl.program_id(1)
    t = iq_ref.shape[0]
    shape = (t, LANES)
    nc = n_pages + 1

    @pl.when(j == 0)
    def _():
        iq = iq_ref[...]
        ikiw = ikiw_ref[...]
        for h in range(H_IDX):
            iq2[h * t:(h + 1) * t, :] = iq[:, h * D_IDX:(h + 1) * D_IDX].astype(BF16)
            wb[h * t:(h + 1) * t, :] = jnp.broadcast_to(ikiw[:, D_IDX + h:D_IDX + h + 1], shape)

    def scores(s):
        n = s.shape[1]
        s = jnp.maximum(s, 0.0) * jnp.tile(wb[...], (1, n // LANES))
        return jnp.sum(s.reshape(H_IDX, t, n), axis=0) * IDX_SCALE

    ikt = jnp.concatenate([r[...] for r in page_refs], axis=1).astype(BF16)
    keys = _score_key(scores(jnp.dot(iq2[...], ikt, preferred_element_type=F32)))
    for u in range(pps):
        sc[j * pps + u] = keys[:, u * LANES:(u + 1) * LANES]

    @pl.when(j == pl.num_programs(1) - 1)
    def _():
        row = lax.broadcasted_iota(I32, shape, 0)
        col = lax.broadcasted_iota(I32, shape, 1)
        ik_new = jnp.concatenate([ikiw_ref[:, 0:D_IDX], jnp.zeros((LANES - t, D_IDX), F32)], axis=0)
        s_new = lax.dot_general(iq2[...], ik_new.astype(BF16), (((1,), (1,)), ((), ())), preferred_element_type=F32)
        s_new = jnp.where(col <= row, scores(s_new), NEG)
        sc[n_pages] = jnp.where(col < t, _score_key(s_new), INT_MIN)
        kf = float(topk)
        thr = _kth_largest(sc, nc, kf, shape)

        def causal(kc):
            return (kc < n_pages) | (col <= row)

        _select_mask(sc, mb_ref, nc, kf, thr, shape, causal)


def _dsa_sample_attn_kernel(pps, n_pages, pt_ref, q_ref, kn_ref, vn_ref, mb_ref, mbn_ref, bias_ref, *rest):
    k_refs = rest[:pps]
    v_refs = rest[pps:2 * pps]
    o_ref = rest[2 * pps]
    qbd, acc, m_s, l_s = rest[2 * pps + 1:]
    j = pl.program_id(1)
    t = q_ref.shape[0]
    rows = H_A * t

    def blockdiag(x):
        r = lax.broadcasted_iota(I32, (rows, A_W), 0) // t
        c = lax.broadcasted_iota(I32, (rows, A_W), 1) // HD_A
        return jnp.where(r == c, jnp.tile(x, (H_A, 1)), 0.0)

    @pl.when(j == 0)
    def _():
        qbd[...] = blockdiag(q_ref[...] * (HD_A ** -0.5)).astype(BF16)
        m_s[...] = jnp.full(m_s.shape, NEG, F32)
        l_s[...] = jnp.zeros(l_s.shape, F32)
        acc[...] = jnp.zeros(acc.shape, F32)

    def step(lg, pv_fn):
        m_old = m_s[...]
        m_new = jnp.maximum(m_old, jnp.max(lg, axis=1, keepdims=True))
        p = jnp.exp(lg - m_new)
        alpha = jnp.exp(m_old - m_new)
        l_s[...] = alpha * l_s[...] + jnp.sum(p, axis=1, keepdims=True)
        acc[...] = alpha * acc[...] + pv_fn(p.astype(BF16))
        m_s[...] = m_new

    kt = jnp.concatenate([r[...].reshape(A_W, PAGE) for r in k_refs], axis=1).astype(BF16)
    vt = jnp.concatenate([r[...].reshape(A_W, PAGE) for r in v_refs], axis=1).astype(BF16)
    last = j == pl.num_programs(1) - 1
    bias = jnp.concatenate([bias_ref[0]] * (pps - 1) + [jnp.where(last, bias_ref[1], bias_ref[0])], axis=1)
    mbc = jnp.concatenate([mb_ref[u] for u in range(pps)], axis=1)
    lg = jnp.dot(qbd[...], kt, preferred_element_type=F32) + bias + jnp.tile(mbc, (H_A, 1))
    step(lg, lambda p: lax.dot_general(p, vt, (((1,), (1,)), ((), ())), preferred_element_type=F32))

    @pl.when(last)
    def _():
        pad = jnp.zeros((LANES - t, A_W), F32)
        kn = jnp.concatenate([kn_ref[...], pad], axis=0).astype(BF16)
        vn = jnp.concatenate([vn_ref[...], pad], axis=0).astype(BF16)
        lgn = lax.dot_general(qbd[...], kn, (((1,), (1,)), ((), ())), preferred_element_type=F32)
        step(lgn + bias_ref[2] + jnp.tile(mbn_ref[0], (H_A, 1)),
             lambda p: jnp.dot(p, vn, preferred_element_type=F32))
        res = blockdiag_sum(acc[...] / l_s[...], t)
        o_ref[...] = res.astype(o_ref.dtype)


def blockdiag_sum(x, t):
    c = lax.broadcasted_iota(I32, (t, A_W), 1) // HD_A
    out = jnp.zeros((t, A_W), F32)
    for h in range(H_A):
        out = out + jnp.where(c == h, x[h * t:(h + 1) * t, :], 0.0)
    return out


def dsa_sample(proj, layer, cache_k, cache_v, cache_idx_k, page_table, bias_s, pps_sel=16, pps_att=8):
    db, t, _ = proj.shape
    n_pages = page_table.shape[1]
    nc = n_pages + 1
    topk = min(TOPK_MAX, (n_pages * PAGE + t) // 4)
    n_pool = cache_k.shape[1]
    ck = jnp.transpose(cache_k, (0, 1, 3, 4, 2))
    cv = jnp.transpose(cache_v, (0, 1, 3, 4, 2))
    cik = jnp.swapaxes(cache_idx_k, 2, 3)
    pt = page_table.reshape(-1)

    def page_spec(dims, pps, u):
        zeros = (0,) * len(dims)
        return pl.BlockSpec((None, None) + dims,
                            lambda b, j, ptr: (layer, ptr[b * n_pages + j * pps + u]) + zeros)

    sel = pl.pallas_call(
        functools.partial(_dsa_sample_select_kernel, pps_sel, n_pages, topk),
        grid_spec=pltpu.PrefetchScalarGridSpec(
            num_scalar_prefetch=1,
            grid=(db, n_pages // pps_sel),
            in_specs=[pl.BlockSpec((None, t, A_W), lambda b, j, ptr: (b, 0, CB_IQ)),
                      pl.BlockSpec((None, t, LANES), lambda b, j, ptr: (b, 0, CB_IK))]
                     + [page_spec((D_IDX, PAGE), pps_sel, u) for u in range(pps_sel)],
            out_specs=pl.BlockSpec((None, nc, t, LANES), lambda b, j, ptr: (b, 0, 0, 0)),
            scratch_shapes=[pltpu.VMEM((H_IDX * t, D_IDX), BF16), pltpu.VMEM((H_IDX * t, LANES), F32),
                            pltpu.VMEM((nc, t, LANES), I32)]),
        out_shape=jax.ShapeDtypeStruct((db, nc, t, LANES), F32),
        compiler_params=_cparams(("parallel", "arbitrary")),
        name="dsa_sample_select",
    )(pt, proj, proj, *([cik] * pps_sel))

    return pl.pallas_call(
        functools.partial(_dsa_sample_attn_kernel, pps_att, n_pages),
        grid_spec=pltpu.PrefetchScalarGridSpec(
            num_scalar_prefetch=1,
            grid=(db, n_pages // pps_att),
            in_specs=[pl.BlockSpec((None, t, A_W), lambda b, j, ptr: (b, 0, CB_Q)),
                      pl.BlockSpec((None, t, A_W), lambda b, j, ptr: (b, 0, CB_K)),
                      pl.BlockSpec((None, t, A_W), lambda b, j, ptr: (b, 0, CB_V)),
                      pl.BlockSpec((None, pps_att, t, LANES), lambda b, j, ptr: (b, j, 0, 0)),
                      pl.BlockSpec((None, 1, t, LANES), lambda b, j, ptr: (b, n_pages, 0, 0)),
                      pl.BlockSpec(bias_s.shape, lambda b, j, ptr: (0, 0, 0))]
                     + [page_spec((H_A, HD_A, PAGE), pps_att, u) for u in range(pps_att)] * 2,
            out_specs=pl.BlockSpec((None, t, A_W), lambda b, j, ptr: (b, 0, 0)),
            scratch_shapes=[pltpu.VMEM((H_A * t, A_W), BF16), pltpu.VMEM((H_A * t, A_W), F32),
                            pltpu.VMEM((H_A * t, 1), F32), pltpu.VMEM((H_A * t, 1), F32)]),
        out_shape=jax.ShapeDtypeStruct((db, t, A_W), BF16),
        compiler_params=_cparams(("parallel", "arbitrary")),
        name="dsa_sample_attn",
    )(pt, proj, proj, proj, sel, sel, bias_s, *([ck] * pps_att), *([cv] * pps_att))


def _mm(a, b):
    return jnp.dot(a.astype(BF16), b.astype(BF16), preferred_element_type=F32)


def _mm_nt(a, b):
    return lax.dot_general(a.astype(BF16), b.astype(BF16), (((1,), (1,)), ((), ())), preferred_element_type=F32)


def _split3(x):
    hi = x.astype(BF16)
    r1 = x - hi.astype(F32)
    mid = r1.astype(BF16)
    lo = (r1 - mid.astype(F32)).astype(BF16)
    return hi, mid, lo


def _rwkv_kernel(has_state, t_valid, r_ref, k_ref, v_ref, wa_ref, g_ref, sr_ref, sk_ref, sv_ref, swa_ref, sg_ref,
                 mr_ref, mk_ref, mv_ref, mwa_ref, mg_ref, w2_ref, a2_ref, g2_ref, vec_ref, *rest):
    if has_state:
        s0_ref, y_ref, so_ref, pr, pk, pv, pwa, pg, st = rest
    else:
        y_ref, so_ref, pr, pk, pv, pwa, pg, st = rest
    c = pl.program_id(1)
    C = RW_C
    tb = r_ref.shape[0]
    n_pairs = H_B // 2
    zero_blk = jnp.zeros((HD_B, HD_B), F32)

    @pl.when(c == 0)
    def _():
        if has_state:
            for p in range(n_pairs):
                st[p] = jnp.concatenate([jnp.concatenate([s0_ref[2 * p], zero_blk], axis=1),
                                         jnp.concatenate([zero_blk, s0_ref[2 * p + 1]], axis=1)], axis=0)
        else:
            st[...] = jnp.zeros(st.shape, F32)

    def shifted(p_ref, prev_scr, shift_ref, mu_ref):
        p = p_ref[...]
        if tb < C:
            p = jnp.concatenate([p, jnp.zeros((C - tb, p.shape[1]), F32)], axis=0)
        prev = jnp.where(c == 0, shift_ref[...], prev_scr[...])
        rowi = lax.broadcasted_iota(I32, p.shape, 0)
        ps = jnp.where(rowi == 0, prev, pltpu.roll(p, 1, 0))
        prev_scr[...] = p[C - 1:C, :]
        return p + (ps - p) * mu_ref[...]

    xr = shifted(r_ref, pr, sr_ref, mr_ref)
    xk = shifted(k_ref, pk, sk_ref, mk_ref)
    xv = shifted(v_ref, pv, sv_ref, mv_ref)
    xwa = shifted(wa_ref, pwa, swa_ref, mwa_ref)
    xg = shifted(g_ref, pg, sg_ref, mg_ref)

    vec = vec_ref[...]
    w0, a0, kkp, kap, rkp, lnw, lnb = (vec[n:n + 1, :] for n in range(7))
    zw = w0 + _mm(jnp.tanh(xwa[:, 0:W_LORA]), w2_ref[...])
    w_log = -(jnp.maximum(-zw, 0.0) + jnp.log(1.0 + jnp.exp(-jnp.abs(zw)))) - 0.5
    ld = -jnp.exp(w_log)
    za = a0 + _mm(xwa[:, W_LORA:W_LORA + A_LORA], a2_ref[...])
    a = 1.0 / (1.0 + jnp.exp(-za))
    g = _mm(1.0 / (1.0 + jnp.exp(-xg)), g2_ref[...])

    r2 = lax.broadcasted_iota(I32, (2 * C, 2 * C), 0)
    c2 = lax.broadcasted_iota(I32, (2 * C, 2 * C), 1)
    same_head = (r2 // C) == (c2 // C)
    head_ones = same_head.astype(BF16)
    strict = same_head & ((c2 % C) < (r2 % C))
    incl = same_head & ((c2 % C) <= (r2 % C))

    def pairs(x):
        return jnp.concatenate([x[:, p * LANES:(p + 1) * LANES] for p in range(n_pairs)], axis=0)

    def unpairs(x):
        return jnp.concatenate([x[p * C:(p + 1) * C] for p in range(n_pairs)], axis=1)

    def head_sum(x):
        xs = pairs(x)
        hi = xs.astype(BF16)
        mid = (xs - hi.astype(F32)).astype(BF16)
        return unpairs(jnp.dot(hi, head_ones, preferred_element_type=F32)
                       + jnp.dot(mid, head_ones, preferred_element_type=F32))

    kk = xk * kkp
    kk = kk / jnp.maximum(jnp.sqrt(head_sum(kk * kk)), 1e-12)
    k2 = xk * (1.0 + (a - 1.0) * kap)
    bonus_w = head_sum(xr * k2 * rkp)
    if t_valid < C:
        valid = lax.broadcasted_iota(I32, (C, B_W), 0) < t_valid
        ld = jnp.where(valid, ld, 0.0)
        xv = jnp.where(valid, xv, 0.0)
        kk = jnp.where(valid, kk, 0.0)
        k2m = jnp.where(valid, k2, 0.0)
    else:
        k2m = k2
    rc = lax.broadcasted_iota(I32, (C, C), 0)
    cc = lax.broadcasted_iota(I32, (C, C), 1)
    tril = (cc <= rc).astype(BF16)
    hi, mid, lo = _split3(ld)
    lgc = (jnp.dot(tril, hi, preferred_element_type=F32) + jnp.dot(tril, mid, preferred_element_type=F32)
           + jnp.dot(tril, lo, preferred_element_type=F32))
    lg_end = lgc[C - 1:C, :]
    ginv = jnp.exp(-lgc)
    e_end = jnp.exp(lg_end - lgc)
    g_end = jnp.exp(lg_end)
    kb_ = kk * a
    al = kk * jnp.exp(lgc - ld)
    be = kb_ * ginv
    kt = k2m * ginv
    rt = xr * jnp.exp(lgc)
    bp = kb_ * e_end
    kp = k2m * e_end

    lane_lo = lax.broadcasted_iota(I32, (C, LANES), 1) < HD_B

    def stack(x, p):
        xp = x[:, p * LANES:(p + 1) * LANES]
        return jnp.concatenate([jnp.where(lane_lo, xp, 0.0), jnp.where(lane_lo, 0.0, xp)], axis=0)

    P = range(n_pairs)
    bf = lambda x: x.astype(BF16)
    lhs = [bf(jnp.concatenate([stack(al, p), stack(rt, p)], axis=0)) for p in P]
    vs = [bf(stack(xv, p)) for p in P]
    gb = [_mm_nt(lhs[p], stack(be, p)) for p in P]
    gk = [_mm_nt(lhs[p], stack(kt, p)) for p in P]
    p0 = [_mm_nt(lhs[p], st[p]) for p in P]
    qn = [bf(jnp.where(strict, -gb[p][0:2 * C], 0.0)) for p in P]
    u = [p0[p][0:2 * C] + _mm(jnp.where(strict, gk[p][0:2 * C], 0.0), vs[p]) for p in P]
    u = [u[p] + _mm(qn[p], u[p]) for p in P]
    for _ in range(int(math.log2(C)) - 1):
        qn = [bf(_mm(qn[p], qn[p])) for p in P]
        u = [u[p] + _mm(qn[p], u[p]) for p in P]
    m2 = [jnp.concatenate([bf(jnp.where(incl, -gb[p][2 * C:], 0.0)), bf(jnp.where(incl, gk[p][2 * C:], 0.0))], axis=1)
          for p in P]
    uv = [jnp.concatenate([bf(u[p]), vs[p]], axis=0) for p in P]
    yp = [p0[p][2 * C:] + _mm(m2[p], uv[p]) for p in P]
    upd = [lax.dot_general(jnp.concatenate([bf(-u[p]), vs[p]], axis=0),
                           bf(jnp.concatenate([stack(bp, p), stack(kp, p)], axis=0)),
                           (((0,), (0,)), ((), ())), preferred_element_type=F32) for p in P]
    for p in P:
        st[p] = st[p] * g_end[:, p * LANES:(p + 1) * LANES] + upd[p]
    y = jnp.concatenate([yp[p][0:C] + yp[p][C:2 * C] for p in P], axis=1)
    mu = head_sum(y) * (1.0 / HD_B)
    dy = y - mu
    var = head_sum(dy * dy) * (1.0 / HD_B)
    yn = dy * lax.rsqrt(var + GN_EPS) * lnw + lnb
    out = (yn + bonus_w * xv) * g
    y_ref[...] = out[0:tb].astype(y_ref.dtype)

    @pl.when(c == pl.num_programs(1) - 1)
    def _():
        for p in range(n_pairs):
            s = st[p]
            so_ref[2 * p] = s[0:HD_B, 0:HD_B]
            so_ref[2 * p + 1] = s[HD_B:, HD_B:]


def rwkv_mix(proj, shift_full, mu_full, w2, a2, g2p, vecs, s0):
    b, t, _ = proj.shape
    tb = min(t, RW_C)
    nch = t // tb
    has_state = s0 is not None
    cbs = [(B_W, OFF_R // B_W), (B_W, OFF_R // B_W + 1), (B_W, OFF_R // B_W + 2), (LANES, CB_WA), (256, CB_G)]

    in_specs = ([pl.BlockSpec((None, tb, w), functools.partial(lambda cb, bi, c: (bi, c, cb), cb)) for w, cb in cbs]
                + [pl.BlockSpec((None, 1, w), functools.partial(lambda cb, bi, c: (bi, 0, cb), cb)) for w, cb in cbs]
                + [pl.BlockSpec((1, w), functools.partial(lambda cb, bi, c: (0, cb), cb)) for w, cb in cbs]
                + [pl.BlockSpec(w2.shape, lambda bi, c: (0, 0)),
                   pl.BlockSpec(a2.shape, lambda bi, c: (0, 0)),
                   pl.BlockSpec(g2p.shape, lambda bi, c: (0, 0)),
                   pl.BlockSpec(vecs.shape, lambda bi, c: (0, 0))])
    args = [proj] * 5 + [shift_full] * 5 + [mu_full] * 5 + [w2, a2, g2p, vecs]
    if has_state:
        in_specs.append(pl.BlockSpec((None, H_B, HD_B, HD_B), lambda bi, c: (bi, 0, 0, 0)))
        args.append(s0)
    return pl.pallas_call(
        functools.partial(_rwkv_kernel, has_state, tb),
        grid=(b, nch),
        in_specs=in_specs,
        out_specs=[pl.BlockSpec((None, tb, B_W), lambda bi, c: (bi, c, 0)),
                   pl.BlockSpec((None, H_B, HD_B, HD_B), lambda bi, c: (bi, 0, 0, 0))],
        out_shape=[jax.ShapeDtypeStruct((b, t, B_W), BF16),
                   jax.ShapeDtypeStruct((b, H_B, HD_B, HD_B), F32)],
        scratch_shapes=[pltpu.VMEM((1, B_W), F32)] * 3 + [pltpu.VMEM((1, LANES), F32), pltpu.VMEM((1, 256), F32),
                                                          pltpu.VMEM((H_B // 2, 2 * HD_B, 2 * HD_B), F32)],
        compiler_params=_cparams(("parallel", "arbitrary")),
        name="rwkv_mix",
    )(*args)


def _gelu(x):
    return 0.5 * x * (1.0 + jnp.tanh(math.sqrt(2.0 / math.pi) * (x + 0.044715 * (x * x * x))))


def _gmlp_kernel(pu_ref, pv_ref, lnw_ref, lnb_ref, ws_ref, bst_ref, y_ref, gv_ref):
    tc = pu_ref.shape[0]
    u = _gelu(pu_ref[...])
    vf = _gelu(pv_ref[...])
    mu = jnp.mean(vf, axis=1, keepdims=True)
    var = jnp.mean(jnp.square(vf - mu), axis=1, keepdims=True)
    v = (vf - mu) * lax.rsqrt(var + 1e-5) * lnw_ref[...] + lnb_ref[...]
    gv_ref[...] = v
    ri = lax.broadcasted_iota(I32, (CHUNK, CHUNK), 0)
    ci = lax.broadcasted_iota(I32, (CHUNK, CHUNK), 1)
    if tc < CHUNK:
        v = jnp.concatenate([v, jnp.zeros((CHUNK - tc, C_W), F32)], axis=0)
    vb = v.astype(BF16)
    for g in range(G_C):
        sl = slice(g * CD, (g + 1) * CD)
        wm = jnp.where(ci <= ri, ws_ref[g], 0.0).astype(BF16)
        sp = jnp.dot(wm, vb[:, sl], preferred_element_type=F32)[0:tc] + bst_ref[0:tc, g:g + 1]
        y_ref[:, sl] = (u[:, sl] * sp).astype(y_ref.dtype)


def gmlp_mix(proj, lnw, lnb, ws, bst):
    b, t, _ = proj.shape
    tc = min(t, CHUNK)
    return pl.pallas_call(
        _gmlp_kernel,
        grid=(b, t // tc),
        in_specs=[pl.BlockSpec((None, tc, C_W), lambda bi, c: (bi, c, CB_U)),
                  pl.BlockSpec((None, tc, C_W), lambda bi, c: (bi, c, CB_PV)),
                  pl.BlockSpec((1, C_W), lambda bi, c: (0, 0)),
                  pl.BlockSpec((1, C_W), lambda bi, c: (0, 0)),
                  pl.BlockSpec(ws.shape, lambda bi, c: (0, 0, 0)),
                  pl.BlockSpec(bst.shape, lambda bi, c: (0, 0))],
        out_specs=[pl.BlockSpec((None, tc, C_W), lambda bi, c: (bi, c, 0)),
                   pl.BlockSpec((None, tc, C_W), lambda bi, c: (bi, c, 0))],
        out_shape=[jax.ShapeDtypeStruct((b, t, C_W), BF16), jax.ShapeDtypeStruct((b, t, C_W), F32)],
        compiler_params=_cparams(("parallel", "parallel")),
        name="gmlp_mix",
    )(proj, proj, lnw.reshape(1, C_W), lnb.reshape(1, C_W), ws, bst)


def _layer(x, lw, attend, shift_full, s0, tm):
    b, t, d = x.shape
    x2 = x.reshape(b * t, d)
    proj = norm_matmul(x2, lw['norm_mix'], lw['w_in'], min(2 * tm, b * t), 512).reshape(b, t, PROJ_P)
    ya = attend(proj)
    yb, wkv = rwkv_mix(proj, shift_full, lw['mu_full'], lw['w2'], lw['a2'], lw['g2p'], lw['vecs'], s0)
    yc, gv = gmlp_mix(proj, lw['gmlp_ln_w'], lw['gmlp_ln_b'], lw['gmlp_ws'], lw['gmlp_bst'])
    h = out_proj(x2, ya.reshape(b * t, A_W), yb.reshape(b * t, B_W), yc.reshape(b * t, C_W), lw['w_out'], tm)
    y = ffn(h, lw['norm_ffn'], lw['ffn_gate'], lw['ffn_up'], lw['ffn_down'], tm, 512)
    k = proj[:, :, A_W:2 * A_W].reshape(b, t, H_A, HD_A)
    v = proj[:, :, 2 * A_W:3 * A_W].reshape(b, t, H_A, HD_A)
    ik = proj[:, :, OFF_IK:OFF_IK + D_IDX]
    shift_new = _rwkv_cols(proj[:, t - 1])
    return y.reshape(b, t, d), k, v, ik, shift_new, wkv, gv


def kernel(x_prompt, x_sample, cache_k, cache_v, cache_idx_k, state_rwkv_shift, state_rwkv_wkv, page_table, rel_bias, norm_mix, w_in, rwkv_mu, rwkv_w0, rwkv_w2, rwkv_a0, rwkv_a2, rwkv_g2, rwkv_kk, rwkv_ka, rwkv_rk, rwkv_ln_w, rwkv_ln_b, gmlp_ln_w, gmlp_ln_b, gmlp_ws, gmlp_b, w_out, norm_ffn, ffn_gate, ffn_up, ffn_down, norm_final):
    depth = w_in.shape[0]
    nbp, seq, d = x_prompt.shape
    db, dec_seq, _ = x_sample.shape
    assert d == D_MODEL and seq % 512 == 0 and dec_seq == SUBLANES and w_in.shape[2] == ORIG_GM + 2 * C_W

    w_in_p = _pad_cols(w_in).astype(BF16)
    w_out_b = w_out.astype(BF16)
    wg_b, wu_b, wd_b = ffn_gate.astype(BF16), ffn_up.astype(BF16), ffn_down.astype(BF16)
    mu_full = _pad_rwkv_cols(rwkv_mu)
    g2p = jnp.concatenate([rwkv_g2, jnp.zeros((depth, 256 - G_LORA, B_W), F32)], axis=1)
    vecs = jnp.stack([rwkv_w0, rwkv_a0, rwkv_kk, rwkv_ka, rwkv_rk, rwkv_ln_w, rwkv_ln_b,
                      jnp.zeros_like(rwkv_w0)], axis=1)
    bst = jnp.swapaxes(gmlp_b, 1, 2)
    shift_s = _pad_rwkv_cols(state_rwkv_shift)[:, :, None, :]
    shift_p = jnp.zeros((nbp, 1, PROJ_P), F32)
    bias_p, bias_s = bias_tiles(rel_bias, dec_seq)

    xp, xs = x_prompt, x_sample
    outs = [[] for _ in range(11)]
    for l in range(depth):
        lw = {'norm_mix': norm_mix[l], 'w_in': w_in_p[l], 'mu_full': mu_full[l:l + 1], 'w2': rwkv_w2[l],
              'a2': rwkv_a2[l], 'g2p': g2p[l], 'vecs': vecs[l], 'gmlp_ln_w': gmlp_ln_w[l],
              'gmlp_ln_b': gmlp_ln_b[l], 'gmlp_ws': gmlp_ws[l], 'gmlp_bst': bst[l], 'w_out': w_out_b[l],
              'norm_ffn': norm_ffn[l], 'ffn_gate': wg_b[l], 'ffn_up': wu_b[l], 'ffn_down': wd_b[l]}
        attend_p = functools.partial(dsa_prompt, bias_p=bias_p, rel_bias=rel_bias)
        xp, kp, vp, ikp, shp, wkvp, _ = _layer(xp, lw, attend_p, shift_p, None, 512)
        attend_s = functools.partial(dsa_sample, layer=l, cache_k=cache_k, cache_v=cache_v,
                                     cache_idx_k=cache_idx_k, page_table=page_table, bias_s=bias_s)
        xs, k_s, v_s, ik_s, sh_s, wkv_s, gv_s = _layer(xs, lw, attend_s, shift_s[l], state_rwkv_wkv[l],
                                                       db * dec_seq)
        for lst, val in zip(outs, (kp, vp, ikp, shp, wkvp, k_s, v_s, ik_s, sh_s, wkv_s, gv_s)):
            lst.append(val)
    y_prompt = rms_final(xp.reshape(nbp * seq, d), norm_final, 512).reshape(nbp, seq, d)
    y_sample = rms_final(xs.reshape(db * dec_seq, d), norm_final, db * dec_seq).reshape(db, dec_seq, d)
    return (y_prompt, y_sample) + tuple(jnp.stack(o) for o in outs)
```

```python
import functools
import math

import numpy as np
import jax
import jax.numpy as jnp
from jax import lax
from jax.experimental import pallas as pl
from jax.experimental.pallas import tpu as pltpu

F32 = jnp.float32
BF16 = jnp.bfloat16
I32 = jnp.int32

LANES = 128
SUBLANES = 8
VMEM_LIMIT = 56 * 1024 * 1024

D_MODEL = 2048
HD_A = 64
H_A = 8
A_W = H_A * HD_A
H_IDX = 8
D_IDX = 64
IDX_SCALE = (D_IDX ** -0.5) * (H_IDX ** -0.5)
TOPK_MAX = 256
N_BUCKETS = 32
MAX_DISTANCE = 128
HD_B = 64
B_W = 1024
H_B = B_W // HD_B
W_LORA = 64
A_LORA = 64
G_LORA = 160
RWKV_PROJ_W = 3 * B_W + W_LORA + A_LORA + G_LORA
GN_EPS = 64e-5
C_W = 512
CHUNK = 128
CD = 64
G_C = C_W // CD
D_FF = 5632
NORM_EPS = 1e-6
NEG = -1e30
PAGE = 128
INT_MIN = -(2 ** 31)

ORIG_IK = 3 * A_W + H_IDX * D_IDX
ORIG_R = ORIG_IK + D_IDX + H_IDX
ORIG_GM = ORIG_R + RWKV_PROJ_W
OFF_R = 2048
OFF_IK = OFF_R + 3 * B_W
OFF_WA = OFF_IK + LANES
OFF_G = OFF_WA + W_LORA + A_LORA
OFF_GM = OFF_G + 256
PROJ_P = OFF_GM + 2 * C_W
CB_Q, CB_K, CB_V, CB_IQ = 0, 1, 2, 3
CB_IK = OFF_IK // LANES
CB_WA = OFF_WA // LANES
CB_G = OFF_G // 256
CB_U = OFF_GM // C_W
CB_PV = CB_U + 1
RW_C = 64
assert OFF_R % B_W == 0 and OFF_G % 256 == 0 and OFF_GM % C_W == 0


def _cparams(sem):
    return pltpu.CompilerParams(dimension_semantics=sem, vmem_limit_bytes=VMEM_LIMIT)


def _pad_cols(a):
    z = lambda n: jnp.zeros(a.shape[:-1] + (n,), a.dtype)
    lora = ORIG_R + 3 * B_W
    return jnp.concatenate([a[..., :ORIG_IK], a[..., ORIG_R:lora], a[..., ORIG_IK:ORIG_R], z(OFF_WA - OFF_IK - D_IDX - H_IDX),
                            a[..., lora:ORIG_GM], z(OFF_GM - OFF_G - G_LORA), a[..., ORIG_GM:]], axis=-1)


def _pad_rwkv_cols(a):
    z = lambda n: jnp.zeros(a.shape[:-1] + (n,), a.dtype)
    return jnp.concatenate([z(OFF_R), a[..., :3 * B_W], z(LANES), a[..., 3 * B_W:], z(PROJ_P - OFF_G - G_LORA)], axis=-1)


def _rwkv_cols(p):
    return jnp.concatenate([p[..., OFF_R:OFF_R + 3 * B_W], p[..., OFF_WA:OFF_WA + RWKV_PROJ_W - 3 * B_W]], axis=-1)


def _norm_matmul_kernel(x_ref, g_ref, w_ref, o_ref, xn_ref):
    @pl.when(pl.program_id(1) == 0)
    def _():
        x = x_ref[...]
        y = x * lax.rsqrt(jnp.mean(x * x, axis=-1, keepdims=True) + NORM_EPS)
        xn_ref[...] = (y * g_ref[...]).astype(BF16)

    o_ref[...] = jnp.dot(xn_ref[...], w_ref[...], preferred_element_type=F32)


def norm_matmul(x, g, w, tm, tn):
    m, d = x.shape
    n = w.shape[1]
    return pl.pallas_call(
        _norm_matmul_kernel,
        grid=(m // tm, n // tn),
        in_specs=[pl.BlockSpec((tm, d), lambda i, j: (i, 0)),
                  pl.BlockSpec((1, d), lambda i, j: (0, 0)),
                  pl.BlockSpec((d, tn), lambda i, j: (0, j))],
        out_specs=pl.BlockSpec((tm, tn), lambda i, j: (i, j)),
        out_shape=jax.ShapeDtypeStruct((m, n), F32),
        scratch_shapes=[pltpu.VMEM((tm, d), BF16)],
        compiler_params=_cparams(("parallel", "arbitrary")),
        name="norm_matmul",
    )(x, g.reshape(1, d), w)


def _out_proj_kernel(x_ref, ya_ref, yb_ref, yc_ref, w_ref, o_ref):
    acc = jnp.dot(ya_ref[...], w_ref[0:A_W, :], preferred_element_type=F32)
    acc += jnp.dot(yb_ref[...], w_ref[A_W:A_W + B_W, :], preferred_element_type=F32)
    acc += jnp.dot(yc_ref[...], w_ref[A_W + B_W:, :], preferred_element_type=F32)
    o_ref[...] = x_ref[...] + acc


def out_proj(x, ya, yb, yc, w, tm):
    m, d = x.shape
    return pl.pallas_call(
        _out_proj_kernel,
        grid=(m // tm,),
        in_specs=[pl.BlockSpec((tm, d), lambda i: (i, 0)),
                  pl.BlockSpec((tm, A_W), lambda i: (i, 0)),
                  pl.BlockSpec((tm, B_W), lambda i: (i, 0)),
                  pl.BlockSpec((tm, C_W), lambda i: (i, 0)),
                  pl.BlockSpec(w.shape, lambda i: (0, 0))],
        out_specs=pl.BlockSpec((tm, d), lambda i: (i, 0)),
        out_shape=jax.ShapeDtypeStruct((m, d), F32),
        compiler_params=_cparams(("parallel",)),
        name="out_proj",
    )(x, ya, yb, yc, w)


def _ffn_kernel(h_ref, g_ref, wg_ref, wu_ref, wd_ref, o_ref, hn_ref, acc_ref):
    f = pl.program_id(1)

    @pl.when(f == 0)
    def _():
        x = h_ref[...]
        y = x * lax.rsqrt(jnp.mean(x * x, axis=-1, keepdims=True) + NORM_EPS)
        hn_ref[...] = (y * g_ref[...]).astype(BF16)
        acc_ref[...] = jnp.zeros_like(acc_ref)

    hn = hn_ref[...]
    gate = jnp.dot(hn, wg_ref[...], preferred_element_type=F32)
    up = jnp.dot(hn, wu_ref[...], preferred_element_type=F32)
    act = (gate / (1.0 + jnp.exp(-gate))) * up
    acc_ref[...] += jnp.dot(act.astype(BF16), wd_ref[...], preferred_element_type=F32)

    @pl.when(f == pl.num_programs(1) - 1)
    def _():
        o_ref[...] = h_ref[...] + acc_ref[...]


def ffn(h, g, wg, wu, wd, tm, tf):
    m, d = h.shape
    nf = wg.shape[1]
    return pl.pallas_call(
        _ffn_kernel,
        grid=(m // tm, nf // tf),
        in_specs=[pl.BlockSpec((tm, d), lambda i, f: (i, 0)),
                  pl.BlockSpec((1, d), lambda i, f: (0, 0)),
                  pl.BlockSpec((d, tf), lambda i, f: (0, f)),
                  pl.BlockSpec((d, tf), lambda i, f: (0, f)),
                  pl.BlockSpec((tf, d), lambda i, f: (f, 0))],
        out_specs=pl.BlockSpec((tm, d), lambda i, f: (i, 0)),
        out_shape=jax.ShapeDtypeStruct((m, d), F32),
        scratch_shapes=[pltpu.VMEM((tm, d), BF16), pltpu.VMEM((tm, d), F32)],
        compiler_params=_cparams(("parallel", "arbitrary")),
        name="ffn",
    )(h, g.reshape(1, d), wg, wu, wd)


def _rms_kernel(x_ref, g_ref, o_ref):
    x = x_ref[...]
    o_ref[...] = x * lax.rsqrt(jnp.mean(x * x, axis=-1, keepdims=True) + NORM_EPS) * g_ref[...]


def rms_final(x, g, tm):
    m, d = x.shape
    return pl.pallas_call(
        _rms_kernel,
        grid=(m // tm,),
        in_specs=[pl.BlockSpec((tm, d), lambda i: (i, 0)), pl.BlockSpec((1, d), lambda i: (0, 0))],
        out_specs=pl.BlockSpec((tm, d), lambda i: (i, 0)),
        out_shape=jax.ShapeDtypeStruct((m, d), F32),
        compiler_params=_cparams(("parallel",)),
        name="rms_final",
    )(x, g.reshape(1, d))


def _bucket_np(dist):
    max_exact = N_BUCKETS // 2
    n = np.maximum(dist, 0)
    nf = np.maximum(n, 1).astype(np.float64)
    large = max_exact + (np.log(nf / max_exact) / math.log(MAX_DISTANCE / max_exact)
                         * (N_BUCKETS - max_exact)).astype(np.int32)
    return np.where(n < max_exact, n, np.minimum(large, N_BUCKETS - 1)).astype(np.int32)


def _bias_tiles_kernel(rb_ref, bp_ref, bs_ref, op_ref, os_ref):
    for d in range(2):
        bk = bp_ref[d]
        for h in range(H_A):
            t = jnp.zeros(bk.shape, F32)
            for b in range(N_BUCKETS):
                t = jnp.where(bk == b, rb_ref[b, h], t)
            op_ref[d, h] = t
    for d in range(3):
        bk = bs_ref[d]
        for h in range(H_A):
            t = jnp.zeros(bk.shape, F32)
            for b in range(N_BUCKETS):
                t = jnp.where(bk == b, rb_ref[b, h], t)
            os_ref[d, h * SUBLANES:(h + 1) * SUBLANES, :] = t


def bias_tiles(rel_bias, dec_seq):
    r = np.arange(LANES)[:, None]
    c = np.arange(LANES)[None, :]
    bp = np.stack([_bucket_np(r - c), _bucket_np(LANES + r - c)])
    t = np.arange(dec_seq)[:, None]
    bs = np.stack([_bucket_np(np.full((dec_seq, LANES), 4 * MAX_DISTANCE)),
                   _bucket_np(PAGE + t - c), _bucket_np(t - c)])
    vm = pl.BlockSpec(memory_space=pltpu.VMEM)
    return pl.pallas_call(
        _bias_tiles_kernel,
        in_specs=[pl.BlockSpec(memory_space=pltpu.SMEM), vm, vm],
        out_specs=[vm, vm],
        out_shape=[jax.ShapeDtypeStruct((2, H_A, LANES, LANES), F32),
                   jax.ShapeDtypeStruct((3, H_A * dec_seq, LANES), F32)],
        name="bias_tiles",
    )(rel_bias, jnp.asarray(bp), jnp.asarray(bs))


def _score_key(s):
    s = jnp.where(s == 0.0, 0.0, s)
    bits = lax.bitcast_convert_type(s, I32)
    return jnp.where(bits < 0, bits ^ 0x7FFFFFFF, bits)


def _kth_largest(sc_ref, n_chunks, kf, shape):
    def count_ge(cand):
        if isinstance(n_chunks, int):
            a = jnp.sum((sc_ref[...] >= cand[None]).astype(F32), axis=0)
        else:
            parts = []
            for r0 in range(0, shape[0], LANES):
                cs = cand[r0:r0 + LANES]

                def body(kc, a, r0=r0, cs=cs):
                    return a + (sc_ref[kc, r0:r0 + LANES, :] >= cs).astype(F32)

                parts.append(lax.fori_loop(0, n_chunks, body, jnp.zeros((LANES, LANES), F32)))
            a = parts[0] if len(parts) == 1 else jnp.concatenate(parts, axis=0)
        return jnp.sum(a, axis=1, keepdims=True)

    zero = jnp.zeros(shape, I32)
    t0 = jnp.where(count_ge(zero) >= kf, zero, jnp.full(shape, INT_MIN, I32))

    def bit_body(it, t):
        cand = t | jnp.left_shift(jnp.int32(1), 30 - it)
        return jnp.where(count_ge(cand) >= kf, cand, t)

    return lax.fori_loop(0, 31, bit_body, t0)


def _select_mask(sc_ref, mb_ref, n_chunks, kf, thr, shape, causal_fn):
    def cnt_body(kc, carry):
        key = sc_ref[kc]
        return carry[0] + (key > thr).astype(F32), carry[1] + (key >= thr).astype(F32)

    zeros = jnp.zeros(shape, F32)
    if isinstance(n_chunks, int):
        keys = sc_ref[...]
        cgt = jnp.sum((keys > thr[None]).astype(F32), axis=0)
        cge = jnp.sum((keys >= thr[None]).astype(F32), axis=0)
    else:
        cgt, cge = lax.fori_loop(0, n_chunks, cnt_body, (zeros, zeros))
    need = kf - jnp.sum(cgt, axis=1, keepdims=True)
    cge = jnp.sum(cge, axis=1, keepdims=True)

    def exact_k():
        def body(kc, carry):
            mb_ref[kc] = jnp.where((sc_ref[kc] >= thr) & causal_fn(kc), 0.0, NEG)
            return carry

        lax.fori_loop(0, n_chunks, body, 0, unroll=8 if isinstance(n_chunks, int) else 1)

    def with_ties():
        rr = lax.broadcasted_iota(I32, (LANES, LANES), 0)
        cc = lax.broadcasted_iota(I32, (LANES, LANES), 1)
        tri = (rr <= cc).astype(BF16)

        def body(kc, offs):
            key = sc_ref[kc]
            eq = key == thr
            pre = jnp.dot(eq.astype(BF16), tri, preferred_element_type=F32) + offs
            sel = (key > thr) | (eq & (pre <= need))
            mb_ref[kc] = jnp.where(sel & causal_fn(kc), 0.0, NEG)
            return jnp.broadcast_to(pre[:, LANES - 1:LANES], shape)

        lax.fori_loop(0, n_chunks, body, zeros)

    lax.cond(jnp.max(jnp.abs(cge - kf)) > 0.5, with_ties, exact_k)


def _dsa_prompt_kernel(topk, q_ref, iq_ref, iw_ref, k_ref, v_ref, ik_ref, bias_ref, rb_ref, o_ref,
                       kb, vb, ikd, qm, iqm, iwb, sc, mb, lgs, acc, m_s, l_s):
    i = pl.program_id(1)
    tq = q_ref.shape[0]
    shape = (tq, LANES)
    nt = (((1,), (1,)), ((), ()))

    @pl.when(i == 0)
    def _():
        kb[...] = k_ref[...].astype(BF16)
        vb[...] = v_ref[...].astype(BF16)
        ik = ik_ref[:, 0:D_IDX].astype(BF16)
        ikd[...] = jnp.concatenate([ik, ik], axis=1)

    row = lax.broadcasted_iota(I32, shape, 0) + i * tq
    col = lax.broadcasted_iota(I32, shape, 1)
    lane_lo = col < HD_A
    sub = tq // LANES
    n_chunks = (i + 1) * sub
    iw = iw_ref[:, D_IDX:D_IDX + H_IDX]
    for h in range(H_A):
        ps = slice((h // 2) * LANES, (h // 2 + 1) * LANES)
        keep = lane_lo if h % 2 == 0 else jnp.logical_not(lane_lo)
        qm[h] = jnp.where(keep, q_ref[:, ps] * (HD_A ** -0.5), 0.0).astype(BF16)
        iqm[h] = jnp.where(keep, iq_ref[:, ps], 0.0).astype(BF16)
        iwb[h] = jnp.broadcast_to(iw[:, h:h + 1], shape)

    def causal(kc):
        return (col + kc * LANES) <= row

    def score_body(kc, carry):
        ikc = ikd[pl.ds(pl.multiple_of(kc * LANES, LANES), LANES), :]
        ss = [lax.dot_general(iqm[h], ikc, nt, preferred_element_type=F32) for h in range(H_IDX)]
        ws = [iwb[h] * jnp.maximum(ss[h], 0.0) for h in range(H_IDX)]
        tot = ((ws[0] + ws[1]) + (ws[2] + ws[3])) + ((ws[4] + ws[5]) + (ws[6] + ws[7]))
        tot = jnp.where(causal(kc), tot * IDX_SCALE, NEG)
        sc[kc] = _score_key(tot)
        return carry

    lax.fori_loop(0, n_chunks, score_body, 0)
    kf = float(topk)
    thr = _kth_largest(sc, n_chunks, kf, shape)
    _select_mask(sc, mb, n_chunks, kf, thr, shape, causal)

    m_s[...] = jnp.full(m_s.shape, NEG, F32)

    def logit_body(kc, carry):
        off = pl.multiple_of(kc * LANES, LANES)
        mbc = mb[kc]
        for h in range(H_A):
            ps = slice((h // 2) * LANES, (h // 2 + 1) * LANES)
            lg = lax.dot_general(qm[h], kb[pl.ds(off, LANES), ps], nt, preferred_element_type=F32)
            tiles = []
            for s in range(sub):
                d = i * sub + s - kc
                tiles.append(jnp.where(d == 0, bias_ref[0, h],
                                       jnp.where(d == 1, bias_ref[1, h], rb_ref[N_BUCKETS - 1, h])))
            bias = tiles[0] if sub == 1 else jnp.concatenate(tiles, axis=0)
            lg = lg + bias + mbc
            lgs[h, kc] = lg
            m_s[h] = jnp.maximum(m_s[h], lg)
        return carry

    lax.fori_loop(0, n_chunks, logit_body, 0)
    for h in range(H_A):
        m_s[h] = jnp.broadcast_to(jnp.max(m_s[h], axis=1, keepdims=True), shape)
    l_s[...] = jnp.zeros(l_s.shape, F32)
    acc[...] = jnp.zeros(acc.shape, F32)

    def attn_body(kc, carry):
        off = pl.multiple_of(kc * LANES, LANES)
        for h in range(H_A):
            ps = slice((h // 2) * LANES, (h // 2 + 1) * LANES)
            p = jnp.exp(lgs[h, kc] - m_s[h])
            l_s[h] += p
            acc[h] += jnp.dot(p.astype(BF16), vb[pl.ds(off, LANES), ps], preferred_element_type=F32)
        return carry

    lax.fori_loop(0, n_chunks, attn_body, 0)
    for j in range(H_A // 2):
        lo = acc[2 * j] / jnp.sum(l_s[2 * j], axis=1, keepdims=True)
        hi = acc[2 * j + 1] / jnp.sum(l_s[2 * j + 1], axis=1, keepdims=True)
        o_ref[:, j * LANES:(j + 1) * LANES] = jnp.where(lane_lo, lo, hi).astype(o_ref.dtype)


def dsa_prompt(proj, bias_p, rel_bias, tq=2 * LANES):
    b, s, _ = proj.shape
    once = pl.Buffered(1)
    topk = min(TOPK_MAX, s // 4)
    nc = s // LANES
    kern = functools.partial(_dsa_prompt_kernel, topk)
    return pl.pallas_call(
        kern,
        grid=(b, s // tq),
        in_specs=[pl.BlockSpec((None, tq, A_W), lambda bi, i: (bi, i, CB_Q)),
                  pl.BlockSpec((None, tq, A_W), lambda bi, i: (bi, i, CB_IQ)),
                  pl.BlockSpec((None, tq, LANES), lambda bi, i: (bi, i, CB_IK)),
                  pl.BlockSpec((None, s, A_W), lambda bi, i: (bi, 0, CB_K), pipeline_mode=once),
                  pl.BlockSpec((None, s, A_W), lambda bi, i: (bi, 0, CB_V), pipeline_mode=once),
                  pl.BlockSpec((None, s, LANES), lambda bi, i: (bi, 0, CB_IK), pipeline_mode=once),
                  pl.BlockSpec(bias_p.shape, lambda bi, i: (0, 0, 0, 0), pipeline_mode=once),
                  pl.BlockSpec(memory_space=pltpu.SMEM)],
        out_specs=pl.BlockSpec((None, tq, A_W), lambda bi, i: (bi, i, 0)),
        out_shape=jax.ShapeDtypeStruct((b, s, A_W), BF16),
        scratch_shapes=[pltpu.VMEM((s, A_W), BF16), pltpu.VMEM((s, A_W), BF16), pltpu.VMEM((s, LANES), BF16),
                        pltpu.VMEM((H_A, tq, LANES), BF16), pltpu.VMEM((H_IDX, tq, LANES), BF16),
                        pltpu.VMEM((H_IDX, tq, LANES), F32),
                        pltpu.VMEM((nc, tq, LANES), I32), pltpu.VMEM((nc, tq, LANES), F32),
                        pltpu.VMEM((H_A, nc, tq, LANES), F32),
                        pltpu.VMEM((H_A, tq, LANES), F32), pltpu.VMEM((H_A, tq, LANES), F32),
                        pltpu.VMEM((H_A, tq, LANES), F32)],
        compiler_params=_cparams(("parallel", "arbitrary")),
        name="dsa_prompt",
    )(proj, proj, proj, proj, proj, proj, bias_p, rel_bias)


def _dsa_sample_select_kernel(pps, n_pages, topk, pt_ref, iq_ref, ikiw_ref, *rest):
    page_refs = rest[:pps]
    mb_ref = rest[pps]
    iq2, wb, sc = rest[pps + 1:]
    j = pl.program_id(1)
    t = iq_ref.shape[0]
    shape = (t, LANES)
    nc = n_pages + 1

    @pl.when(j == 0)
    def _():
        iq = iq_ref[...]
        ikiw = ikiw_ref[...]
        for h in range(H_IDX):
            iq2[h * t:(h + 1) * t, :] = iq[:, h * D_IDX:(h + 1) * D_IDX].astype(BF16)
            wb[h * t:(h + 1) * t, :] = jnp.broadcast_to(ikiw[:, D_IDX + h:D_IDX + h + 1], shape)

    def scores(s):
        n = s.shape[1]
        s = jnp.maximum(s, 0.0) * jnp.tile(wb[...], (1, n // LANES))
        return jnp.sum(s.reshape(H_IDX, t, n), axis=0) * IDX_SCALE

    ikt = jnp.concatenate([r[...] for r in page_refs], axis=1).astype(BF16)
    keys = _score_key(scores(jnp.dot(iq2[...], ikt, preferred_element_type=F32)))
    for u in range(pps):
        sc[j * pps + u] = keys[:, u * LANES:(u + 1) * LANES]

    @pl.when(j == pl.num_programs(1) - 1)
    def _():
        row = lax.broadcasted_iota(I32, shape, 0)
        col = lax.broadcasted_iota(I32, shape, 1)
        ik_new = jnp.concatenate([ikiw_ref[:, 0:D_IDX], jnp.zeros((LANES - t, D_IDX), F32)], axis=0)
        s_new = lax.dot_general(iq2[...], ik_new.astype(BF16), (((1,), (1,)), ((), ())), preferred_element_type=F32)
        s_new = jnp.where(col <= row, scores(s_new), NEG)
        sc[n_pages] = jnp.where(col < t, _score_key(s_new), INT_MIN)
        kf = float(topk)
        thr = _kth_largest(sc, nc, kf, shape)

        def causal(kc):
            return (kc < n_pages) | (col <= row)

        _select_mask(sc, mb_ref, nc, kf, thr, shape, causal)


def _dsa_sample_attn_kernel(pps, n_pages, pt_ref, q_ref, kn_ref, vn_ref, mb_ref, mbn_ref, bias_ref, *rest):
    k_refs = rest[:pps]
    v_refs = rest[pps:2 * pps]
    o_ref = rest[2 * pps]
    qbd, acc, m_s, l_s = rest[2 * pps + 1:]
    j = pl.program_id(1)
    t = q_ref.shape[0]
    rows = H_A * t

    def blockdiag(x):
        r = lax.broadcasted_iota(I32, (rows, A_W), 0) // t
        c = lax.broadcasted_iota(I32, (rows, A_W), 1) // HD_A
        return jnp.where(r == c, jnp.tile(x, (H_A, 1)), 0.0)

    @pl.when(j == 0)
    def _():
        qbd[...] = blockdiag(q_ref[...] * (HD_A ** -0.5)).astype(BF16)
        m_s[...] = jnp.full(m_s.shape, NEG, F32)
        l_s[...] = jnp.zeros(l_s.shape, F32)
        acc[...] = jnp.zeros(acc.shape, F32)

    def step(lg, pv_fn):
        m_old = m_s[...]
        m_new = jnp.maximum(m_old, jnp.max(lg, axis=1, keepdims=True))
        p = jnp.exp(lg - m_new)
        alpha = jnp.exp(m_old - m_new)
        l_s[...] = alpha * l_s[...] + jnp.sum(p, axis=1, keepdims=True)
        acc[...] = alpha * acc[...] + pv_fn(p.astype(BF16))
        m_s[...] = m_new

    kt = jnp.concatenate([r[...].reshape(A_W, PAGE) for r in k_refs], axis=1).astype(BF16)
    vt = jnp.concatenate([r[...].reshape(A_W, PAGE) for r in v_refs], axis=1).astype(BF16)
    last = j == pl.num_programs(1) - 1
    bias = jnp.concatenate([bias_ref[0]] * (pps - 1) + [jnp.where(last, bias_ref[1], bias_ref[0])], axis=1)
    mbc = jnp.concatenate([mb_ref[u] for u in range(pps)], axis=1)
    lg = jnp.dot(qbd[...], kt, preferred_element_type=F32) + bias + jnp.tile(mbc, (H_A, 1))
    step(lg, lambda p: lax.dot_general(p, vt, (((1,), (1,)), ((), ())), preferred_element_type=F32))

    @pl.when(last)
    def _():
        pad = jnp.zeros((LANES - t, A_W), F32)
        kn = jnp.concatenate([kn_ref[...], pad], axis=0).astype(BF16)
        vn = jnp.concatenate([vn_ref[...], pad], axis=0).astype(BF16)
        lgn = lax.dot_general(qbd[...], kn, (((1,), (1,)), ((), ())), preferred_element_type=F32)
        step(lgn + bias_ref[2] + jnp.tile(mbn_ref[0], (H_A, 1)),
             lambda p: jnp.dot(p, vn, preferred_element_type=F32))
        res = blockdiag_sum(acc[...] / l_s[...], t)
        o_ref[...] = res.astype(o_ref.dtype)


def blockdiag_sum(x, t):
    c = lax.broadcasted_iota(I32, (t, A_W), 1) // HD_A
    out = jnp.zeros((t, A_W), F32)
    for h in range(H_A):
        out = out + jnp.where(c == h, x[h * t:(h + 1) * t, :], 0.0)
    return out


def dsa_sample(proj, layer, cache_k, cache_v, cache_idx_k, page_table, bias_s, pps_sel=16, pps_att=16):
    db, t, _ = proj.shape
    n_pages = page_table.shape[1]
    nc = n_pages + 1
    topk = min(TOPK_MAX, (n_pages * PAGE + t) // 4)
    n_pool = cache_k.shape[1]
    ck = jnp.transpose(cache_k, (0, 1, 3, 4, 2))
    cv = jnp.transpose(cache_v, (0, 1, 3, 4, 2))
    cik = jnp.swapaxes(cache_idx_k, 2, 3)
    pt = page_table.reshape(-1)

    def page_spec(dims, pps, u):
        zeros = (0,) * len(dims)
        return pl.BlockSpec((None, None) + dims,
                            lambda b, j, ptr: (layer, ptr[b * n_pages + j * pps + u]) + zeros)

    sel = pl.pallas_call(
        functools.partial(_dsa_sample_select_kernel, pps_sel, n_pages, topk),
        grid_spec=pltpu.PrefetchScalarGridSpec(
            num_scalar_prefetch=1,
            grid=(db, n_pages // pps_sel),
            in_specs=[pl.BlockSpec((None, t, A_W), lambda b, j, ptr: (b, 0, CB_IQ)),
                      pl.BlockSpec((None, t, LANES), lambda b, j, ptr: (b, 0, CB_IK))]
                     + [page_spec((D_IDX, PAGE), pps_sel, u) for u in range(pps_sel)],
            out_specs=pl.BlockSpec((None, nc, t, LANES), lambda b, j, ptr: (b, 0, 0, 0)),
            scratch_shapes=[pltpu.VMEM((H_IDX * t, D_IDX), BF16), pltpu.VMEM((H_IDX * t, LANES), F32),
                            pltpu.VMEM((nc, t, LANES), I32)]),
        out_shape=jax.ShapeDtypeStruct((db, nc, t, LANES), F32),
        compiler_params=_cparams(("parallel", "arbitrary")),
        name="dsa_sample_select",
    )(pt, proj, proj, *([cik] * pps_sel))

    return pl.pallas_call(
        functools.partial(_dsa_sample_attn_kernel, pps_att, n_pages),
        grid_spec=pltpu.PrefetchScalarGridSpec(
            num_scalar_prefetch=1,
            grid=(db, n_pages // pps_att),
            in_specs=[pl.BlockSpec((None, t, A_W), lambda b, j, ptr: (b, 0, CB_Q)),
                      pl.BlockSpec((None, t, A_W), lambda b, j, ptr: (b, 0, CB_K)),
                      pl.BlockSpec((None, t, A_W), lambda b, j, ptr: (b, 0, CB_V)),
                      pl.BlockSpec((None, pps_att, t, LANES), lambda b, j, ptr: (b, j, 0, 0)),
                      pl.BlockSpec((None, 1, t, LANES), lambda b, j, ptr: (b, n_pages, 0, 0)),
                      pl.BlockSpec(bias_s.shape, lambda b, j, ptr: (0, 0, 0))]
                     + [page_spec((H_A, HD_A, PAGE), pps_att, u) for u in range(pps_att)] * 2,
            out_specs=pl.BlockSpec((None, t, A_W), lambda b, j, ptr: (b, 0, 0)),
            scratch_shapes=[pltpu.VMEM((H_A * t, A_W), BF16), pltpu.VMEM((H_A * t, A_W), F32),
                            pltpu.VMEM((H_A * t, 1), F32), pltpu.VMEM((H_A * t, 1), F32)]),
        out_shape=jax.ShapeDtypeStruct((db, t, A_W), BF16),
        compiler_params=_cparams(("parallel", "arbitrary")),
        name="dsa_sample_attn",
    )(pt, proj, proj, proj, sel, sel, bias_s, *([ck] * pps_att), *([cv] * pps_att))


def _mm(a, b):
    return jnp.dot(a.astype(BF16), b.astype(BF16), preferred_element_type=F32)


def _mm_nt(a, b):
    return lax.dot_general(a.astype(BF16), b.astype(BF16), (((1,), (1,)), ((), ())), preferred_element_type=F32)


def _split3(x):
    hi = x.astype(BF16)
    r1 = x - hi.astype(F32)
    mid = r1.astype(BF16)
    lo = (r1 - mid.astype(F32)).astype(BF16)
    return hi, mid, lo


def _rwkv_kernel(has_state, t_valid, r_ref, k_ref, v_ref, wa_ref, g_ref, sr_ref, sk_ref, sv_ref, swa_ref, sg_ref,
                 mr_ref, mk_ref, mv_ref, mwa_ref, mg_ref, w2_ref, a2_ref, g2_ref, vec_ref, *rest):
    if has_state:
        s0_ref, y_ref, so_ref, pr, pk, pv, pwa, pg, st = rest
    else:
        y_ref, so_ref, pr, pk, pv, pwa, pg, st = rest
    c = pl.program_id(1)
    C = RW_C
    tb = r_ref.shape[0]
    n_pairs = H_B // 2
    zero_blk = jnp.zeros((HD_B, HD_B), F32)

    @pl.when(c == 0)
    def _():
        if has_state:
            for p in range(n_pairs):
                st[p] = jnp.concatenate([jnp.concatenate([s0_ref[2 * p], zero_blk], axis=1),
                                         jnp.concatenate([zero_blk, s0_ref[2 * p + 1]], axis=1)], axis=0)
        else:
            st[...] = jnp.zeros(st.shape, F32)

    def shifted(p_ref, prev_scr, shift_ref, mu_ref):
        p = p_ref[...]
        if tb < C:
            p = jnp.concatenate([p, jnp.zeros((C - tb, p.shape[1]), F32)], axis=0)
        prev = jnp.where(c == 0, shift_ref[...], prev_scr[...])
        rowi = lax.broadcasted_iota(I32, p.shape, 0)
        ps = jnp.where(rowi == 0, prev, pltpu.roll(p, 1, 0))
        prev_scr[...] = p[C - 1:C, :]
        return p + (ps - p) * mu_ref[...]

    xr = shifted(r_ref, pr, sr_ref, mr_ref)
    xk = shifted(k_ref, pk, sk_ref, mk_ref)
    xv = shifted(v_ref, pv, sv_ref, mv_ref)
    xwa = shifted(wa_ref, pwa, swa_ref, mwa_ref)
    xg = shifted(g_ref, pg, sg_ref, mg_ref)

    vec = vec_ref[...]
    w0, a0, kkp, kap, rkp, lnw, lnb = (vec[n:n + 1, :] for n in range(7))
    zw = w0 + _mm(jnp.tanh(xwa[:, 0:W_LORA]), w2_ref[...])
    w_log = -(jnp.maximum(-zw, 0.0) + jnp.log(1.0 + jnp.exp(-jnp.abs(zw)))) - 0.5
    ld = -jnp.exp(w_log)
    za = a0 + _mm(xwa[:, W_LORA:W_LORA + A_LORA], a2_ref[...])
    a = 1.0 / (1.0 + jnp.exp(-za))
    g = _mm(1.0 / (1.0 + jnp.exp(-xg)), g2_ref[...])

    r2 = lax.broadcasted_iota(I32, (2 * C, 2 * C), 0)
    c2 = lax.broadcasted_iota(I32, (2 * C, 2 * C), 1)
    same_head = (r2 // C) == (c2 // C)
    head_ones = same_head.astype(BF16)
    strict = same_head & ((c2 % C) < (r2 % C))
    incl = same_head & ((c2 % C) <= (r2 % C))

    def pairs(x):
        return jnp.concatenate([x[:, p * LANES:(p + 1) * LANES] for p in range(n_pairs)], axis=0)

    def unpairs(x):
        return jnp.concatenate([x[p * C:(p + 1) * C] for p in range(n_pairs)], axis=1)

    def head_sum(x):
        xs = pairs(x)
        hi = xs.astype(BF16)
        mid = (xs - hi.astype(F32)).astype(BF16)
        return unpairs(jnp.dot(hi, head_ones, preferred_element_type=F32)
                       + jnp.dot(mid, head_ones, preferred_element_type=F32))

    kk = xk * kkp
    kk = kk / jnp.maximum(jnp.sqrt(head_sum(kk * kk)), 1e-12)
    k2 = xk * (1.0 + (a - 1.0) * kap)
    bonus_w = head_sum(xr * k2 * rkp)
    if t_valid < C:
        valid = lax.broadcasted_iota(I32, (C, B_W), 0) < t_valid
        ld = jnp.where(valid, ld, 0.0)
        xv = jnp.where(valid, xv, 0.0)
        kk = jnp.where(valid, kk, 0.0)
        k2m = jnp.where(valid, k2, 0.0)
    else:
        k2m = k2
    rc = lax.broadcasted_iota(I32, (C, C), 0)
    cc = lax.broadcasted_iota(I32, (C, C), 1)
    tril = (cc <= rc).astype(BF16)
    hi, mid, lo = _split3(ld)
    lgc = (jnp.dot(tril, hi, preferred_element_type=F32) + jnp.dot(tril, mid, preferred_element_type=F32)
           + jnp.dot(tril, lo, preferred_element_type=F32))
    lg_end = lgc[C - 1:C, :]
    ginv = jnp.exp(-lgc)
    e_end = jnp.exp(lg_end - lgc)
    g_end = jnp.exp(lg_end)
    kb_ = kk * a
    al = kk * jnp.exp(lgc - ld)
    be = kb_ * ginv
    kt = k2m * ginv
    rt = xr * jnp.exp(lgc)
    bp = kb_ * e_end
    kp = k2m * e_end

    lane_lo = lax.broadcasted_iota(I32, (C, LANES), 1) < HD_B

    def stack(x, p):
        xp = x[:, p * LANES:(p + 1) * LANES]
        return jnp.concatenate([jnp.where(lane_lo, xp, 0.0), jnp.where(lane_lo, 0.0, xp)], axis=0)

    P = range(n_pairs)
    bf = lambda x: x.astype(BF16)
    lhs = [bf(jnp.concatenate([stack(al, p), stack(rt, p)], axis=0)) for p in P]
    vs = [bf(stack(xv, p)) for p in P]
    gb = [_mm_nt(lhs[p], stack(be, p)) for p in P]
    gk = [_mm_nt(lhs[p], stack(kt, p)) for p in P]
    p0 = [_mm_nt(lhs[p], st[p]) for p in P]
    qn = [bf(jnp.where(strict, -gb[p][0:2 * C], 0.0)) for p in P]
    u = [p0[p][0:2 * C] + _mm(jnp.where(strict, gk[p][0:2 * C], 0.0), vs[p]) for p in P]
    u = [u[p] + _mm(qn[p], u[p]) for p in P]
    for _ in range(int(math.log2(C)) - 1):
        qn = [bf(_mm(qn[p], qn[p])) for p in P]
        u = [u[p] + _mm(qn[p], u[p]) for p in P]
    m2 = [jnp.concatenate([bf(jnp.where(incl, -gb[p][2 * C:], 0.0)), bf(jnp.where(incl, gk[p][2 * C:], 0.0))], axis=1)
          for p in P]
    uv = [jnp.concatenate([bf(u[p]), vs[p]], axis=0) for p in P]
    yp = [p0[p][2 * C:] + _mm(m2[p], uv[p]) for p in P]
    upd = [lax.dot_general(jnp.concatenate([bf(-u[p]), vs[p]], axis=0),
                           bf(jnp.concatenate([stack(bp, p), stack(kp, p)], axis=0)),
                           (((0,), (0,)), ((), ())), preferred_element_type=F32) for p in P]
    for p in P:
        st[p] = st[p] * g_end[:, p * LANES:(p + 1) * LANES] + upd[p]
    y = jnp.concatenate([yp[p][0:C] + yp[p][C:2 * C] for p in P], axis=1)
    mu = head_sum(y) * (1.0 / HD_B)
    dy = y - mu
    var = head_sum(dy * dy) * (1.0 / HD_B)
    yn = dy * lax.rsqrt(var + GN_EPS) * lnw + lnb
    out = (yn + bonus_w * xv) * g
    y_ref[...] = out[0:tb].astype(y_ref.dtype)

    @pl.when(c == pl.num_programs(1) - 1)
    def _():
        for p in range(n_pairs):
            s = st[p]
            so_ref[2 * p] = s[0:HD_B, 0:HD_B]
            so_ref[2 * p + 1] = s[HD_B:, HD_B:]


def rwkv_mix(proj, shift_full, mu_full, w2, a2, g2p, vecs, s0):
    b, t, _ = proj.shape
    tb = min(t, RW_C)
    nch = t // tb
    has_state = s0 is not None
    cbs = [(B_W, OFF_R // B_W), (B_W, OFF_R // B_W + 1), (B_W, OFF_R // B_W + 2), (LANES, CB_WA), (256, CB_G)]

    in_specs = ([pl.BlockSpec((None, tb, w), functools.partial(lambda cb, bi, c: (bi, c, cb), cb)) for w, cb in cbs]
                + [pl.BlockSpec((None, 1, w), functools.partial(lambda cb, bi, c: (bi, 0, cb), cb)) for w, cb in cbs]
                + [pl.BlockSpec((1, w), functools.partial(lambda cb, bi, c: (0, cb), cb)) for w, cb in cbs]
                + [pl.BlockSpec(w2.shape, lambda bi, c: (0, 0)),
                   pl.BlockSpec(a2.shape, lambda bi, c: (0, 0)),
                   pl.BlockSpec(g2p.shape, lambda bi, c: (0, 0)),
                   pl.BlockSpec(vecs.shape, lambda bi, c: (0, 0))])
    args = [proj] * 5 + [shift_full] * 5 + [mu_full] * 5 + [w2, a2, g2p, vecs]
    if has_state:
        in_specs.append(pl.BlockSpec((None, H_B, HD_B, HD_B), lambda bi, c: (bi, 0, 0, 0)))
        args.append(s0)
    return pl.pallas_call(
        functools.partial(_rwkv_kernel, has_state, tb),
        grid=(b, nch),
        in_specs=in_specs,
        out_specs=[pl.BlockSpec((None, tb, B_W), lambda bi, c: (bi, c, 0)),
                   pl.BlockSpec((None, H_B, HD_B, HD_B), lambda bi, c: (bi, 0, 0, 0))],
        out_shape=[jax.ShapeDtypeStruct((b, t, B_W), BF16),
                   jax.ShapeDtypeStruct((b, H_B, HD_B, HD_B), F32)],
        scratch_shapes=[pltpu.VMEM((1, B_W), F32)] * 3 + [pltpu.VMEM((1, LANES), F32), pltpu.VMEM((1, 256), F32),
                                                          pltpu.VMEM((H_B // 2, 2 * HD_B, 2 * HD_B), F32)],
        compiler_params=_cparams(("parallel", "arbitrary")),
        name="rwkv_mix",
    )(*args)


def _gelu(x):
    return 0.5 * x * (1.0 + jnp.tanh(math.sqrt(2.0 / math.pi) * (x + 0.044715 * (x * x * x))))


def _gmlp_kernel(pu_ref, pv_ref, lnw_ref, lnb_ref, ws_ref, bst_ref, y_ref, gv_ref):
    tc = pu_ref.shape[0]
    u = _gelu(pu_ref[...])
    vf = _gelu(pv_ref[...])
    mu = jnp.mean(vf, axis=1, keepdims=True)
    var = jnp.mean(jnp.square(vf - mu), axis=1, keepdims=True)
    v = (vf - mu) * lax.rsqrt(var + 1e-5) * lnw_ref[...] + lnb_ref[...]
    gv_ref[...] = v
    ri = lax.broadcasted_iota(I32, (CHUNK, CHUNK), 0)
    ci = lax.broadcasted_iota(I32, (CHUNK, CHUNK), 1)
    if tc < CHUNK:
        v = jnp.concatenate([v, jnp.zeros((CHUNK - tc, C_W), F32)], axis=0)
    vb = v.astype(BF16)
    for g in range(G_C):
        sl = slice(g * CD, (g + 1) * CD)
        wm = jnp.where(ci <= ri, ws_ref[g], 0.0).astype(BF16)
        sp = jnp.dot(wm, vb[:, sl], preferred_element_type=F32)[0:tc] + bst_ref[0:tc, g:g + 1]
        y_ref[:, sl] = (u[:, sl] * sp).astype(y_ref.dtype)


def gmlp_mix(proj, lnw, lnb, ws, bst):
    b, t, _ = proj.shape
    tc = min(t, CHUNK)
    return pl.pallas_call(
        _gmlp_kernel,
        grid=(b, t // tc),
        in_specs=[pl.BlockSpec((None, tc, C_W), lambda bi, c: (bi, c, CB_U)),
                  pl.BlockSpec((None, tc, C_W), lambda bi, c: (bi, c, CB_PV)),
                  pl.BlockSpec((1, C_W), lambda bi, c: (0, 0)),
                  pl.BlockSpec((1, C_W), lambda bi, c: (0, 0)),
                  pl.BlockSpec(ws.shape, lambda bi, c: (0, 0, 0)),
                  pl.BlockSpec(bst.shape, lambda bi, c: (0, 0))],
        out_specs=[pl.BlockSpec((None, tc, C_W), lambda bi, c: (bi, c, 0)),
                   pl.BlockSpec((None, tc, C_W), lambda bi, c: (bi, c, 0))],
        out_shape=[jax.ShapeDtypeStruct((b, t, C_W), BF16), jax.ShapeDtypeStruct((b, t, C_W), F32)],
        compiler_params=_cparams(("parallel", "parallel")),
        name="gmlp_mix",
    )(proj, proj, lnw.reshape(1, C_W), lnb.reshape(1, C_W), ws, bst)


def _layer(x, lw, attend, shift_full, s0, tm):
    b, t, d = x.shape
    x2 = x.reshape(b * t, d)
    proj = norm_matmul(x2, lw['norm_mix'], lw['w_in'], min(2 * tm, b * t), 512).reshape(b, t, PROJ_P)
    ya = attend(proj)
    yb, wkv = rwkv_mix(proj, shift_full, lw['mu_full'], lw['w2'], lw['a2'], lw['g2p'], lw['vecs'], s0)
    yc, gv = gmlp_mix(proj, lw['gmlp_ln_w'], lw['gmlp_ln_b'], lw['gmlp_ws'], lw['gmlp_bst'])
    h = out_proj(x2, ya.reshape(b * t, A_W), yb.reshape(b * t, B_W), yc.reshape(b * t, C_W), lw['w_out'], tm)
    y = ffn(h, lw['norm_ffn'], lw['ffn_gate'], lw['ffn_up'], lw['ffn_down'], tm, 512)
    k = proj[:, :, A_W:2 * A_W].reshape(b, t, H_A, HD_A)
    v = proj[:, :, 2 * A_W:3 * A_W].reshape(b, t, H_A, HD_A)
    ik = proj[:, :, OFF_IK:OFF_IK + D_IDX]
    shift_new = _rwkv_cols(proj[:, t - 1])
    return y.reshape(b, t, d), k, v, ik, shift_new, wkv, gv


def kernel(x_prompt, x_sample, cache_k, cache_v, cache_idx_k, state_rwkv_shift, state_rwkv_wkv, page_table, rel_bias, norm_mix, w_in, rwkv_mu, rwkv_w0, rwkv_w2, rwkv_a0, rwkv_a2, rwkv_g2, rwkv_kk, rwkv_ka, rwkv_rk, rwkv_ln_w, rwkv_ln_b, gmlp_ln_w, gmlp_ln_b, gmlp_ws, gmlp_b, w_out, norm_ffn, ffn_gate, ffn_up, ffn_down, norm_final):
    depth = w_in.shape[0]
    nbp, seq, d = x_prompt.shape
    db, dec_seq, _ = x_sample.shape
    assert d == D_MODEL and seq % 512 == 0 and dec_seq == SUBLANES and w_in.shape[2] == ORIG_GM + 2 * C_W

    w_in_p = _pad_cols(w_in).astype(BF16)
    w_out_b = w_out.astype(BF16)
    wg_b, wu_b, wd_b = ffn_gate.astype(BF16), ffn_up.astype(BF16), ffn_down.astype(BF16)
    mu_full = _pad_rwkv_cols(rwkv_mu)
    g2p = jnp.concatenate([rwkv_g2, jnp.zeros((depth, 256 - G_LORA, B_W), F32)], axis=1)
    vecs = jnp.stack([rwkv_w0, rwkv_a0, rwkv_kk, rwkv_ka, rwkv_rk, rwkv_ln_w, rwkv_ln_b,
                      jnp.zeros_like(rwkv_w0)], axis=1)
    bst = jnp.swapaxes(gmlp_b, 1, 2)
    shift_s = _pad_rwkv_cols(state_rwkv_shift)[:, :, None, :]
    shift_p = jnp.zeros((nbp, 1, PROJ_P), F32)
    bias_p, bias_s = bias_tiles(rel_bias, dec_seq)

    xp, xs = x_prompt, x_sample
    outs = [[] for _ in range(11)]
    for l in range(depth):
        lw = {'norm_mix': norm_mix[l], 'w_in': w_in_p[l], 'mu_full': mu_full[l:l + 1], 'w2': rwkv_w2[l],
              'a2': rwkv_a2[l], 'g2p': g2p[l], 'vecs': vecs[l], 'gmlp_ln_w': gmlp_ln_w[l],
              'gmlp_ln_b': gmlp_ln_b[l], 'gmlp_ws': gmlp_ws[l], 'gmlp_bst': bst[l], 'w_out': w_out_b[l],
              'norm_ffn': norm_ffn[l], 'ffn_gate': wg_b[l], 'ffn_up': wu_b[l], 'ffn_down': wd_b[l]}
        attend_p = functools.partial(dsa_prompt, bias_p=bias_p, rel_bias=rel_bias)
        xp, kp, vp, ikp, shp, wkvp, _ = _layer(xp, lw, attend_p, shift_p, None, 512)
        attend_s = functools.partial(dsa_sample, layer=l, cache_k=cache_k, cache_v=cache_v,
                                     cache_idx_k=cache_idx_k, page_table=page_table, bias_s=bias_s)
        xs, k_s, v_s, ik_s, sh_s, wkv_s, gv_s = _layer(xs, lw, attend_s, shift_s[l], state_rwkv_wkv[l],
                                                       db * dec_seq)
        for lst, val in zip(outs, (kp, vp, ikp, shp, wkvp, k_s, v_s, ik_s, sh_s, wkv_s, gv_s)):
            lst.append(val)
    y_prompt = rms_final(xp.reshape(nbp * seq, d), norm_final, 512).reshape(nbp, seq, d)
    y_sample = rms_final(xs.reshape(db * dec_seq, d), norm_final, db * dec_seq).reshape(db, dec_seq, d)
    return (y_prompt, y_sample) + tuple(jnp.stack(o) for o in outs)
```

```python
import functools
import math

import numpy as np
import jax
import jax.numpy as jnp
from jax import lax
from jax.experimental import pallas as pl
from jax.experimental.pallas import tpu as pltpu

F32 = jnp.float32
BF16 = jnp.bfloat16
I32 = jnp.int32

LANES = 128
SUBLANES = 8
VMEM_LIMIT = 56 * 1024 * 1024

D_MODEL = 2048
HD_A = 64
H_A = 8
A_W = H_A * HD_A
H_IDX = 8
D_IDX = 64
IDX_SCALE = (D_IDX ** -0.5) * (H_IDX ** -0.5)
TOPK_MAX = 256
N_BUCKETS = 32
MAX_DISTANCE = 128
HD_B = 64
B_W = 1024
H_B = B_W // HD_B
W_LORA = 64
A_LORA = 64
G_LORA = 160
RWKV_PROJ_W = 3 * B_W + W_LORA + A_LORA + G_LORA
GN_EPS = 64e-5
C_W = 512
CHUNK = 128
CD = 64
G_C = C_W // CD
D_FF = 5632
NORM_EPS = 1e-6
NEG = -1e30
PAGE = 128
INT_MIN = -(2 ** 31)

ORIG_IK = 3 * A_W + H_IDX * D_IDX
ORIG_R = ORIG_IK + D_IDX + H_IDX
ORIG_GM = ORIG_R + RWKV_PROJ_W
OFF_R = 2048
OFF_IK = OFF_R + 3 * B_W
OFF_WA = OFF_IK + LANES
OFF_G = OFF_WA + W_LORA + A_LORA
OFF_GM = OFF_G + 256
PROJ_P = OFF_GM + 2 * C_W
CB_Q, CB_K, CB_V, CB_IQ = 0, 1, 2, 3
CB_IK = OFF_IK // LANES
CB_WA = OFF_WA // LANES
CB_G = OFF_G // 256
CB_U = OFF_GM // C_W
CB_PV = CB_U + 1
RW_C = 64
RW_SEQS_PER_STEP = 2
assert OFF_R % B_W == 0 and OFF_G % 256 == 0 and OFF_GM % C_W == 0


def _cparams(sem):
    return pltpu.CompilerParams(dimension_semantics=sem, vmem_limit_bytes=VMEM_LIMIT)


def _pad_cols(a):
    z = lambda n: jnp.zeros(a.shape[:-1] + (n,), a.dtype)
    lora = ORIG_R + 3 * B_W
    return jnp.concatenate([a[..., :ORIG_IK], a[..., ORIG_R:lora], a[..., ORIG_IK:ORIG_R], z(OFF_WA - OFF_IK - D_IDX - H_IDX),
                            a[..., lora:ORIG_GM], z(OFF_GM - OFF_G - G_LORA), a[..., ORIG_GM:]], axis=-1)


def _pad_rwkv_cols(a):
    z = lambda n: jnp.zeros(a.shape[:-1] + (n,), a.dtype)
    return jnp.concatenate([z(OFF_R), a[..., :3 * B_W], z(LANES), a[..., 3 * B_W:], z(PROJ_P - OFF_G - G_LORA)], axis=-1)


def _rwkv_cols(p):
    return jnp.concatenate([p[..., OFF_R:OFF_R + 3 * B_W], p[..., OFF_WA:OFF_WA + RWKV_PROJ_W - 3 * B_W]], axis=-1)


def _norm_matmul_kernel(x_ref, g_ref, w_ref, o_ref, xn_ref):
    @pl.when(pl.program_id(1) == 0)
    def _():
        x = x_ref[...]
        y = x * lax.rsqrt(jnp.mean(x * x, axis=-1, keepdims=True) + NORM_EPS)
        xn_ref[...] = (y * g_ref[...]).astype(BF16)

    o_ref[...] = jnp.dot(xn_ref[...], w_ref[...], preferred_element_type=F32)


def norm_matmul(x, g, w, tm, tn):
    m, d = x.shape
    n = w.shape[1]
    return pl.pallas_call(
        _norm_matmul_kernel,
        grid=(m // tm, n // tn),
        in_specs=[pl.BlockSpec((tm, d), lambda i, j: (i, 0)),
                  pl.BlockSpec((1, d), lambda i, j: (0, 0)),
                  pl.BlockSpec((d, tn), lambda i, j: (0, j))],
        out_specs=pl.BlockSpec((tm, tn), lambda i, j: (i, j)),
        out_shape=jax.ShapeDtypeStruct((m, n), F32),
        scratch_shapes=[pltpu.VMEM((tm, d), BF16)],
        compiler_params=_cparams(("parallel", "arbitrary")),
        name="norm_matmul",
    )(x, g.reshape(1, d), w)


def _out_proj_kernel(x_ref, ya_ref, yb_ref, yc_ref, w_ref, o_ref):
    acc = jnp.dot(ya_ref[...], w_ref[0:A_W, :], preferred_element_type=F32)
    acc += jnp.dot(yb_ref[...], w_ref[A_W:A_W + B_W, :], preferred_element_type=F32)
    acc += jnp.dot(yc_ref[...], w_ref[A_W + B_W:, :], preferred_element_type=F32)
    o_ref[...] = x_ref[...] + acc


def out_proj(x, ya, yb, yc, w, tm):
    m, d = x.shape
    return pl.pallas_call(
        _out_proj_kernel,
        grid=(m // tm,),
        in_specs=[pl.BlockSpec((tm, d), lambda i: (i, 0)),
                  pl.BlockSpec((tm, A_W), lambda i: (i, 0)),
                  pl.BlockSpec((tm, B_W), lambda i: (i, 0)),
                  pl.BlockSpec((tm, C_W), lambda i: (i, 0)),
                  pl.BlockSpec(w.shape, lambda i: (0, 0))],
        out_specs=pl.BlockSpec((tm, d), lambda i: (i, 0)),
        out_shape=jax.ShapeDtypeStruct((m, d), F32),
        compiler_params=_cparams(("parallel",)),
        name="out_proj",
    )(x, ya, yb, yc, w)


def _ffn_kernel(h_ref, g_ref, wg_ref, wu_ref, wd_ref, o_ref, hn_ref, acc_ref):
    f = pl.program_id(1)

    @pl.when(f == 0)
    def _():
        x = h_ref[...]
        y = x * lax.rsqrt(jnp.mean(x * x, axis=-1, keepdims=True) + NORM_EPS)
        hn_ref[...] = (y * g_ref[...]).astype(BF16)
        acc_ref[...] = jnp.zeros_like(acc_ref)

    hn = hn_ref[...]
    gate = jnp.dot(hn, wg_ref[...], preferred_element_type=F32)
    up = jnp.dot(hn, wu_ref[...], preferred_element_type=F32)
    act = (gate / (1.0 + jnp.exp(-gate))) * up
    acc_ref[...] += jnp.dot(act.astype(BF16), wd_ref[...], preferred_element_type=F32)

    @pl.when(f == pl.num_programs(1) - 1)
    def _():
        o_ref[...] = h_ref[...] + acc_ref[...]


def ffn(h, g, wg, wu, wd, tm, tf):
    m, d = h.shape
    nf = wg.shape[1]
    return pl.pallas_call(
        _ffn_kernel,
        grid=(m // tm, nf // tf),
        in_specs=[pl.BlockSpec((tm, d), lambda i, f: (i, 0)),
                  pl.BlockSpec((1, d), lambda i, f: (0, 0)),
                  pl.BlockSpec((d, tf), lambda i, f: (0, f)),
                  pl.BlockSpec((d, tf), lambda i, f: (0, f)),
                  pl.BlockSpec((tf, d), lambda i, f: (f, 0))],
        out_specs=pl.BlockSpec((tm, d), lambda i, f: (i, 0)),
        out_shape=jax.ShapeDtypeStruct((m, d), F32),
        scratch_shapes=[pltpu.VMEM((tm, d), BF16), pltpu.VMEM((tm, d), F32)],
        compiler_params=_cparams(("parallel", "arbitrary")),
        name="ffn",
    )(h, g.reshape(1, d), wg, wu, wd)


def _rms_kernel(x_ref, g_ref, o_ref):
    x = x_ref[...]
    o_ref[...] = x * lax.rsqrt(jnp.mean(x * x, axis=-1, keepdims=True) + NORM_EPS) * g_ref[...]


def rms_final(x, g, tm):
    m, d = x.shape
    return pl.pallas_call(
        _rms_kernel,
        grid=(m // tm,),
        in_specs=[pl.BlockSpec((tm, d), lambda i: (i, 0)), pl.BlockSpec((1, d), lambda i: (0, 0))],
        out_specs=pl.BlockSpec((tm, d), lambda i: (i, 0)),
        out_shape=jax.ShapeDtypeStruct((m, d), F32),
        compiler_params=_cparams(("parallel",)),
        name="rms_final",
    )(x, g.reshape(1, d))


def _bucket_np(dist):
    max_exact = N_BUCKETS // 2
    n = np.maximum(dist, 0)
    nf = np.maximum(n, 1).astype(np.float64)
    large = max_exact + (np.log(nf / max_exact) / math.log(MAX_DISTANCE / max_exact)
                         * (N_BUCKETS - max_exact)).astype(np.int32)
    return np.where(n < max_exact, n, np.minimum(large, N_BUCKETS - 1)).astype(np.int32)


def _bias_tiles_kernel(rb_ref, bp_ref, bs_ref, op_ref, os_ref):
    for d in range(2):
        bk = bp_ref[d]
        for h in range(H_A):
            t = jnp.zeros(bk.shape, F32)
            for b in range(N_BUCKETS):
                t = jnp.where(bk == b, rb_ref[b, h], t)
            op_ref[d, h] = t
    for d in range(3):
        bk = bs_ref[d]
        for h in range(H_A):
            t = jnp.zeros(bk.shape, F32)
            for b in range(N_BUCKETS):
                t = jnp.where(bk == b, rb_ref[b, h], t)
            os_ref[d, h * SUBLANES:(h + 1) * SUBLANES, :] = t


def bias_tiles(rel_bias, dec_seq):
    r = np.arange(LANES)[:, None]
    c = np.arange(LANES)[None, :]
    bp = np.stack([_bucket_np(r - c), _bucket_np(LANES + r - c)])
    t = np.arange(dec_seq)[:, None]
    bs = np.stack([_bucket_np(np.full((dec_seq, LANES), 4 * MAX_DISTANCE)),
                   _bucket_np(PAGE + t - c), _bucket_np(t - c)])
    vm = pl.BlockSpec(memory_space=pltpu.VMEM)
    return pl.pallas_call(
        _bias_tiles_kernel,
        in_specs=[pl.BlockSpec(memory_space=pltpu.SMEM), vm, vm],
        out_specs=[vm, vm],
        out_shape=[jax.ShapeDtypeStruct((2, H_A, LANES, LANES), F32),
                   jax.ShapeDtypeStruct((3, H_A * dec_seq, LANES), F32)],
        name="bias_tiles",
    )(rel_bias, jnp.asarray(bp), jnp.asarray(bs))


def _score_key(s):
    s = jnp.where(s == 0.0, 0.0, s)
    bits = lax.bitcast_convert_type(s, I32)
    return jnp.where(bits < 0, bits ^ 0x7FFFFFFF, bits)


class _Pairs:
    def __init__(self, n_pairs):
        self.n_pairs = n_pairs


def _chunk_loop(n_chunks, body, init):
    if isinstance(n_chunks, _Pairs):
        return lax.fori_loop(0, n_chunks.n_pairs, lambda m, c: body(2 * m + 1, body(2 * m, c)), init)
    return lax.fori_loop(0, n_chunks, body, init)


def _kth_largest(sc_ref, n_chunks, kf, shape):
    def count_ge(cand):
        if isinstance(n_chunks, int):
            a = jnp.sum((sc_ref[...] >= cand[None]).astype(F32), axis=0)
        else:
            parts = []
            for r0 in range(0, shape[0], LANES):
                cs = cand[r0:r0 + LANES]

                def body(kc, a, r0=r0, cs=cs):
                    return a + (sc_ref[kc, r0:r0 + LANES, :] >= cs).astype(F32)

                parts.append(_chunk_loop(n_chunks, body, jnp.zeros((LANES, LANES), F32)))
            a = parts[0] if len(parts) == 1 else jnp.concatenate(parts, axis=0)
        return jnp.sum(a, axis=1, keepdims=True)

    zero = jnp.zeros(shape, I32)
    t0 = jnp.where(count_ge(zero) >= kf, zero, jnp.full(shape, INT_MIN, I32))

    def bit_body(it, t):
        cand = t | jnp.left_shift(jnp.int32(1), 30 - it)
        return jnp.where(count_ge(cand) >= kf, cand, t)

    return lax.fori_loop(0, 31, bit_body, t0)


def _select_mask(sc_ref, mb_ref, n_chunks, kf, thr, shape, causal_fn):
    def cnt_body(kc, carry):
        key = sc_ref[kc]
        return carry[0] + (key > thr).astype(F32), carry[1] + (key >= thr).astype(F32)

    zeros = jnp.zeros(shape, F32)
    if isinstance(n_chunks, int):
        keys = sc_ref[...]
        cgt = jnp.sum((keys > thr[None]).astype(F32), axis=0)
        cge = jnp.sum((keys >= thr[None]).astype(F32), axis=0)
    else:
        cgt, cge = _chunk_loop(n_chunks, cnt_body, (zeros, zeros))
    need = kf - jnp.sum(cgt, axis=1, keepdims=True)
    cge = jnp.sum(cge, axis=1, keepdims=True)

    def exact_k():
        def body(kc, carry):
            mb_ref[kc] = jnp.where((sc_ref[kc] >= thr) & causal_fn(kc), 0.0, NEG)
            return carry

        if isinstance(n_chunks, int):
            lax.fori_loop(0, n_chunks, body, 0, unroll=8)
        else:
            _chunk_loop(n_chunks, body, 0)

    def with_ties():
        rr = lax.broadcasted_iota(I32, (LANES, LANES), 0)
        cc = lax.broadcasted_iota(I32, (LANES, LANES), 1)
        tri = (rr <= cc).astype(BF16)

        def body(kc, offs):
            key = sc_ref[kc]
            eq = key == thr
            pre = jnp.dot(eq.astype(BF16), tri, preferred_element_type=F32) + offs
            sel = (key > thr) | (eq & (pre <= need))
            mb_ref[kc] = jnp.where(sel & causal_fn(kc), 0.0, NEG)
            return jnp.broadcast_to(pre[:, LANES - 1:LANES], shape)

        _chunk_loop(n_chunks, body, zeros)

    lax.cond(jnp.max(jnp.abs(cge - kf)) > 0.5, with_ties, exact_k)


def _dsa_prompt_kernel(topk, q_ref, iq_ref, iw_ref, k_ref, v_ref, ik_ref, bias_ref, rb_ref, o_ref,
                       kb, vb, ikd, qm, iqm, iwb, sc, mb, lgs, acc, m_s, l_s):
    i = pl.program_id(1)
    tq = q_ref.shape[0]
    shape = (tq, LANES)
    nt = (((1,), (1,)), ((), ()))

    @pl.when(i == 0)
    def _():
        kb[...] = k_ref[...].astype(BF16)
        vb[...] = v_ref[...].astype(BF16)
        ik = ik_ref[:, 0:D_IDX].astype(BF16)
        ikd[...] = jnp.concatenate([ik, ik], axis=1)

    row = lax.broadcasted_iota(I32, shape, 0) + i * tq
    col = lax.broadcasted_iota(I32, shape, 1)
    lane_lo = col < HD_A
    sub = tq // LANES
    n_chunks = _Pairs((i + 1) * (sub // 2)) if sub % 2 == 0 else (i + 1) * sub
    iw = iw_ref[:, D_IDX:D_IDX + H_IDX]
    for h in range(H_A):
        ps = slice((h // 2) * LANES, (h // 2 + 1) * LANES)
        keep = lane_lo if h % 2 == 0 else jnp.logical_not(lane_lo)
        qm[h] = jnp.where(keep, q_ref[:, ps] * (HD_A ** -0.5), 0.0).astype(BF16)
        iqm[h] = jnp.where(keep, iq_ref[:, ps], 0.0).astype(BF16)
        iwb[h] = jnp.broadcast_to(iw[:, h:h + 1], shape)

    def causal(kc):
        return (col + kc * LANES) <= row

    def score_body(kc, carry):
        ikc = ikd[pl.ds(pl.multiple_of(kc * LANES, LANES), LANES), :]
        ss = [lax.dot_general(iqm[h], ikc, nt, preferred_element_type=F32) for h in range(H_IDX)]
        ws = [iwb[h] * jnp.maximum(ss[h], 0.0) for h in range(H_IDX)]
        tot = ((ws[0] + ws[1]) + (ws[2] + ws[3])) + ((ws[4] + ws[5]) + (ws[6] + ws[7]))
        tot = jnp.where(causal(kc), tot * IDX_SCALE, NEG)
        sc[kc] = _score_key(tot)
        return carry

    _chunk_loop(n_chunks, score_body, 0)
    kf = float(topk)
    thr = _kth_largest(sc, n_chunks, kf, shape)
    _select_mask(sc, mb, n_chunks, kf, thr, shape, causal)

    m_s[...] = jnp.full(m_s.shape, NEG, F32)

    def logit_body(kc, carry):
        off = pl.multiple_of(kc * LANES, LANES)
        mbc = mb[kc]
        for h in range(H_A):
            ps = slice((h // 2) * LANES, (h // 2 + 1) * LANES)
            lg = lax.dot_general(qm[h], kb[pl.ds(off, LANES), ps], nt, preferred_element_type=F32)
            tiles = []
            for s in range(sub):
                d = i * sub + s - kc
                tiles.append(jnp.where(d == 0, bias_ref[0, h],
                                       jnp.where(d == 1, bias_ref[1, h], rb_ref[N_BUCKETS - 1, h])))
            bias = tiles[0] if sub == 1 else jnp.concatenate(tiles, axis=0)
            lg = lg + bias + mbc
            lgs[h, kc] = lg
            m_s[h] = jnp.maximum(m_s[h], lg)
        return carry

    _chunk_loop(n_chunks, logit_body, 0)
    for h in range(H_A):
        m_s[h] = jnp.broadcast_to(jnp.max(m_s[h], axis=1, keepdims=True), shape)
    l_s[...] = jnp.zeros(l_s.shape, F32)
    acc[...] = jnp.zeros(acc.shape, F32)

    def attn_body(kc, carry):
        off = pl.multiple_of(kc * LANES, LANES)
        for h in range(H_A):
            ps = slice((h // 2) * LANES, (h // 2 + 1) * LANES)
            p = jnp.exp(lgs[h, kc] - m_s[h])
            l_s[h] += p
            acc[h] += jnp.dot(p.astype(BF16), vb[pl.ds(off, LANES), ps], preferred_element_type=F32)
        return carry

    _chunk_loop(n_chunks, attn_body, 0)
    for j in range(H_A // 2):
        lo = acc[2 * j] / jnp.sum(l_s[2 * j], axis=1, keepdims=True)
        hi = acc[2 * j + 1] / jnp.sum(l_s[2 * j + 1], axis=1, keepdims=True)
        o_ref[:, j * LANES:(j + 1) * LANES] = jnp.where(lane_lo, lo, hi).astype(o_ref.dtype)


def dsa_prompt(proj, bias_p, rel_bias, tq=2 * LANES):
    b, s, _ = proj.shape
    once = pl.Buffered(1)
    topk = min(TOPK_MAX, s // 4)
    nc = s // LANES
    kern = functools.partial(_dsa_prompt_kernel, topk)
    return pl.pallas_call(
        kern,
        grid=(b, s // tq),
        in_specs=[pl.BlockSpec((None, tq, A_W), lambda bi, i: (bi, i, CB_Q)),
                  pl.BlockSpec((None, tq, A_W), lambda bi, i: (bi, i, CB_IQ)),
                  pl.BlockSpec((None, tq, LANES), lambda bi, i: (bi, i, CB_IK)),
                  pl.BlockSpec((None, s, A_W), lambda bi, i: (bi, 0, CB_K), pipeline_mode=once),
                  pl.BlockSpec((None, s, A_W), lambda bi, i: (bi, 0, CB_V), pipeline_mode=once),
                  pl.BlockSpec((None, s, LANES), lambda bi, i: (bi, 0, CB_IK), pipeline_mode=once),
                  pl.BlockSpec(bias_p.shape, lambda bi, i: (0, 0, 0, 0), pipeline_mode=once),
                  pl.BlockSpec(memory_space=pltpu.SMEM)],
        out_specs=pl.BlockSpec((None, tq, A_W), lambda bi, i: (bi, i, 0)),
        out_shape=jax.ShapeDtypeStruct((b, s, A_W), BF16),
        scratch_shapes=[pltpu.VMEM((s, A_W), BF16), pltpu.VMEM((s, A_W), BF16), pltpu.VMEM((s, LANES), BF16),
                        pltpu.VMEM((H_A, tq, LANES), BF16), pltpu.VMEM((H_IDX, tq, LANES), BF16),
                        pltpu.VMEM((H_IDX, tq, LANES), F32),
                        pltpu.VMEM((nc, tq, LANES), I32), pltpu.VMEM((nc, tq, LANES), F32),
                        pltpu.VMEM((H_A, nc, tq, LANES), F32),
                        pltpu.VMEM((H_A, tq, LANES), F32), pltpu.VMEM((H_A, tq, LANES), F32),
                        pltpu.VMEM((H_A, tq, LANES), F32)],
        compiler_params=_cparams(("parallel", "arbitrary")),
        name="dsa_prompt",
    )(proj, proj, proj, proj, proj, proj, bias_p, rel_bias)


def _dsa_sample_select_kernel(pps, n_pages, topk, pt_ref, iq_ref, ikiw_ref, *rest):
    page_refs = rest[:pps]
    mb_ref = rest[pps]
    iq2, wb, sc = rest[pps + 1:]
    j = pl.program_id(1)
    t = iq_ref.shape[0]
    shape = (t, LANES)
    nc = n_pages + 1

    @pl.when(j == 0)
    def _():
        iq = iq_ref[...]
        ikiw = ikiw_ref[...]
        for h in range(H_IDX):
            iq2[h * t:(h + 1) * t, :] = iq[:, h * D_IDX:(h + 1) * D_IDX].astype(BF16)
            wb[h * t:(h + 1) * t, :] = jnp.broadcast_to(ikiw[:, D_IDX + h:D_IDX + h + 1], shape)

    def scores(s):
        n = s.shape[1]
        s = jnp.maximum(s, 0.0) * jnp.tile(wb[...], (1, n // LANES))
        return jnp.sum(s.reshape(H_IDX, t, n), axis=0) * IDX_SCALE

    ikt = jnp.concatenate([r[...] for r in page_refs], axis=1).astype(BF16)
    keys = _score_key(scores(jnp.dot(iq2[...], ikt, preferred_element_type=F32)))
    for u in range(pps):
        sc[j * pps + u] = keys[:, u * LANES:(u + 1) * LANES]

    @pl.when(j == pl.num_programs(1) - 1)
    def _():
        row = lax.broadcasted_iota(I32, shape, 0)
        col = lax.broadcasted_iota(I32, shape, 1)
        ik_new = jnp.concatenate([ikiw_ref[:, 0:D_IDX], jnp.zeros((LANES - t, D_IDX), F32)], axis=0)
        s_new = lax.dot_general(iq2[...], ik_new.astype(BF16), (((1,), (1,)), ((), ())), preferred_element_type=F32)
        s_new = jnp.where(col <= row, scores(s_new), NEG)
        sc[n_pages] = jnp.where(col < t, _score_key(s_new), INT_MIN)
        kf = float(topk)
        thr = _kth_largest(sc, nc, kf, shape)

        def causal(kc):
            return (kc < n_pages) | (col <= row)

        _select_mask(sc, mb_ref, nc, kf, thr, shape, causal)


def _dsa_sample_attn_kernel(pps, n_pages, pt_ref, q_ref, kn_ref, vn_ref, mb_ref, mbn_ref, bias_ref, *rest):
    k_refs = rest[:pps]
    v_refs = rest[pps:2 * pps]
    o_ref = rest[2 * pps]
    qbd, acc, m_s, l_s = rest[2 * pps + 1:]
    j = pl.program_id(1)
    t = q_ref.shape[0]
    rows = H_A * t

    def blockdiag(x):
        r = lax.broadcasted_iota(I32, (rows, A_W), 0) // t
        c = lax.broadcasted_iota(I32, (rows, A_W), 1) // HD_A
        return jnp.where(r == c, jnp.tile(x, (H_A, 1)), 0.0)

    @pl.when(j == 0)
    def _():
        qbd[...] = blockdiag(q_ref[...] * (HD_A ** -0.5)).astype(BF16)
        m_s[...] = jnp.full(m_s.shape, NEG, F32)
        l_s[...] = jnp.zeros(l_s.shape, F32)
        acc[...] = jnp.zeros(acc.shape, F32)

    def step(lg, pv_fn):
        m_old = m_s[...]
        m_new = jnp.maximum(m_old, jnp.max(lg, axis=1, keepdims=True))
        p = jnp.exp(lg - m_new)
        alpha = jnp.exp(m_old - m_new)
        l_s[...] = alpha * l_s[...] + jnp.sum(p, axis=1, keepdims=True)
        acc[...] = alpha * acc[...] + pv_fn(p.astype(BF16))
        m_s[...] = m_new

    kt = jnp.concatenate([r[...].reshape(A_W, PAGE) for r in k_refs], axis=1).astype(BF16)
    vt = jnp.concatenate([r[...].reshape(A_W, PAGE) for r in v_refs], axis=1).astype(BF16)
    last = j == pl.num_programs(1) - 1
    bias = jnp.concatenate([bias_ref[0]] * (pps - 1) + [jnp.where(last, bias_ref[1], bias_ref[0])], axis=1)
    mbc = jnp.concatenate([mb_ref[u] for u in range(pps)], axis=1)
    lg = jnp.dot(qbd[...], kt, preferred_element_type=F32) + bias + jnp.tile(mbc, (H_A, 1))
    step(lg, lambda p: lax.dot_general(p, vt, (((1,), (1,)), ((), ())), preferred_element_type=F32))

    @pl.when(last)
    def _():
        pad = jnp.zeros((LANES - t, A_W), F32)
        kn = jnp.concatenate([kn_ref[...], pad], axis=0).astype(BF16)
        vn = jnp.concatenate([vn_ref[...], pad], axis=0).astype(BF16)
        lgn = lax.dot_general(qbd[...], kn, (((1,), (1,)), ((), ())), preferred_element_type=F32)
        step(lgn + bias_ref[2] + jnp.tile(mbn_ref[0], (H_A, 1)),
             lambda p: jnp.dot(p, vn, preferred_element_type=F32))
        res = blockdiag_sum(acc[...] / l_s[...], t)
        o_ref[...] = res.astype(o_ref.dtype)


def blockdiag_sum(x, t):
    c = lax.broadcasted_iota(I32, (t, A_W), 1) // HD_A
    out = jnp.zeros((t, A_W), F32)
    for h in range(H_A):
        out = out + jnp.where(c == h, x[h * t:(h + 1) * t, :], 0.0)
    return out


def dsa_sample(proj, layer, cache_k, cache_v, cache_idx_k, page_table, bias_s, pps_sel=16, pps_att=16):
    db, t, _ = proj.shape
    n_pages = page_table.shape[1]
    nc = n_pages + 1
    topk = min(TOPK_MAX, (n_pages * PAGE + t) // 4)
    n_pool = cache_k.shape[1]
    ck = jnp.transpose(cache_k, (0, 1, 3, 4, 2))
    cv = jnp.transpose(cache_v, (0, 1, 3, 4, 2))
    cik = jnp.swapaxes(cache_idx_k, 2, 3)
    pt = page_table.reshape(-1)

    def page_spec(dims, pps, u):
        zeros = (0,) * len(dims)
        return pl.BlockSpec((None, None) + dims,
                            lambda b, j, ptr: (layer, ptr[b * n_pages + j * pps + u]) + zeros)

    sel = pl.pallas_call(
        functools.partial(_dsa_sample_select_kernel, pps_sel, n_pages, topk),
        grid_spec=pltpu.PrefetchScalarGridSpec(
            num_scalar_prefetch=1,
            grid=(db, n_pages // pps_sel),
            in_specs=[pl.BlockSpec((None, t, A_W), lambda b, j, ptr: (b, 0, CB_IQ)),
                      pl.BlockSpec((None, t, LANES), lambda b, j, ptr: (b, 0, CB_IK))]
                     + [page_spec((D_IDX, PAGE), pps_sel, u) for u in range(pps_sel)],
            out_specs=pl.BlockSpec((None, nc, t, LANES), lambda b, j, ptr: (b, 0, 0, 0)),
            scratch_shapes=[pltpu.VMEM((H_IDX * t, D_IDX), BF16), pltpu.VMEM((H_IDX * t, LANES), F32),
                            pltpu.VMEM((nc, t, LANES), I32)]),
        out_shape=jax.ShapeDtypeStruct((db, nc, t, LANES), F32),
        compiler_params=_cparams(("parallel", "arbitrary")),
        name="dsa_sample_select",
    )(pt, proj, proj, *([cik] * pps_sel))

    return pl.pallas_call(
        functools.partial(_dsa_sample_attn_kernel, pps_att, n_pages),
        grid_spec=pltpu.PrefetchScalarGridSpec(
            num_scalar_prefetch=1,
            grid=(db, n_pages // pps_att),
            in_specs=[pl.BlockSpec((None, t, A_W), lambda b, j, ptr: (b, 0, CB_Q)),
                      pl.BlockSpec((None, t, A_W), lambda b, j, ptr: (b, 0, CB_K)),
                      pl.BlockSpec((None, t, A_W), lambda b, j, ptr: (b, 0, CB_V)),
                      pl.BlockSpec((None, pps_att, t, LANES), lambda b, j, ptr: (b, j, 0, 0)),
                      pl.BlockSpec((None, 1, t, LANES), lambda b, j, ptr: (b, n_pages, 0, 0)),
                      pl.BlockSpec(bias_s.shape, lambda b, j, ptr: (0, 0, 0))]
                     + [page_spec((H_A, HD_A, PAGE), pps_att, u) for u in range(pps_att)] * 2,
            out_specs=pl.BlockSpec((None, t, A_W), lambda b, j, ptr: (b, 0, 0)),
            scratch_shapes=[pltpu.VMEM((H_A * t, A_W), BF16), pltpu.VMEM((H_A * t, A_W), F32),
                            pltpu.VMEM((H_A * t, 1), F32), pltpu.VMEM((H_A * t, 1), F32)]),
        out_shape=jax.ShapeDtypeStruct((db, t, A_W), BF16),
        compiler_params=_cparams(("parallel", "arbitrary")),
        name="dsa_sample_attn",
    )(pt, proj, proj, proj, sel, sel, bias_s, *([ck] * pps_att), *([cv] * pps_att))


def _mm(a, b):
    return jnp.dot(a.astype(BF16), b.astype(BF16), preferred_element_type=F32)


def _mm_nt(a, b):
    return lax.dot_general(a.astype(BF16), b.astype(BF16), (((1,), (1,)), ((), ())), preferred_element_type=F32)


def _split3(x):
    hi = x.astype(BF16)
    r1 = x - hi.astype(F32)
    mid = r1.astype(BF16)
    lo = (r1 - mid.astype(F32)).astype(BF16)
    return hi, mid, lo


def _rwkv_kernel(has_state, t_valid, r_ref, k_ref, v_ref, wa_ref, g_ref, sr_ref, sk_ref, sv_ref, swa_ref, sg_ref,
                 mr_ref, mk_ref, mv_ref, mwa_ref, mg_ref, w2_ref, a2_ref, g2_ref, vec_ref, *rest):
    if has_state:
        s0_ref, y_ref, so_ref, pr, pk, pv, pwa, pg, st = rest
    else:
        y_ref, so_ref, pr, pk, pv, pwa, pg, st = rest
    c = pl.program_id(1)
    C = RW_C
    nb, tb = r_ref.shape[0], r_ref.shape[1]
    n_pairs = H_B // 2
    zero_blk = jnp.zeros((HD_B, HD_B), F32)

    @pl.when(c == 0)
    def _():
        if has_state:
            for bi in range(nb):
                for p in range(n_pairs):
                    st[bi * n_pairs + p] = jnp.concatenate(
                        [jnp.concatenate([s0_ref[bi, 2 * p], zero_blk], axis=1),
                         jnp.concatenate([zero_blk, s0_ref[bi, 2 * p + 1]], axis=1)], axis=0)
        else:
            st[...] = jnp.zeros(st.shape, F32)

    def shifted(bi, p_ref, prev_scr, shift_ref, mu_ref):
        p = p_ref[bi]
        if tb < C:
            p = jnp.concatenate([p, jnp.zeros((C - tb, p.shape[1]), F32)], axis=0)
        prev = jnp.where(c == 0, shift_ref[bi], prev_scr[bi])
        rowi = lax.broadcasted_iota(I32, p.shape, 0)
        ps = jnp.where(rowi == 0, prev, pltpu.roll(p, 1, 0))
        prev_scr[bi] = p[C - 1:C, :]
        return p + (ps - p) * mu_ref[...]

    vec = vec_ref[...]
    w0, a0, kkp, kap, rkp, lnw, lnb = (vec[n:n + 1, :] for n in range(7))
    r2 = lax.broadcasted_iota(I32, (2 * C, 2 * C), 0)
    c2 = lax.broadcasted_iota(I32, (2 * C, 2 * C), 1)
    same_head = (r2 // C) == (c2 // C)
    head_ones = same_head.astype(BF16)
    strict = same_head & ((c2 % C) < (r2 % C))
    incl = same_head & ((c2 % C) <= (r2 % C))

    def pairs(x):
        return jnp.concatenate([x[:, p * LANES:(p + 1) * LANES] for p in range(n_pairs)], axis=0)

    def unpairs(x):
        return jnp.concatenate([x[p * C:(p + 1) * C] for p in range(n_pairs)], axis=1)

    def head_sum(x):
        xs = pairs(x)
        hi = xs.astype(BF16)
        mid = (xs - hi.astype(F32)).astype(BF16)
        return unpairs(jnp.dot(hi, head_ones, preferred_element_type=F32)
                       + jnp.dot(mid, head_ones, preferred_element_type=F32))

    rc = lax.broadcasted_iota(I32, (C, C), 0)
    cc = lax.broadcasted_iota(I32, (C, C), 1)
    tril = (cc <= rc).astype(BF16)

    def prep(bi):
        xr = shifted(bi, r_ref, pr, sr_ref, mr_ref)
        xk = shifted(bi, k_ref, pk, sk_ref, mk_ref)
        xv = shifted(bi, v_ref, pv, sv_ref, mv_ref)
        xwa = shifted(bi, wa_ref, pwa, swa_ref, mwa_ref)
        xg = shifted(bi, g_ref, pg, sg_ref, mg_ref)
        zw = w0 + _mm(jnp.tanh(xwa[:, 0:W_LORA]), w2_ref[...])
        w_log = -(jnp.maximum(-zw, 0.0) + jnp.log(1.0 + jnp.exp(-jnp.abs(zw)))) - 0.5
        ld = -jnp.exp(w_log)
        za = a0 + _mm(xwa[:, W_LORA:W_LORA + A_LORA], a2_ref[...])
        a = 1.0 / (1.0 + jnp.exp(-za))
        g = _mm(1.0 / (1.0 + jnp.exp(-xg)), g2_ref[...])
        kk = xk * kkp
        kk = kk / jnp.maximum(jnp.sqrt(head_sum(kk * kk)), 1e-12)
        k2 = xk * (1.0 + (a - 1.0) * kap)
        bonus_w = head_sum(xr * k2 * rkp)
        if t_valid < C:
            valid = lax.broadcasted_iota(I32, (C, B_W), 0) < t_valid
            ld = jnp.where(valid, ld, 0.0)
            xv = jnp.where(valid, xv, 0.0)
            kk = jnp.where(valid, kk, 0.0)
            k2m = jnp.where(valid, k2, 0.0)
        else:
            k2m = k2
        hi, mid, lo = _split3(ld)
        lgc = (jnp.dot(tril, hi, preferred_element_type=F32) + jnp.dot(tril, mid, preferred_element_type=F32)
               + jnp.dot(tril, lo, preferred_element_type=F32))
        lg_end = lgc[C - 1:C, :]
        ginv = jnp.exp(-lgc)
        e_end = jnp.exp(lg_end - lgc)
        kb_ = kk * a
        return dict(al=kk * jnp.exp(lgc - ld), be=kb_ * ginv, kt=k2m * ginv, rt=xr * jnp.exp(lgc),
                    bp=kb_ * e_end, kp=k2m * e_end, xv=xv, g_end=jnp.exp(lg_end), bonus_w=bonus_w, g=g)

    seqs = [prep(bi) for bi in range(nb)]
    lane_lo = lax.broadcasted_iota(I32, (C, LANES), 1) < HD_B

    def stack(name, q):
        bi, p = divmod(q, n_pairs)
        xp = seqs[bi][name][:, p * LANES:(p + 1) * LANES]
        return jnp.concatenate([jnp.where(lane_lo, xp, 0.0), jnp.where(lane_lo, 0.0, xp)], axis=0)

    P = range(nb * n_pairs)
    bf = lambda x: x.astype(BF16)
    lhs = [bf(jnp.concatenate([stack('al', p), stack('rt', p)], axis=0)) for p in P]
    vs = [bf(stack('xv', p)) for p in P]
    g3 = [_mm_nt(lhs[p], jnp.concatenate([bf(stack('be', p)), bf(stack('kt', p)), bf(st[p])], axis=0)) for p in P]
    gb = [g3[p][:, 0:2 * C] for p in P]
    gk = [g3[p][:, 2 * C:4 * C] for p in P]
    p0 = [g3[p][:, 4 * C:] for p in P]
    qn = [bf(jnp.where(strict, -gb[p][0:2 * C], 0.0)) for p in P]
    u = [p0[p][0:2 * C] + _mm(jnp.where(strict, gk[p][0:2 * C], 0.0), vs[p]) for p in P]
    n_levels = int(math.log2(C))
    for m in range(n_levels):
        if m + 1 < n_levels:
            qu = [_mm(qn[p], jnp.concatenate([qn[p], bf(u[p])], axis=1)) for p in P]
            u = [u[p] + qu[p][:, 2 * C:] for p in P]
            qn = [bf(qu[p][:, 0:2 * C]) for p in P]
        else:
            u = [u[p] + _mm(qn[p], u[p]) for p in P]
    m2 = [jnp.concatenate([bf(jnp.where(incl, -gb[p][2 * C:], 0.0)), bf(jnp.where(incl, gk[p][2 * C:], 0.0))], axis=1)
          for p in P]
    uv = [jnp.concatenate([bf(u[p]), vs[p]], axis=0) for p in P]
    yp = [p0[p][2 * C:] + _mm(m2[p], uv[p]) for p in P]
    upd = [lax.dot_general(jnp.concatenate([bf(-u[p]), vs[p]], axis=0),
                           bf(jnp.concatenate([stack('bp', p), stack('kp', p)], axis=0)),
                           (((0,), (0,)), ((), ())), preferred_element_type=F32) for p in P]
    for q in P:
        bi, p = divmod(q, n_pairs)
        st[q] = st[q] * seqs[bi]['g_end'][:, p * LANES:(p + 1) * LANES] + upd[q]
    for bi in range(nb):
        sq = seqs[bi]
        y = jnp.concatenate([yp[q][0:C] + yp[q][C:2 * C] for q in range(bi * n_pairs, (bi + 1) * n_pairs)],
                            axis=1)
        mu = head_sum(y) * (1.0 / HD_B)
        dy = y - mu
        var = head_sum(dy * dy) * (1.0 / HD_B)
        yn = dy * lax.rsqrt(var + GN_EPS) * lnw + lnb
        out = (yn + sq['bonus_w'] * sq['xv']) * sq['g']
        y_ref[bi] = out[0:tb].astype(y_ref.dtype)

    @pl.when(c == pl.num_programs(1) - 1)
    def _():
        for bi in range(nb):
            for p in range(n_pairs):
                s = st[bi * n_pairs + p]
                so_ref[bi, 2 * p] = s[0:HD_B, 0:HD_B]
                so_ref[bi, 2 * p + 1] = s[HD_B:, HD_B:]


def rwkv_mix(proj, shift_full, mu_full, w2, a2, g2p, vecs, s0):
    b, t, _ = proj.shape
    tb = min(t, RW_C)
    nch = t // tb
    has_state = s0 is not None
    cbs = [(B_W, OFF_R // B_W), (B_W, OFF_R // B_W + 1), (B_W, OFF_R // B_W + 2), (LANES, CB_WA), (256, CB_G)]

    nb = RW_SEQS_PER_STEP
    assert b % nb == 0
    in_specs = ([pl.BlockSpec((nb, tb, w), functools.partial(lambda cb, bi, c: (bi, c, cb), cb)) for w, cb in cbs]
                + [pl.BlockSpec((nb, 1, w), functools.partial(lambda cb, bi, c: (bi, 0, cb), cb)) for w, cb in cbs]
                + [pl.BlockSpec((1, w), functools.partial(lambda cb, bi, c: (0, cb), cb)) for w, cb in cbs]
                + [pl.BlockSpec(w2.shape, lambda bi, c: (0, 0)),
                   pl.BlockSpec(a2.shape, lambda bi, c: (0, 0)),
                   pl.BlockSpec(g2p.shape, lambda bi, c: (0, 0)),
                   pl.BlockSpec(vecs.shape, lambda bi, c: (0, 0))])
    args = [proj] * 5 + [shift_full] * 5 + [mu_full] * 5 + [w2, a2, g2p, vecs]
    if has_state:
        in_specs.append(pl.BlockSpec((nb, H_B, HD_B, HD_B), lambda bi, c: (bi, 0, 0, 0)))
        args.append(s0)
    return pl.pallas_call(
        functools.partial(_rwkv_kernel, has_state, tb),
        grid=(b // nb, nch),
        in_specs=in_specs,
        out_specs=[pl.BlockSpec((nb, tb, B_W), lambda bi, c: (bi, c, 0)),
                   pl.BlockSpec((nb, H_B, HD_B, HD_B), lambda bi, c: (bi, 0, 0, 0))],
        out_shape=[jax.ShapeDtypeStruct((b, t, B_W), BF16),
                   jax.ShapeDtypeStruct((b, H_B, HD_B, HD_B), F32)],
        scratch_shapes=[pltpu.VMEM((nb, 1, B_W), F32)] * 3
                       + [pltpu.VMEM((nb, 1, LANES), F32), pltpu.VMEM((nb, 1, 256), F32),
                          pltpu.VMEM((nb * H_B // 2, 2 * HD_B, 2 * HD_B), F32)],
        compiler_params=_cparams(("parallel", "arbitrary")),
        name="rwkv_mix",
    )(*args)


def _gelu(x):
    return 0.5 * x * (1.0 + jnp.tanh(math.sqrt(2.0 / math.pi) * (x + 0.044715 * (x * x * x))))


def _gmlp_kernel(pu_ref, pv_ref, lnw_ref, lnb_ref, ws_ref, bst_ref, y_ref, gv_ref):
    tc = pu_ref.shape[0]
    u = _gelu(pu_ref[...])
    vf = _gelu(pv_ref[...])
    mu = jnp.mean(vf, axis=1, keepdims=True)
    var = jnp.mean(jnp.square(vf - mu), axis=1, keepdims=True)
    v = (vf - mu) * lax.rsqrt(var + 1e-5) * lnw_ref[...] + lnb_ref[...]
    gv_ref[...] = v
    ri = lax.broadcasted_iota(I32, (CHUNK, CHUNK), 0)
    ci = lax.broadcasted_iota(I32, (CHUNK, CHUNK), 1)
    if tc < CHUNK:
        v = jnp.concatenate([v, jnp.zeros((CHUNK - tc, C_W), F32)], axis=0)
    vb = v.astype(BF16)
    for g in range(G_C):
        sl = slice(g * CD, (g + 1) * CD)
        wm = jnp.where(ci <= ri, ws_ref[g], 0.0).astype(BF16)
        sp = jnp.dot(wm, vb[:, sl], preferred_element_type=F32)[0:tc] + bst_ref[0:tc, g:g + 1]
        y_ref[:, sl] = (u[:, sl] * sp).astype(y_ref.dtype)


def gmlp_mix(proj, lnw, lnb, ws, bst):
    b, t, _ = proj.shape
    tc = min(t, CHUNK)
    return pl.pallas_call(
        _gmlp_kernel,
        grid=(b, t // tc),
        in_specs=[pl.BlockSpec((None, tc, C_W), lambda bi, c: (bi, c, CB_U)),
                  pl.BlockSpec((None, tc, C_W), lambda bi, c: (bi, c, CB_PV)),
                  pl.BlockSpec((1, C_W), lambda bi, c: (0, 0)),
                  pl.BlockSpec((1, C_W), lambda bi, c: (0, 0)),
                  pl.BlockSpec(ws.shape, lambda bi, c: (0, 0, 0)),
                  pl.BlockSpec(bst.shape, lambda bi, c: (0, 0))],
        out_specs=[pl.BlockSpec((None, tc, C_W), lambda bi, c: (bi, c, 0)),
                   pl.BlockSpec((None, tc, C_W), lambda bi, c: (bi, c, 0))],
        out_shape=[jax.ShapeDtypeStruct((b, t, C_W), BF16), jax.ShapeDtypeStruct((b, t, C_W), F32)],
        compiler_params=_cparams(("parallel", "parallel")),
        name="gmlp_mix",
    )(proj, proj, lnw.reshape(1, C_W), lnb.reshape(1, C_W), ws, bst)


def _layer(x, lw, attend, shift_full, s0, tm):
    b, t, d = x.shape
    x2 = x.reshape(b * t, d)
    proj = norm_matmul(x2, lw['norm_mix'], lw['w_in'], min(2 * tm, b * t), 512).reshape(b, t, PROJ_P)
    ya = attend(proj)
    yb, wkv = rwkv_mix(proj, shift_full, lw['mu_full'], lw['w2'], lw['a2'], lw['g2p'], lw['vecs'], s0)
    yc, gv = gmlp_mix(proj, lw['gmlp_ln_w'], lw['gmlp_ln_b'], lw['gmlp_ws'], lw['gmlp_bst'])
    h = out_proj(x2, ya.reshape(b * t, A_W), yb.reshape(b * t, B_W), yc.reshape(b * t, C_W), lw['w_out'], tm)
    y = ffn(h, lw['norm_ffn'], lw['ffn_gate'], lw['ffn_up'], lw['ffn_down'], tm, 512)
    k = proj[:, :, A_W:2 * A_W].reshape(b, t, H_A, HD_A)
    v = proj[:, :, 2 * A_W:3 * A_W].reshape(b, t, H_A, HD_A)
    ik = proj[:, :, OFF_IK:OFF_IK + D_IDX]
    shift_new = _rwkv_cols(proj[:, t - 1])
    return y.reshape(b, t, d), k, v, ik, shift_new, wkv, gv


def kernel(x_prompt, x_sample, cache_k, cache_v, cache_idx_k, state_rwkv_shift, state_rwkv_wkv, page_table, rel_bias, norm_mix, w_in, rwkv_mu, rwkv_w0, rwkv_w2, rwkv_a0, rwkv_a2, rwkv_g2, rwkv_kk, rwkv_ka, rwkv_rk, rwkv_ln_w, rwkv_ln_b, gmlp_ln_w, gmlp_ln_b, gmlp_ws, gmlp_b, w_out, norm_ffn, ffn_gate, ffn_up, ffn_down, norm_final):
    depth = w_in.shape[0]
    nbp, seq, d = x_prompt.shape
    db, dec_seq, _ = x_sample.shape
    assert d == D_MODEL and seq % 512 == 0 and dec_seq == SUBLANES and w_in.shape[2] == ORIG_GM + 2 * C_W

    w_in_p = _pad_cols(w_in).astype(BF16)
    w_out_b = w_out.astype(BF16)
    wg_b, wu_b, wd_b = ffn_gate.astype(BF16), ffn_up.astype(BF16), ffn_down.astype(BF16)
    mu_full = _pad_rwkv_cols(rwkv_mu)
    g2p = jnp.concatenate([rwkv_g2, jnp.zeros((depth, 256 - G_LORA, B_W), F32)], axis=1)
    vecs = jnp.stack([rwkv_w0, rwkv_a0, rwkv_kk, rwkv_ka, rwkv_rk, rwkv_ln_w, rwkv_ln_b,
                      jnp.zeros_like(rwkv_w0)], axis=1)
    bst = jnp.swapaxes(gmlp_b, 1, 2)
    shift_s = _pad_rwkv_cols(state_rwkv_shift)[:, :, None, :]
    shift_p = jnp.zeros((nbp, 1, PROJ_P), F32)
    bias_p, bias_s = bias_tiles(rel_bias, dec_seq)

    xp, xs = x_prompt, x_sample
    outs = [[] for _ in range(11)]
    for l in range(depth):
        lw = {'norm_mix': norm_mix[l], 'w_in': w_in_p[l], 'mu_full': mu_full[l:l + 1], 'w2': rwkv_w2[l],
              'a2': rwkv_a2[l], 'g2p': g2p[l], 'vecs': vecs[l], 'gmlp_ln_w': gmlp_ln_w[l],
              'gmlp_ln_b': gmlp_ln_b[l], 'gmlp_ws': gmlp_ws[l], 'gmlp_bst': bst[l], 'w_out': w_out_b[l],
              'norm_ffn': norm_ffn[l], 'ffn_gate': wg_b[l], 'ffn_up': wu_b[l], 'ffn_down': wd_b[l]}
        attend_p = functools.partial(dsa_prompt, bias_p=bias_p, rel_bias=rel_bias)
        xp, kp, vp, ikp, shp, wkvp, _ = _layer(xp, lw, attend_p, shift_p, None, 512)
        attend_s = functools.partial(dsa_sample, layer=l, cache_k=cache_k, cache_v=cache_v,
                                     cache_idx_k=cache_idx_k, page_table=page_table, bias_s=bias_s)
        xs, k_s, v_s, ik_s, sh_s, wkv_s, gv_s = _layer(xs, lw, attend_s, shift_s[l], state_rwkv_wkv[l],
                                                       db * dec_seq)
        for lst, val in zip(outs, (kp, vp, ikp, shp, wkvp, k_s, v_s, ik_s, sh_s, wkv_s, gv_s)):
            lst.append(val)
    y_prompt = rms_final(xp.reshape(nbp * seq, d), norm_final, 512).reshape(nbp, seq, d)
    y_sample = rms_final(xs.reshape(db * dec_seq, d), norm_final, db * dec_seq).reshape(db, dec_seq, d)
    return (y_prompt, y_sample) + tuple(jnp.stack(o) for o in outs)
```

```python
import functools
import math

import numpy as np
import jax
import jax.numpy as jnp
from jax import lax
from jax.experimental import pallas as pl
from jax.experimental.pallas import tpu as pltpu

F32 = jnp.float32
BF16 = jnp.bfloat16
I32 = jnp.int32

LANES = 128
SUBLANES = 8
VMEM_LIMIT = 56 * 1024 * 1024

D_MODEL = 2048
HD_A = 64
H_A = 8
A_W = H_A * HD_A
H_IDX = 8
D_IDX = 64
IDX_SCALE = (D_IDX ** -0.5) * (H_IDX ** -0.5)
TOPK_MAX = 256
N_BUCKETS = 32
MAX_DISTANCE = 128
HD_B = 64
B_W = 1024
H_B = B_W // HD_B
W_LORA = 64
A_LORA = 64
G_LORA = 160
RWKV_PROJ_W = 3 * B_W + W_LORA + A_LORA + G_LORA
GN_EPS = 64e-5
C_W = 512
CHUNK = 128
CD = 64
G_C = C_W // CD
D_FF = 5632
NORM_EPS = 1e-6
NEG = -1e30
PAGE = 128
INT_MIN = -(2 ** 31)

ORIG_IK = 3 * A_W + H_IDX * D_IDX
ORIG_R = ORIG_IK + D_IDX + H_IDX
ORIG_GM = ORIG_R + RWKV_PROJ_W
OFF_R = 2048
OFF_IK = OFF_R + 3 * B_W
OFF_WA = OFF_IK + LANES
OFF_G = OFF_WA + W_LORA + A_LORA
OFF_GM = OFF_G + 256
PROJ_P = OFF_GM + 2 * C_W
CB_Q, CB_K, CB_V, CB_IQ = 0, 1, 2, 3
CB_IK = OFF_IK // LANES
CB_WA = OFF_WA // LANES
CB_G = OFF_G // 256
CB_U = OFF_GM // C_W
CB_PV = CB_U + 1
RW_C = 64
RW_SEQS_PER_STEP = 2
assert OFF_R % B_W == 0 and OFF_G % 256 == 0 and OFF_GM % C_W == 0


def _cparams(sem):
    return pltpu.CompilerParams(dimension_semantics=sem, vmem_limit_bytes=VMEM_LIMIT)


def _pad_cols(a, axis=-1):
    a = jnp.moveaxis(a, axis, -1)
    z = lambda n: jnp.zeros(a.shape[:-1] + (n,), a.dtype)
    lora = ORIG_R + 3 * B_W
    out = jnp.concatenate([a[..., :ORIG_IK], a[..., ORIG_R:lora], a[..., ORIG_IK:ORIG_R], z(OFF_WA - OFF_IK - D_IDX - H_IDX),
                           a[..., lora:ORIG_GM], z(OFF_GM - OFF_G - G_LORA), a[..., ORIG_GM:]], axis=-1)
    return jnp.moveaxis(out, -1, axis)


def _pad_rwkv_cols(a):
    z = lambda n: jnp.zeros(a.shape[:-1] + (n,), a.dtype)
    return jnp.concatenate([z(OFF_R), a[..., :3 * B_W], z(LANES), a[..., 3 * B_W:], z(PROJ_P - OFF_G - G_LORA)], axis=-1)


def _rwkv_cols(p):
    return jnp.concatenate([p[..., OFF_R:OFF_R + 3 * B_W], p[..., OFF_WA:OFF_WA + RWKV_PROJ_W - 3 * B_W]], axis=-1)


def _norm_matmul_kernel(x_ref, g_ref, w_ref, o_ref, xn_ref):
    @pl.when(pl.program_id(1) == 0)
    def _():
        x = x_ref[...]
        y = x * lax.rsqrt(jnp.mean(x * x, axis=-1, keepdims=True) + NORM_EPS)
        xn_ref[...] = (y * g_ref[...]).astype(BF16)

    o_ref[...] = lax.dot_general(xn_ref[...], w_ref[...], (((1,), (1,)), ((), ())), preferred_element_type=F32)


def norm_matmul(x, g, wt, tm, tn):
    m, d = x.shape
    n = wt.shape[0]
    return pl.pallas_call(
        _norm_matmul_kernel,
        grid=(m // tm, n // tn),
        in_specs=[pl.BlockSpec((tm, d), lambda i, j: (i, 0)),
                  pl.BlockSpec((1, d), lambda i, j: (0, 0)),
                  pl.BlockSpec((tn, d), lambda i, j: (j, 0))],
        out_specs=pl.BlockSpec((tm, tn), lambda i, j: (i, j)),
        out_shape=jax.ShapeDtypeStruct((m, n), F32),
        scratch_shapes=[pltpu.VMEM((tm, d), BF16)],
        compiler_params=_cparams(("parallel", "arbitrary")),
        name="norm_matmul",
    )(x, g.reshape(1, d), wt)


def _out_proj_kernel(x_ref, ya_ref, yb_ref, yc_ref, w_ref, o_ref):
    acc = jnp.dot(ya_ref[...], w_ref[0:A_W, :], preferred_element_type=F32)
    acc += jnp.dot(yb_ref[...], w_ref[A_W:A_W + B_W, :], preferred_element_type=F32)
    acc += jnp.dot(yc_ref[...], w_ref[A_W + B_W:, :], preferred_element_type=F32)
    o_ref[...] = x_ref[...] + acc


def out_proj(x, ya, yb, yc, w, tm):
    m, d = x.shape
    return pl.pallas_call(
        _out_proj_kernel,
        grid=(m // tm,),
        in_specs=[pl.BlockSpec((tm, d), lambda i: (i, 0)),
                  pl.BlockSpec((tm, A_W), lambda i: (i, 0)),
                  pl.BlockSpec((tm, B_W), lambda i: (i, 0)),
                  pl.BlockSpec((tm, C_W), lambda i: (i, 0)),
                  pl.BlockSpec(w.shape, lambda i: (0, 0))],
        out_specs=pl.BlockSpec((tm, d), lambda i: (i, 0)),
        out_shape=jax.ShapeDtypeStruct((m, d), F32),
        compiler_params=_cparams(("parallel",)),
        name="out_proj",
    )(x, ya, yb, yc, w)


def _ffn_kernel(h_ref, g_ref, wg_ref, wu_ref, wd_ref, o_ref, hn_ref, acc_ref):
    f = pl.program_id(1)

    @pl.when(f == 0)
    def _():
        x = h_ref[...]
        y = x * lax.rsqrt(jnp.mean(x * x, axis=-1, keepdims=True) + NORM_EPS)
        hn_ref[...] = (y * g_ref[...]).astype(BF16)
        acc_ref[...] = jnp.zeros_like(acc_ref)

    hn = hn_ref[...]
    gate = jnp.dot(hn, wg_ref[...], preferred_element_type=F32)
    up = jnp.dot(hn, wu_ref[...], preferred_element_type=F32)
    act = (gate / (1.0 + jnp.exp(-gate))) * up
    acc_ref[...] += jnp.dot(act.astype(BF16), wd_ref[...], preferred_element_type=F32)

    @pl.when(f == pl.num_programs(1) - 1)
    def _():
        o_ref[...] = h_ref[...] + acc_ref[...]


def ffn(h, g, wg, wu, wd, tm, tf):
    m, d = h.shape
    nf = wg.shape[1]
    return pl.pallas_call(
        _ffn_kernel,
        grid=(m // tm, nf // tf),
        in_specs=[pl.BlockSpec((tm, d), lambda i, f: (i, 0)),
                  pl.BlockSpec((1, d), lambda i, f: (0, 0)),
                  pl.BlockSpec((d, tf), lambda i, f: (0, f)),
                  pl.BlockSpec((d, tf), lambda i, f: (0, f)),
                  pl.BlockSpec((tf, d), lambda i, f: (f, 0))],
        out_specs=pl.BlockSpec((tm, d), lambda i, f: (i, 0)),
        out_shape=jax.ShapeDtypeStruct((m, d), F32),
        scratch_shapes=[pltpu.VMEM((tm, d), BF16), pltpu.VMEM((tm, d), F32)],
        compiler_params=_cparams(("parallel", "arbitrary")),
        name="ffn",
    )(h, g.reshape(1, d), wg, wu, wd)


def _rms_kernel(x_ref, g_ref, o_ref):
    x = x_ref[...]
    o_ref[...] = x * lax.rsqrt(jnp.mean(x * x, axis=-1, keepdims=True) + NORM_EPS) * g_ref[...]


def rms_final(x, g, tm):
    m, d = x.shape
    return pl.pallas_call(
        _rms_kernel,
        grid=(m // tm,),
        in_specs=[pl.BlockSpec((tm, d), lambda i: (i, 0)), pl.BlockSpec((1, d), lambda i: (0, 0))],
        out_specs=pl.BlockSpec((tm, d), lambda i: (i, 0)),
        out_shape=jax.ShapeDtypeStruct((m, d), F32),
        compiler_params=_cparams(("parallel",)),
        name="rms_final",
    )(x, g.reshape(1, d))


def _bucket_np(dist):
    max_exact = N_BUCKETS // 2
    n = np.maximum(dist, 0)
    nf = np.maximum(n, 1).astype(np.float64)
    large = max_exact + (np.log(nf / max_exact) / math.log(MAX_DISTANCE / max_exact)
                         * (N_BUCKETS - max_exact)).astype(np.int32)
    return np.where(n < max_exact, n, np.minimum(large, N_BUCKETS - 1)).astype(np.int32)


def _bias_tiles_kernel(rb_ref, bp_ref, bs_ref, op_ref, os_ref):
    for d in range(2):
        bk = bp_ref[d]
        for h in range(H_A):
            t = jnp.zeros(bk.shape, F32)
            for b in range(N_BUCKETS):
                t = jnp.where(bk == b, rb_ref[b, h], t)
            op_ref[d, h] = t
    for d in range(3):
        bk = bs_ref[d]
        for h in range(H_A):
            t = jnp.zeros(bk.shape, F32)
            for b in range(N_BUCKETS):
                t = jnp.where(bk == b, rb_ref[b, h], t)
            os_ref[d, h * SUBLANES:(h + 1) * SUBLANES, :] = t


def bias_tiles(rel_bias, dec_seq):
    r = np.arange(LANES)[:, None]
    c = np.arange(LANES)[None, :]
    bp = np.stack([_bucket_np(r - c), _bucket_np(LANES + r - c)])
    t = np.arange(dec_seq)[:, None]
    bs = np.stack([_bucket_np(np.full((dec_seq, LANES), 4 * MAX_DISTANCE)),
                   _bucket_np(PAGE + t - c), _bucket_np(t - c)])
    vm = pl.BlockSpec(memory_space=pltpu.VMEM)
    return pl.pallas_call(
        _bias_tiles_kernel,
        in_specs=[pl.BlockSpec(memory_space=pltpu.SMEM), vm, vm],
        out_specs=[vm, vm],
        out_shape=[jax.ShapeDtypeStruct((2, H_A, LANES, LANES), F32),
                   jax.ShapeDtypeStruct((3, H_A * dec_seq, LANES), F32)],
        name="bias_tiles",
    )(rel_bias, jnp.asarray(bp), jnp.asarray(bs))


def _score_key(s):
    s = jnp.where(s == 0.0, 0.0, s)
    bits = lax.bitcast_convert_type(s, I32)
    return jnp.where(bits < 0, bits ^ 0x7FFFFFFF, bits)


class _Pairs:
    def __init__(self, n_pairs):
        self.n_pairs = n_pairs


def _chunk_loop(n_chunks, body, init):
    if isinstance(n_chunks, _Pairs):
        return lax.fori_loop(0, n_chunks.n_pairs, lambda m, c: body(2 * m + 1, body(2 * m, c)), init)
    return lax.fori_loop(0, n_chunks, body, init)


def _kth_largest(sc_ref, n_chunks, kf, shape):
    def count_ge(cand):
        if isinstance(n_chunks, int):
            a = jnp.sum((sc_ref[...] >= cand[None]).astype(F32), axis=0)
        else:
            parts = []
            for r0 in range(0, shape[0], LANES):
                cs = cand[r0:r0 + LANES]

                def body(kc, a, r0=r0, cs=cs):
                    return a + (sc_ref[kc, r0:r0 + LANES, :] >= cs).astype(F32)

                parts.append(_chunk_loop(n_chunks, body, jnp.zeros((LANES, LANES), F32)))
            a = parts[0] if len(parts) == 1 else jnp.concatenate(parts, axis=0)
        return jnp.sum(a, axis=1, keepdims=True)

    zero = jnp.zeros(shape, I32)
    t0 = jnp.where(count_ge(zero) >= kf, zero, jnp.full(shape, INT_MIN, I32))

    def bit_body(it, t):
        cand = t | jnp.left_shift(jnp.int32(1), 30 - it)
        return jnp.where(count_ge(cand) >= kf, cand, t)

    return lax.fori_loop(0, 31, bit_body, t0)


def _select_mask(sc_ref, mb_ref, n_chunks, kf, thr, shape, causal_fn):
    def cnt_body(kc, carry):
        key = sc_ref[kc]
        return carry[0] + (key > thr).astype(F32), carry[1] + (key >= thr).astype(F32)

    zeros = jnp.zeros(shape, F32)
    if isinstance(n_chunks, int):
        keys = sc_ref[...]
        cgt = jnp.sum((keys > thr[None]).astype(F32), axis=0)
        cge = jnp.sum((keys >= thr[None]).astype(F32), axis=0)
    else:
        cgt, cge = _chunk_loop(n_chunks, cnt_body, (zeros, zeros))
    need = kf - jnp.sum(cgt, axis=1, keepdims=True)
    cge = jnp.sum(cge, axis=1, keepdims=True)

    def exact_k():
        def body(kc, carry):
            mb_ref[kc] = jnp.where((sc_ref[kc] >= thr) & causal_fn(kc), 0.0, NEG)
            return carry

        if isinstance(n_chunks, int):
            lax.fori_loop(0, n_chunks, body, 0, unroll=8)
        else:
            _chunk_loop(n_chunks, body, 0)

    def with_ties():
        rr = lax.broadcasted_iota(I32, (LANES, LANES), 0)
        cc = lax.broadcasted_iota(I32, (LANES, LANES), 1)
        tri = (rr <= cc).astype(BF16)

        def body(kc, offs):
            key = sc_ref[kc]
            eq = key == thr
            pre = jnp.dot(eq.astype(BF16), tri, preferred_element_type=F32) + offs
            sel = (key > thr) | (eq & (pre <= need))
            mb_ref[kc] = jnp.where(sel & causal_fn(kc), 0.0, NEG)
            return jnp.broadcast_to(pre[:, LANES - 1:LANES], shape)

        _chunk_loop(n_chunks, body, zeros)

    lax.cond(jnp.max(jnp.abs(cge - kf)) > 0.5, with_ties, exact_k)


def _dsa_prompt_kernel(topk, q_ref, iq_ref, iw_ref, k_ref, v_ref, ik_ref, bias_ref, rb_ref, o_ref,
                       kb, vb, ikd, qm, iqm, iwb, sc, mb, lgs, acc, m_s, l_s):
    i = pl.program_id(1)
    tq = q_ref.shape[0]
    shape = (tq, LANES)
    nt = (((1,), (1,)), ((), ()))

    @pl.when(i == 0)
    def _():
        kb[...] = k_ref[...].astype(BF16)
        vb[...] = v_ref[...].astype(BF16)
        ik = ik_ref[:, 0:D_IDX].astype(BF16)
        ikd[...] = jnp.concatenate([ik, ik], axis=1)

    row = lax.broadcasted_iota(I32, shape, 0) + i * tq
    col = lax.broadcasted_iota(I32, shape, 1)
    lane_lo = col < HD_A
    sub = tq // LANES
    n_chunks = _Pairs((i + 1) * (sub // 2)) if sub % 2 == 0 else (i + 1) * sub
    iw = iw_ref[:, D_IDX:D_IDX + H_IDX]
    for h in range(H_A):
        ps = slice((h // 2) * LANES, (h // 2 + 1) * LANES)
        keep = lane_lo if h % 2 == 0 else jnp.logical_not(lane_lo)
        qm[h] = jnp.where(keep, q_ref[:, ps] * (HD_A ** -0.5), 0.0).astype(BF16)
        iqm[h] = jnp.where(keep, iq_ref[:, ps], 0.0).astype(BF16)
        iwb[h] = jnp.broadcast_to(iw[:, h:h + 1], shape)

    def causal(kc):
        return (col + kc * LANES) <= row

    def score_body(kc, carry):
        ikc = ikd[pl.ds(pl.multiple_of(kc * LANES, LANES), LANES), :]
        ss = [lax.dot_general(iqm[h], ikc, nt, preferred_element_type=F32) for h in range(H_IDX)]
        ws = [iwb[h] * jnp.maximum(ss[h], 0.0) for h in range(H_IDX)]
        tot = ((ws[0] + ws[1]) + (ws[2] + ws[3])) + ((ws[4] + ws[5]) + (ws[6] + ws[7]))
        tot = jnp.where(causal(kc), tot * IDX_SCALE, NEG)
        sc[kc] = _score_key(tot)
        return carry

    _chunk_loop(n_chunks, score_body, 0)
    kf = float(topk)
    thr = _kth_largest(sc, n_chunks, kf, shape)
    _select_mask(sc, mb, n_chunks, kf, thr, shape, causal)

    m_s[...] = jnp.full(m_s.shape, NEG, F32)

    def logit_body(kc, carry):
        off = pl.multiple_of(kc * LANES, LANES)
        mbc = mb[kc]
        for h in range(H_A):
            ps = slice((h // 2) * LANES, (h // 2 + 1) * LANES)
            lg = lax.dot_general(qm[h], kb[pl.ds(off, LANES), ps], nt, preferred_element_type=F32)
            tiles = []
            for s in range(sub):
                d = i * sub + s - kc
                tiles.append(jnp.where(d == 0, bias_ref[0, h],
                                       jnp.where(d == 1, bias_ref[1, h], rb_ref[N_BUCKETS - 1, h])))
            bias = tiles[0] if sub == 1 else jnp.concatenate(tiles, axis=0)
            lg = lg + bias + mbc
            lgs[h, kc] = lg
            m_s[h] = jnp.maximum(m_s[h], lg)
        return carry

    _chunk_loop(n_chunks, logit_body, 0)
    for h in range(H_A):
        m_s[h] = jnp.broadcast_to(jnp.max(m_s[h], axis=1, keepdims=True), shape)
    l_s[...] = jnp.zeros(l_s.shape, F32)
    acc[...] = jnp.zeros(acc.shape, F32)

    def attn_body(kc, carry):
        off = pl.multiple_of(kc * LANES, LANES)
        for h in range(H_A):
            ps = slice((h // 2) * LANES, (h // 2 + 1) * LANES)
            p = jnp.exp(lgs[h, kc] - m_s[h])
            l_s[h] += p
            acc[h] += jnp.dot(p.astype(BF16), vb[pl.ds(off, LANES), ps], preferred_element_type=F32)
        return carry

    _chunk_loop(n_chunks, attn_body, 0)
    for j in range(H_A // 2):
        lo = acc[2 * j] / jnp.sum(l_s[2 * j], axis=1, keepdims=True)
        hi = acc[2 * j + 1] / jnp.sum(l_s[2 * j + 1], axis=1, keepdims=True)
        o_ref[:, j * LANES:(j + 1) * LANES] = jnp.where(lane_lo, lo, hi).astype(o_ref.dtype)


def dsa_prompt(proj, bias_p, rel_bias, tq=2 * LANES):
    b, s, _ = proj.shape
    once = pl.Buffered(1)
    topk = min(TOPK_MAX, s // 4)
    nc = s // LANES
    kern = functools.partial(_dsa_prompt_kernel, topk)
    return pl.pallas_call(
        kern,
        grid=(b, s // tq),
        in_specs=[pl.BlockSpec((None, tq, A_W), lambda bi, i: (bi, i, CB_Q)),
                  pl.BlockSpec((None, tq, A_W), lambda bi, i: (bi, i, CB_IQ)),
                  pl.BlockSpec((None, tq, LANES), lambda bi, i: (bi, i, CB_IK)),
                  pl.BlockSpec((None, s, A_W), lambda bi, i: (bi, 0, CB_K), pipeline_mode=once),
                  pl.BlockSpec((None, s, A_W), lambda bi, i: (bi, 0, CB_V), pipeline_mode=once),
                  pl.BlockSpec((None, s, LANES), lambda bi, i: (bi, 0, CB_IK), pipeline_mode=once),
                  pl.BlockSpec(bias_p.shape, lambda bi, i: (0, 0, 0, 0), pipeline_mode=once),
                  pl.BlockSpec(memory_space=pltpu.SMEM)],
        out_specs=pl.BlockSpec((None, tq, A_W), lambda bi, i: (bi, i, 0)),
        out_shape=jax.ShapeDtypeStruct((b, s, A_W), BF16),
        scratch_shapes=[pltpu.VMEM((s, A_W), BF16), pltpu.VMEM((s, A_W), BF16), pltpu.VMEM((s, LANES), BF16),
                        pltpu.VMEM((H_A, tq, LANES), BF16), pltpu.VMEM((H_IDX, tq, LANES), BF16),
                        pltpu.VMEM((H_IDX, tq, LANES), F32),
                        pltpu.VMEM((nc, tq, LANES), I32), pltpu.VMEM((nc, tq, LANES), F32),
                        pltpu.VMEM((H_A, nc, tq, LANES), F32),
                        pltpu.VMEM((H_A, tq, LANES), F32), pltpu.VMEM((H_A, tq, LANES), F32),
                        pltpu.VMEM((H_A, tq, LANES), F32)],
        compiler_params=_cparams(("parallel", "arbitrary")),
        name="dsa_prompt",
    )(proj, proj, proj, proj, proj, proj, bias_p, rel_bias)


def _dsa_sample_select_kernel(pps, n_pages, topk, pt_ref, iq_ref, ikiw_ref, *rest):
    page_refs = rest[:pps]
    mb_ref = rest[pps]
    iq2, wb, sc = rest[pps + 1:]
    j = pl.program_id(1)
    t = iq_ref.shape[0]
    shape = (t, LANES)
    nc = n_pages + 1

    @pl.when(j == 0)
    def _():
        iq = iq_ref[...]
        ikiw = ikiw_ref[...]
        for h in range(H_IDX):
            iq2[h * t:(h + 1) * t, :] = iq[:, h * D_IDX:(h + 1) * D_IDX].astype(BF16)
            wb[h * t:(h + 1) * t, :] = jnp.broadcast_to(ikiw[:, D_IDX + h:D_IDX + h + 1], shape)

    def scores(s):
        n = s.shape[1]
        s = jnp.maximum(s, 0.0) * jnp.tile(wb[...], (1, n // LANES))
        return jnp.sum(s.reshape(H_IDX, t, n), axis=0) * IDX_SCALE

    ikt = jnp.concatenate([r[...] for r in page_refs], axis=1).astype(BF16)
    keys = _score_key(scores(jnp.dot(iq2[...], ikt, preferred_element_type=F32)))
    for u in range(pps):
        sc[j * pps + u] = keys[:, u * LANES:(u + 1) * LANES]

    @pl.when(j == pl.num_programs(1) - 1)
    def _():
        row = lax.broadcasted_iota(I32, shape, 0)
        col = lax.broadcasted_iota(I32, shape, 1)
        ik_new = jnp.concatenate([ikiw_ref[:, 0:D_IDX], jnp.zeros((LANES - t, D_IDX), F32)], axis=0)
        s_new = lax.dot_general(iq2[...], ik_new.astype(BF16), (((1,), (1,)), ((), ())), preferred_element_type=F32)
        s_new = jnp.where(col <= row, scores(s_new), NEG)
        sc[n_pages] = jnp.where(col < t, _score_key(s_new), INT_MIN)
        kf = float(topk)
        thr = _kth_largest(sc, nc, kf, shape)

        def causal(kc):
            return (kc < n_pages) | (col <= row)

        _select_mask(sc, mb_ref, nc, kf, thr, shape, causal)


def _dsa_sample_attn_kernel(pps, n_pages, pt_ref, q_ref, kn_ref, vn_ref, mb_ref, mbn_ref, bias_ref, *rest):
    k_refs = rest[:pps]
    v_refs = rest[pps:2 * pps]
    o_ref = rest[2 * pps]
    qbd, acc, m_s, l_s = rest[2 * pps + 1:]
    j = pl.program_id(1)
    t = q_ref.shape[0]
    rows = H_A * t

    def blockdiag(x):
        r = lax.broadcasted_iota(I32, (rows, A_W), 0) // t
        c = lax.broadcasted_iota(I32, (rows, A_W), 1) // HD_A
        return jnp.where(r == c, jnp.tile(x, (H_A, 1)), 0.0)

    @pl.when(j == 0)
    def _():
        qbd[...] = blockdiag(q_ref[...] * (HD_A ** -0.5)).astype(BF16)
        m_s[...] = jnp.full(m_s.shape, NEG, F32)
        l_s[...] = jnp.zeros(l_s.shape, F32)
        acc[...] = jnp.zeros(acc.shape, F32)

    def step(lg, pv_fn):
        m_old = m_s[...]
        m_new = jnp.maximum(m_old, jnp.max(lg, axis=1, keepdims=True))
        p = jnp.exp(lg - m_new)
        alpha = jnp.exp(m_old - m_new)
        l_s[...] = alpha * l_s[...] + jnp.sum(p, axis=1, keepdims=True)
        acc[...] = alpha * acc[...] + pv_fn(p.astype(BF16))
        m_s[...] = m_new

    kt = jnp.concatenate([r[...].reshape(A_W, PAGE) for r in k_refs], axis=1).astype(BF16)
    vt = jnp.concatenate([r[...].reshape(A_W, PAGE) for r in v_refs], axis=1).astype(BF16)
    last = j == pl.num_programs(1) - 1
    bias = jnp.concatenate([bias_ref[0]] * (pps - 1) + [jnp.where(last, bias_ref[1], bias_ref[0])], axis=1)
    mbc = jnp.concatenate([mb_ref[u] for u in range(pps)], axis=1)
    lg = jnp.dot(qbd[...], kt, preferred_element_type=F32) + bias + jnp.tile(mbc, (H_A, 1))
    step(lg, lambda p: lax.dot_general(p, vt, (((1,), (1,)), ((), ())), preferred_element_type=F32))

    @pl.when(last)
    def _():
        pad = jnp.zeros((LANES - t, A_W), F32)
        kn = jnp.concatenate([kn_ref[...], pad], axis=0).astype(BF16)
        vn = jnp.concatenate([vn_ref[...], pad], axis=0).astype(BF16)
        lgn = lax.dot_general(qbd[...], kn, (((1,), (1,)), ((), ())), preferred_element_type=F32)
        step(lgn + bias_ref[2] + jnp.tile(mbn_ref[0], (H_A, 1)),
             lambda p: jnp.dot(p, vn, preferred_element_type=F32))
        res = blockdiag_sum(acc[...] / l_s[...], t)
        o_ref[...] = res.astype(o_ref.dtype)


def blockdiag_sum(x, t):
    c = lax.broadcasted_iota(I32, (t, A_W), 1) // HD_A
    out = jnp.zeros((t, A_W), F32)
    for h in range(H_A):
        out = out + jnp.where(c == h, x[h * t:(h + 1) * t, :], 0.0)
    return out


def dsa_sample(proj, layer, cache_k, cache_v, cache_idx_k, page_table, bias_s, pps_sel=16, pps_att=16):
    db, t, _ = proj.shape
    n_pages = page_table.shape[1]
    nc = n_pages + 1
    topk = min(TOPK_MAX, (n_pages * PAGE + t) // 4)
    n_pool = cache_k.shape[1]
    ck = jnp.transpose(cache_k, (0, 1, 3, 4, 2))
    cv = jnp.transpose(cache_v, (0, 1, 3, 4, 2))
    cik = jnp.swapaxes(cache_idx_k, 2, 3)
    pt = page_table.reshape(-1)

    def page_spec(dims, pps, u):
        zeros = (0,) * len(dims)
        return pl.BlockSpec((None, None) + dims,
                            lambda b, j, ptr: (layer, ptr[b * n_pages + j * pps + u]) + zeros)

    sel = pl.pallas_call(
        functools.partial(_dsa_sample_select_kernel, pps_sel, n_pages, topk),
        grid_spec=pltpu.PrefetchScalarGridSpec(
            num_scalar_prefetch=1,
            grid=(db, n_pages // pps_sel),
            in_specs=[pl.BlockSpec((None, t, A_W), lambda b, j, ptr: (b, 0, CB_IQ)),
                      pl.BlockSpec((None, t, LANES), lambda b, j, ptr: (b, 0, CB_IK))]
                     + [page_spec((D_IDX, PAGE), pps_sel, u) for u in range(pps_sel)],
            out_specs=pl.BlockSpec((None, nc, t, LANES), lambda b, j, ptr: (b, 0, 0, 0)),
            scratch_shapes=[pltpu.VMEM((H_IDX * t, D_IDX), BF16), pltpu.VMEM((H_IDX * t, LANES), F32),
                            pltpu.VMEM((nc, t, LANES), I32)]),
        out_shape=jax.ShapeDtypeStruct((db, nc, t, LANES), F32),
        compiler_params=_cparams(("parallel", "arbitrary")),
        name="dsa_sample_select",
    )(pt, proj, proj, *([cik] * pps_sel))

    return pl.pallas_call(
        functools.partial(_dsa_sample_attn_kernel, pps_att, n_pages),
        grid_spec=pltpu.PrefetchScalarGridSpec(
            num_scalar_prefetch=1,
            grid=(db, n_pages // pps_att),
            in_specs=[pl.BlockSpec((None, t, A_W), lambda b, j, ptr: (b, 0, CB_Q)),
                      pl.BlockSpec((None, t, A_W), lambda b, j, ptr: (b, 0, CB_K)),
                      pl.BlockSpec((None, t, A_W), lambda b, j, ptr: (b, 0, CB_V)),
                      pl.BlockSpec((None, pps_att, t, LANES), lambda b, j, ptr: (b, j, 0, 0)),
                      pl.BlockSpec((None, 1, t, LANES), lambda b, j, ptr: (b, n_pages, 0, 0)),
                      pl.BlockSpec(bias_s.shape, lambda b, j, ptr: (0, 0, 0))]
                     + [page_spec((H_A, HD_A, PAGE), pps_att, u) for u in range(pps_att)] * 2,
            out_specs=pl.BlockSpec((None, t, A_W), lambda b, j, ptr: (b, 0, 0)),
            scratch_shapes=[pltpu.VMEM((H_A * t, A_W), BF16), pltpu.VMEM((H_A * t, A_W), F32),
                            pltpu.VMEM((H_A * t, 1), F32), pltpu.VMEM((H_A * t, 1), F32)]),
        out_shape=jax.ShapeDtypeStruct((db, t, A_W), BF16),
        compiler_params=_cparams(("parallel", "arbitrary")),
        name="dsa_sample_attn",
    )(pt, proj, proj, proj, sel, sel, bias_s, *([ck] * pps_att), *([cv] * pps_att))


def _mm(a, b):
    return jnp.dot(a.astype(BF16), b.astype(BF16), preferred_element_type=F32)


def _mm_nt(a, b):
    return lax.dot_general(a.astype(BF16), b.astype(BF16), (((1,), (1,)), ((), ())), preferred_element_type=F32)


def _split3(x):
    hi = x.astype(BF16)
    r1 = x - hi.astype(F32)
    mid = r1.astype(BF16)
    lo = (r1 - mid.astype(F32)).astype(BF16)
    return hi, mid, lo


def _rwkv_kernel(has_state, t_valid, r_ref, k_ref, v_ref, wa_ref, g_ref, sr_ref, sk_ref, sv_ref, swa_ref, sg_ref,
                 mr_ref, mk_ref, mv_ref, mwa_ref, mg_ref, w2_ref, a2_ref, g2_ref, vec_ref, *rest):
    if has_state:
        s0_ref, y_ref, so_ref, pr, pk, pv, pwa, pg, st = rest
    else:
        y_ref, so_ref, pr, pk, pv, pwa, pg, st = rest
    c = pl.program_id(1)
    C = RW_C
    nb, tb = r_ref.shape[0], r_ref.shape[1]
    n_pairs = H_B // 2
    zero_blk = jnp.zeros((HD_B, HD_B), F32)

    @pl.when(c == 0)
    def _():
        if has_state:
            for bi in range(nb):
                for p in range(n_pairs):
                    st[bi * n_pairs + p] = jnp.concatenate(
                        [jnp.concatenate([s0_ref[bi, 2 * p], zero_blk], axis=1),
                         jnp.concatenate([zero_blk, s0_ref[bi, 2 * p + 1]], axis=1)], axis=0)
        else:
            st[...] = jnp.zeros(st.shape, F32)

    def shifted(bi, p_ref, prev_scr, shift_ref, mu_ref):
        p = p_ref[bi]
        if tb < C:
            p = jnp.concatenate([p, jnp.zeros((C - tb, p.shape[1]), F32)], axis=0)
        prev = jnp.where(c == 0, shift_ref[bi], prev_scr[bi])
        rowi = lax.broadcasted_iota(I32, p.shape, 0)
        ps = jnp.where(rowi == 0, prev, pltpu.roll(p, 1, 0))
        prev_scr[bi] = p[C - 1:C, :]
        return p + (ps - p) * mu_ref[...]

    vec = vec_ref[...]
    w0, a0, kkp, kap, rkp, lnw, lnb = (vec[n:n + 1, :] for n in range(7))
    r2 = lax.broadcasted_iota(I32, (2 * C, 2 * C), 0)
    c2 = lax.broadcasted_iota(I32, (2 * C, 2 * C), 1)
    same_head = (r2 // C) == (c2 // C)
    head_ones = same_head.astype(BF16)
    strict = same_head & ((c2 % C) < (r2 % C))
    incl = same_head & ((c2 % C) <= (r2 % C))

    def pairs(x):
        return jnp.concatenate([x[:, p * LANES:(p + 1) * LANES] for p in range(n_pairs)], axis=0)

    def unpairs(x):
        return jnp.concatenate([x[p * C:(p + 1) * C] for p in range(n_pairs)], axis=1)

    def head_sum(x):
        xs = pairs(x)
        hi = xs.astype(BF16)
        mid = (xs - hi.astype(F32)).astype(BF16)
        return unpairs(jnp.dot(hi, head_ones, preferred_element_type=F32)
                       + jnp.dot(mid, head_ones, preferred_element_type=F32))

    rc = lax.broadcasted_iota(I32, (C, C), 0)
    cc = lax.broadcasted_iota(I32, (C, C), 1)
    tril = (cc <= rc).astype(BF16)

    def prep(bi):
        xr = shifted(bi, r_ref, pr, sr_ref, mr_ref)
        xk = shifted(bi, k_ref, pk, sk_ref, mk_ref)
        xv = shifted(bi, v_ref, pv, sv_ref, mv_ref)
        xwa = shifted(bi, wa_ref, pwa, swa_ref, mwa_ref)
        xg = shifted(bi, g_ref, pg, sg_ref, mg_ref)
        zw = w0 + _mm(jnp.tanh(xwa[:, 0:W_LORA]), w2_ref[...])
        w_log = -(jnp.maximum(-zw, 0.0) + jnp.log(1.0 + jnp.exp(-jnp.abs(zw)))) - 0.5
        ld = -jnp.exp(w_log)
        za = a0 + _mm(xwa[:, W_LORA:W_LORA + A_LORA], a2_ref[...])
        a = 1.0 / (1.0 + jnp.exp(-za))
        g = _mm(1.0 / (1.0 + jnp.exp(-xg)), g2_ref[...])
        kk = xk * kkp
        kk = kk / jnp.maximum(jnp.sqrt(head_sum(kk * kk)), 1e-12)
        k2 = xk * (1.0 + (a - 1.0) * kap)
        bonus_w = head_sum(xr * k2 * rkp)
        if t_valid < C:
            valid = lax.broadcasted_iota(I32, (C, B_W), 0) < t_valid
            ld = jnp.where(valid, ld, 0.0)
            xv = jnp.where(valid, xv, 0.0)
            kk = jnp.where(valid, kk, 0.0)
            k2m = jnp.where(valid, k2, 0.0)
        else:
            k2m = k2
        hi, mid, lo = _split3(ld)
        lgc = (jnp.dot(tril, hi, preferred_element_type=F32) + jnp.dot(tril, mid, preferred_element_type=F32)
               + jnp.dot(tril, lo, preferred_element_type=F32))
        lg_end = lgc[C - 1:C, :]
        ginv = jnp.exp(-lgc)
        e_end = jnp.exp(lg_end - lgc)
        kb_ = kk * a
        return dict(al=kk * jnp.exp(lgc - ld), be=kb_ * ginv, kt=k2m * ginv, rt=xr * jnp.exp(lgc),
                    bp=kb_ * e_end, kp=k2m * e_end, xv=xv, g_end=jnp.exp(lg_end), bonus_w=bonus_w, g=g)

    seqs = [prep(bi) for bi in range(nb)]
    lane_lo = lax.broadcasted_iota(I32, (C, LANES), 1) < HD_B

    def stack(name, q):
        bi, p = divmod(q, n_pairs)
        xp = seqs[bi][name][:, p * LANES:(p + 1) * LANES]
        return jnp.concatenate([jnp.where(lane_lo, xp, 0.0), jnp.where(lane_lo, 0.0, xp)], axis=0)

    P = range(nb * n_pairs)
    bf = lambda x: x.astype(BF16)
    lhs = [bf(jnp.concatenate([stack('al', p), stack('rt', p)], axis=0)) for p in P]
    vs = [bf(stack('xv', p)) for p in P]
    g3 = [_mm_nt(lhs[p], jnp.concatenate([bf(stack('be', p)), bf(stack('kt', p)), bf(st[p])], axis=0)) for p in P]
    gb = [g3[p][:, 0:2 * C] for p in P]
    gk = [g3[p][:, 2 * C:4 * C] for p in P]
    p0 = [g3[p][:, 4 * C:] for p in P]
    qn = [bf(jnp.where(strict, -gb[p][0:2 * C], 0.0)) for p in P]
    u = [p0[p][0:2 * C] + _mm(jnp.where(strict, gk[p][0:2 * C], 0.0), vs[p]) for p in P]
    n_levels = int(math.log2(C))
    for m in range(n_levels):
        if m + 1 < n_levels:
            qu = [_mm(qn[p], jnp.concatenate([qn[p], bf(u[p])], axis=1)) for p in P]
            u = [u[p] + qu[p][:, 2 * C:] for p in P]
            qn = [bf(qu[p][:, 0:2 * C]) for p in P]
        else:
            u = [u[p] + _mm(qn[p], u[p]) for p in P]
    m2 = [jnp.concatenate([bf(jnp.where(incl, -gb[p][2 * C:], 0.0)), bf(jnp.where(incl, gk[p][2 * C:], 0.0))], axis=1)
          for p in P]
    uv = [jnp.concatenate([bf(u[p]), vs[p]], axis=0) for p in P]
    yp = [p0[p][2 * C:] + _mm(m2[p], uv[p]) for p in P]
    upd = [lax.dot_general(jnp.concatenate([bf(-u[p]), vs[p]], axis=0),
                           bf(jnp.concatenate([stack('bp', p), stack('kp', p)], axis=0)),
                           (((0,), (0,)), ((), ())), preferred_element_type=F32) for p in P]
    for q in P:
        bi, p = divmod(q, n_pairs)
        st[q] = st[q] * seqs[bi]['g_end'][:, p * LANES:(p + 1) * LANES] + upd[q]
    for bi in range(nb):
        sq = seqs[bi]
        y = jnp.concatenate([yp[q][0:C] + yp[q][C:2 * C] for q in range(bi * n_pairs, (bi + 1) * n_pairs)],
                            axis=1)
        mu = head_sum(y) * (1.0 / HD_B)
        dy = y - mu
        var = head_sum(dy * dy) * (1.0 / HD_B)
        yn = dy * lax.rsqrt(var + GN_EPS) * lnw + lnb
        out = (yn + sq['bonus_w'] * sq['xv']) * sq['g']
        y_ref[bi] = out[0:tb].astype(y_ref.dtype)

    @pl.when(c == pl.num_programs(1) - 1)
    def _():
        for bi in range(nb):
            for p in range(n_pairs):
                s = st[bi * n_pairs + p]
                so_ref[bi, 2 * p] = s[0:HD_B, 0:HD_B]
                so_ref[bi, 2 * p + 1] = s[HD_B:, HD_B:]


def rwkv_mix(proj, shift_full, mu_full, w2, a2, g2p, vecs, s0):
    b, t, _ = proj.shape
    tb = min(t, RW_C)
    nch = t // tb
    has_state = s0 is not None
    cbs = [(B_W, OFF_R // B_W), (B_W, OFF_R // B_W + 1), (B_W, OFF_R // B_W + 2), (LANES, CB_WA), (256, CB_G)]

    nb = RW_SEQS_PER_STEP
    assert b % nb == 0
    in_specs = ([pl.BlockSpec((nb, tb, w), functools.partial(lambda cb, bi, c: (bi, c, cb), cb)) for w, cb in cbs]
                + [pl.BlockSpec((nb, 1, w), functools.partial(lambda cb, bi, c: (bi, 0, cb), cb)) for w, cb in cbs]
                + [pl.BlockSpec((1, w), functools.partial(lambda cb, bi, c: (0, cb), cb)) for w, cb in cbs]
                + [pl.BlockSpec(w2.shape, lambda bi, c: (0, 0)),
                   pl.BlockSpec(a2.shape, lambda bi, c: (0, 0)),
                   pl.BlockSpec(g2p.shape, lambda bi, c: (0, 0)),
                   pl.BlockSpec(vecs.shape, lambda bi, c: (0, 0))])
    args = [proj] * 5 + [shift_full] * 5 + [mu_full] * 5 + [w2, a2, g2p, vecs]
    if has_state:
        in_specs.append(pl.BlockSpec((nb, H_B, HD_B, HD_B), lambda bi, c: (bi, 0, 0, 0)))
        args.append(s0)
    return pl.pallas_call(
        functools.partial(_rwkv_kernel, has_state, tb),
        grid=(b // nb, nch),
        in_specs=in_specs,
        out_specs=[pl.BlockSpec((nb, tb, B_W), lambda bi, c: (bi, c, 0)),
                   pl.BlockSpec((nb, H_B, HD_B, HD_B), lambda bi, c: (bi, 0, 0, 0))],
        out_shape=[jax.ShapeDtypeStruct((b, t, B_W), BF16),
                   jax.ShapeDtypeStruct((b, H_B, HD_B, HD_B), F32)],
        scratch_shapes=[pltpu.VMEM((nb, 1, B_W), F32)] * 3
                       + [pltpu.VMEM((nb, 1, LANES), F32), pltpu.VMEM((nb, 1, 256), F32),
                          pltpu.VMEM((nb * H_B // 2, 2 * HD_B, 2 * HD_B), F32)],
        compiler_params=_cparams(("parallel", "arbitrary")),
        name="rwkv_mix",
    )(*args)


def _gelu(x):
    return 0.5 * x * (1.0 + jnp.tanh(math.sqrt(2.0 / math.pi) * (x + 0.044715 * (x * x * x))))


def _gmlp_kernel(pu_ref, pv_ref, lnw_ref, lnb_ref, ws_ref, bst_ref, y_ref, gv_ref):
    tc = pu_ref.shape[0]
    u = _gelu(pu_ref[...])
    vf = _gelu(pv_ref[...])
    mu = jnp.mean(vf, axis=1, keepdims=True)
    var = jnp.mean(jnp.square(vf - mu), axis=1, keepdims=True)
    v = (vf - mu) * lax.rsqrt(var + 1e-5) * lnw_ref[...] + lnb_ref[...]
    gv_ref[...] = v
    ri = lax.broadcasted_iota(I32, (CHUNK, CHUNK), 0)
    ci = lax.broadcasted_iota(I32, (CHUNK, CHUNK), 1)
    if tc < CHUNK:
        v = jnp.concatenate([v, jnp.zeros((CHUNK - tc, C_W), F32)], axis=0)
    vb = v.astype(BF16)
    for g in range(G_C):
        sl = slice(g * CD, (g + 1) * CD)
        wm = jnp.where(ci <= ri, ws_ref[g], 0.0).astype(BF16)
        sp = jnp.dot(wm, vb[:, sl], preferred_element_type=F32)[0:tc] + bst_ref[0:tc, g:g + 1]
        y_ref[:, sl] = (u[:, sl] * sp).astype(y_ref.dtype)


def gmlp_mix(proj, lnw, lnb, ws, bst):
    b, t, _ = proj.shape
    tc = min(t, CHUNK)
    return pl.pallas_call(
        _gmlp_kernel,
        grid=(b, t // tc),
        in_specs=[pl.BlockSpec((None, tc, C_W), lambda bi, c: (bi, c, CB_U)),
                  pl.BlockSpec((None, tc, C_W), lambda bi, c: (bi, c, CB_PV)),
                  pl.BlockSpec((1, C_W), lambda bi, c: (0, 0)),
                  pl.BlockSpec((1, C_W), lambda bi, c: (0, 0)),
                  pl.BlockSpec(ws.shape, lambda bi, c: (0, 0, 0)),
                  pl.BlockSpec(bst.shape, lambda bi, c: (0, 0))],
        out_specs=[pl.BlockSpec((None, tc, C_W), lambda bi, c: (bi, c, 0)),
                   pl.BlockSpec((None, tc, C_W), lambda bi, c: (bi, c, 0))],
        out_shape=[jax.ShapeDtypeStruct((b, t, C_W), BF16), jax.ShapeDtypeStruct((b, t, C_W), F32)],
        compiler_params=_cparams(("parallel", "parallel")),
        name="gmlp_mix",
    )(proj, proj, lnw.reshape(1, C_W), lnb.reshape(1, C_W), ws, bst)


def _layer(x, lw, attend, shift_full, s0, tm):
    b, t, d = x.shape
    x2 = x.reshape(b * t, d)
    proj = norm_matmul(x2, lw['norm_mix'], lw['w_in'], min(2 * tm, b * t), 512).reshape(b, t, PROJ_P)
    ya = attend(proj)
    yb, wkv = rwkv_mix(proj, shift_full, lw['mu_full'], lw['w2'], lw['a2'], lw['g2p'], lw['vecs'], s0)
    yc, gv = gmlp_mix(proj, lw['gmlp_ln_w'], lw['gmlp_ln_b'], lw['gmlp_ws'], lw['gmlp_bst'])
    h = out_proj(x2, ya.reshape(b * t, A_W), yb.reshape(b * t, B_W), yc.reshape(b * t, C_W), lw['w_out'], tm)
    y = ffn(h, lw['norm_ffn'], lw['ffn_gate'], lw['ffn_up'], lw['ffn_down'], tm, 512)
    k = proj[:, :, A_W:2 * A_W].reshape(b, t, H_A, HD_A)
    v = proj[:, :, 2 * A_W:3 * A_W].reshape(b, t, H_A, HD_A)
    ik = proj[:, :, OFF_IK:OFF_IK + D_IDX]
    shift_new = _rwkv_cols(proj[:, t - 1])
    return y.reshape(b, t, d), k, v, ik, shift_new, wkv, gv


def kernel(x_prompt, x_sample, cache_k, cache_v, cache_idx_k, state_rwkv_shift, state_rwkv_wkv, page_table, rel_bias, norm_mix, w_in, rwkv_mu, rwkv_w0, rwkv_w2, rwkv_a0, rwkv_a2, rwkv_g2, rwkv_kk, rwkv_ka, rwkv_rk, rwkv_ln_w, rwkv_ln_b, gmlp_ln_w, gmlp_ln_b, gmlp_ws, gmlp_b, w_out, norm_ffn, ffn_gate, ffn_up, ffn_down, norm_final):
    depth = w_in.shape[0]
    nbp, seq, d = x_prompt.shape
    db, dec_seq, _ = x_sample.shape
    assert d == D_MODEL and seq % 512 == 0 and dec_seq == SUBLANES and w_in.shape[2] == ORIG_GM + 2 * C_W

    w_in_p = _pad_cols(jnp.swapaxes(w_in, 1, 2), axis=1).astype(BF16)
    w_out_b = w_out.astype(BF16)
    wg_b, wu_b, wd_b = ffn_gate.astype(BF16), ffn_up.astype(BF16), ffn_down.astype(BF16)
    mu_full = _pad_rwkv_cols(rwkv_mu)
    g2p = jnp.concatenate([rwkv_g2, jnp.zeros((depth, 256 - G_LORA, B_W), F32)], axis=1)
    vecs = jnp.stack([rwkv_w0, rwkv_a0, rwkv_kk, rwkv_ka, rwkv_rk, rwkv_ln_w, rwkv_ln_b,
                      jnp.zeros_like(rwkv_w0)], axis=1)
    bst = jnp.swapaxes(gmlp_b, 1, 2)
    shift_s = _pad_rwkv_cols(state_rwkv_shift)[:, :, None, :]
    shift_p = jnp.zeros((nbp, 1, PROJ_P), F32)
    bias_p, bias_s = bias_tiles(rel_bias, dec_seq)

    xp, xs = x_prompt, x_sample
    outs = [[] for _ in range(11)]
    for l in range(depth):
        lw = {'norm_mix': norm_mix[l], 'w_in': w_in_p[l], 'mu_full': mu_full[l:l + 1], 'w2': rwkv_w2[l],
              'a2': rwkv_a2[l], 'g2p': g2p[l], 'vecs': vecs[l], 'gmlp_ln_w': gmlp_ln_w[l],
              'gmlp_ln_b': gmlp_ln_b[l], 'gmlp_ws': gmlp_ws[l], 'gmlp_bst': bst[l], 'w_out': w_out_b[l],
              'norm_ffn': norm_ffn[l], 'ffn_gate': wg_b[l], 'ffn_up': wu_b[l], 'ffn_down': wd_b[l]}
        attend_p = functools.partial(dsa_prompt, bias_p=bias_p, rel_bias=rel_bias)
        xp, kp, vp, ikp, shp, wkvp, _ = _layer(xp, lw, attend_p, shift_p, None, 512)
        attend_s = functools.partial(dsa_sample, layer=l, cache_k=cache_k, cache_v=cache_v,
                                     cache_idx_k=cache_idx_k, page_table=page_table, bias_s=bias_s)
        xs, k_s, v_s, ik_s, sh_s, wkv_s, gv_s = _layer(xs, lw, attend_s, shift_s[l], state_rwkv_wkv[l],
                                                       db * dec_seq)
        for lst, val in zip(outs, (kp, vp, ikp, shp, wkvp, k_s, v_s, ik_s, sh_s, wkv_s, gv_s)):
            lst.append(val)
    y_prompt = rms_final(xp.reshape(nbp * seq, d), norm_final, 512).reshape(nbp, seq, d)
    y_sample = rms_final(xs.reshape(db * dec_seq, d), norm_final, db * dec_seq).reshape(db, dec_seq, d)
    return (y_prompt, y_sample) + tuple(jnp.stack(o) for o in outs)
```

```python
import functools
import math

import numpy as np
import jax
import jax.numpy as jnp
from jax import lax
from jax.experimental import pallas as pl
from jax.experimental.pallas import tpu as pltpu

F32 = jnp.float32
BF16 = jnp.bfloat16
I32 = jnp.int32

LANES = 128
SUBLANES = 8
VMEM_LIMIT = 56 * 1024 * 1024

D_MODEL = 2048
HD_A = 64
H_A = 8
A_W = H_A * HD_A
H_IDX = 8
D_IDX = 64
IDX_SCALE = (D_IDX ** -0.5) * (H_IDX ** -0.5)
TOPK_MAX = 256
N_BUCKETS = 32
MAX_DISTANCE = 128
HD_B = 64
B_W = 1024
H_B = B_W // HD_B
W_LORA = 64
A_LORA = 64
G_LORA = 160
RWKV_PROJ_W = 3 * B_W + W_LORA + A_LORA + G_LORA
GN_EPS = 64e-5
C_W = 512
CHUNK = 128
CD = 64
G_C = C_W // CD
D_FF = 5632
NORM_EPS = 1e-6
NEG = -1e30
PAGE = 128
INT_MIN = -(2 ** 31)

ORIG_IK = 3 * A_W + H_IDX * D_IDX
ORIG_R = ORIG_IK + D_IDX + H_IDX
ORIG_GM = ORIG_R + RWKV_PROJ_W
OFF_R = 2048
OFF_IK = OFF_R + 3 * B_W
OFF_WA = OFF_IK + LANES
OFF_G = OFF_WA + W_LORA + A_LORA
OFF_GM = OFF_G + 256
PROJ_P = OFF_GM + 2 * C_W
CB_Q, CB_K, CB_V, CB_IQ = 0, 1, 2, 3
CB_IK = OFF_IK // LANES
CB_WA = OFF_WA // LANES
CB_G = OFF_G // 256
CB_U = OFF_GM // C_W
CB_PV = CB_U + 1
RW_C = 64
RW_SEQS_PER_STEP = 2
assert OFF_R % B_W == 0 and OFF_G % 256 == 0 and OFF_GM % C_W == 0


def _cparams(sem):
    return pltpu.CompilerParams(dimension_semantics=sem, vmem_limit_bytes=VMEM_LIMIT)


def _pad_cols(a, axis=-1):
    a = jnp.moveaxis(a, axis, -1)
    z = lambda n: jnp.zeros(a.shape[:-1] + (n,), a.dtype)
    lora = ORIG_R + 3 * B_W
    out = jnp.concatenate([a[..., :ORIG_IK], a[..., ORIG_R:lora], a[..., ORIG_IK:ORIG_R], z(OFF_WA - OFF_IK - D_IDX - H_IDX),
                           a[..., lora:ORIG_GM], z(OFF_GM - OFF_G - G_LORA), a[..., ORIG_GM:]], axis=-1)
    return jnp.moveaxis(out, -1, axis)


def _pad_rwkv_cols(a):
    z = lambda n: jnp.zeros(a.shape[:-1] + (n,), a.dtype)
    return jnp.concatenate([z(OFF_R), a[..., :3 * B_W], z(LANES), a[..., 3 * B_W:], z(PROJ_P - OFF_G - G_LORA)], axis=-1)


def _rwkv_cols(p):
    return jnp.concatenate([p[..., OFF_R:OFF_R + 3 * B_W], p[..., OFF_WA:OFF_WA + RWKV_PROJ_W - 3 * B_W]], axis=-1)


def _norm_matmul_kernel(x_ref, g_ref, w_ref, o_ref, xn_ref):
    @pl.when(pl.program_id(1) == 0)
    def _():
        x = x_ref[...]
        y = x * lax.rsqrt(jnp.mean(x * x, axis=-1, keepdims=True) + NORM_EPS)
        xn_ref[...] = (y * g_ref[...]).astype(BF16)

    o_ref[...] = lax.dot_general(xn_ref[...], w_ref[...], (((1,), (1,)), ((), ())), preferred_element_type=F32)


def norm_matmul(x, g, wt, layer, tm, tn):
    m, d = x.shape
    n = wt.shape[1]
    return pl.pallas_call(
        _norm_matmul_kernel,
        grid=(m // tm, n // tn),
        in_specs=[pl.BlockSpec((tm, d), lambda i, j: (i, 0)),
                  pl.BlockSpec((1, d), lambda i, j: (0, 0)),
                  pl.BlockSpec((None, tn, d), lambda i, j: (layer, j, 0))],
        out_specs=pl.BlockSpec((tm, tn), lambda i, j: (i, j)),
        out_shape=jax.ShapeDtypeStruct((m, n), F32),
        scratch_shapes=[pltpu.VMEM((tm, d), BF16)],
        compiler_params=_cparams(("parallel", "arbitrary")),
        name="norm_matmul",
    )(x, g.reshape(1, d), wt)


def _out_proj_kernel(x_ref, ya_ref, yb_ref, yc_ref, w_ref, o_ref):
    acc = jnp.dot(ya_ref[...], w_ref[0:A_W, :], preferred_element_type=F32)
    acc += jnp.dot(yb_ref[...], w_ref[A_W:A_W + B_W, :], preferred_element_type=F32)
    acc += jnp.dot(yc_ref[...], w_ref[A_W + B_W:, :], preferred_element_type=F32)
    o_ref[...] = x_ref[...] + acc


def out_proj(x, ya, yb, yc, w, layer, tm):
    m, d = x.shape
    return pl.pallas_call(
        _out_proj_kernel,
        grid=(m // tm,),
        in_specs=[pl.BlockSpec((tm, d), lambda i: (i, 0)),
                  pl.BlockSpec((tm, A_W), lambda i: (i, 0)),
                  pl.BlockSpec((tm, B_W), lambda i: (i, 0)),
                  pl.BlockSpec((tm, C_W), lambda i: (i, 0)),
                  pl.BlockSpec((None,) + w.shape[1:], lambda i: (layer, 0, 0))],
        out_specs=pl.BlockSpec((tm, d), lambda i: (i, 0)),
        out_shape=jax.ShapeDtypeStruct((m, d), F32),
        compiler_params=_cparams(("parallel",)),
        name="out_proj",
    )(x, ya, yb, yc, w)


def _ffn_kernel(h_ref, g_ref, wg_ref, wu_ref, wd_ref, o_ref, hn_ref, acc_ref):
    f = pl.program_id(1)

    @pl.when(f == 0)
    def _():
        x = h_ref[...]
        y = x * lax.rsqrt(jnp.mean(x * x, axis=-1, keepdims=True) + NORM_EPS)
        hn_ref[...] = (y * g_ref[...]).astype(BF16)
        acc_ref[...] = jnp.zeros_like(acc_ref)

    hn = hn_ref[...]
    gate = jnp.dot(hn, wg_ref[...], preferred_element_type=F32)
    up = jnp.dot(hn, wu_ref[...], preferred_element_type=F32)
    act = (gate / (1.0 + jnp.exp(-gate))) * up
    acc_ref[...] += jnp.dot(act.astype(BF16), wd_ref[...], preferred_element_type=F32)

    @pl.when(f == pl.num_programs(1) - 1)
    def _():
        o_ref[...] = h_ref[...] + acc_ref[...]


def ffn(h, g, wg, wu, wd, layer, tm, tf):
    m, d = h.shape
    nf = wg.shape[2]
    return pl.pallas_call(
        _ffn_kernel,
        grid=(m // tm, nf // tf),
        in_specs=[pl.BlockSpec((tm, d), lambda i, f: (i, 0)),
                  pl.BlockSpec((1, d), lambda i, f: (0, 0)),
                  pl.BlockSpec((None, d, tf), lambda i, f: (layer, 0, f)),
                  pl.BlockSpec((None, d, tf), lambda i, f: (layer, 0, f)),
                  pl.BlockSpec((None, tf, d), lambda i, f: (layer, f, 0))],
        out_specs=pl.BlockSpec((tm, d), lambda i, f: (i, 0)),
        out_shape=jax.ShapeDtypeStruct((m, d), F32),
        scratch_shapes=[pltpu.VMEM((tm, d), BF16), pltpu.VMEM((tm, d), F32)],
        compiler_params=_cparams(("parallel", "arbitrary")),
        name="ffn",
    )(h, g.reshape(1, d), wg, wu, wd)


def _rms_kernel(x_ref, g_ref, o_ref):
    x = x_ref[...]
    o_ref[...] = x * lax.rsqrt(jnp.mean(x * x, axis=-1, keepdims=True) + NORM_EPS) * g_ref[...]


def rms_final(x, g, tm):
    m, d = x.shape
    return pl.pallas_call(
        _rms_kernel,
        grid=(m // tm,),
        in_specs=[pl.BlockSpec((tm, d), lambda i: (i, 0)), pl.BlockSpec((1, d), lambda i: (0, 0))],
        out_specs=pl.BlockSpec((tm, d), lambda i: (i, 0)),
        out_shape=jax.ShapeDtypeStruct((m, d), F32),
        compiler_params=_cparams(("parallel",)),
        name="rms_final",
    )(x, g.reshape(1, d))


def _bucket_np(dist):
    max_exact = N_BUCKETS // 2
    n = np.maximum(dist, 0)
    nf = np.maximum(n, 1).astype(np.float64)
    large = max_exact + (np.log(nf / max_exact) / math.log(MAX_DISTANCE / max_exact)
                         * (N_BUCKETS - max_exact)).astype(np.int32)
    return np.where(n < max_exact, n, np.minimum(large, N_BUCKETS - 1)).astype(np.int32)


def _bias_tiles_kernel(rb_ref, bp_ref, bs_ref, op_ref, os_ref):
    for d in range(2):
        bk = bp_ref[d]
        for h in range(H_A):
            t = jnp.zeros(bk.shape, F32)
            for b in range(N_BUCKETS):
                t = jnp.where(bk == b, rb_ref[b, h], t)
            op_ref[d, h] = t
    for d in range(3):
        bk = bs_ref[d]
        for h in range(H_A):
            t = jnp.zeros(bk.shape, F32)
            for b in range(N_BUCKETS):
                t = jnp.where(bk == b, rb_ref[b, h], t)
            os_ref[d, h * SUBLANES:(h + 1) * SUBLANES, :] = t


def bias_tiles(rel_bias, dec_seq):
    r = np.arange(LANES)[:, None]
    c = np.arange(LANES)[None, :]
    bp = np.stack([_bucket_np(r - c), _bucket_np(LANES + r - c)])
    t = np.arange(dec_seq)[:, None]
    bs = np.stack([_bucket_np(np.full((dec_seq, LANES), 4 * MAX_DISTANCE)),
                   _bucket_np(PAGE + t - c), _bucket_np(t - c)])
    vm = pl.BlockSpec(memory_space=pltpu.VMEM)
    return pl.pallas_call(
        _bias_tiles_kernel,
        in_specs=[pl.BlockSpec(memory_space=pltpu.SMEM), vm, vm],
        out_specs=[vm, vm],
        out_shape=[jax.ShapeDtypeStruct((2, H_A, LANES, LANES), F32),
                   jax.ShapeDtypeStruct((3, H_A * dec_seq, LANES), F32)],
        name="bias_tiles",
    )(rel_bias, jnp.asarray(bp), jnp.asarray(bs))


def _score_key(s):
    s = jnp.where(s == 0.0, 0.0, s)
    bits = lax.bitcast_convert_type(s, I32)
    return jnp.where(bits < 0, bits ^ 0x7FFFFFFF, bits)


class _Pairs:
    def __init__(self, n_pairs):
        self.n_pairs = n_pairs


def _chunk_loop(n_chunks, body, init):
    if isinstance(n_chunks, _Pairs):
        return lax.fori_loop(0, n_chunks.n_pairs, lambda m, c: body(2 * m + 1, body(2 * m, c)), init)
    return lax.fori_loop(0, n_chunks, body, init)


def _kth_largest(sc_ref, n_chunks, kf, shape):
    def count_ge(cand):
        if isinstance(n_chunks, int):
            a = jnp.sum((sc_ref[...] >= cand[None]).astype(F32), axis=0)
        else:
            parts = []
            for r0 in range(0, shape[0], LANES):
                cs = cand[r0:r0 + LANES]

                def body(kc, a, r0=r0, cs=cs):
                    return a + (sc_ref[kc, r0:r0 + LANES, :] >= cs).astype(F32)

                parts.append(_chunk_loop(n_chunks, body, jnp.zeros((LANES, LANES), F32)))
            a = parts[0] if len(parts) == 1 else jnp.concatenate(parts, axis=0)
        return jnp.sum(a, axis=1, keepdims=True)

    zero = jnp.zeros(shape, I32)
    t0 = jnp.where(count_ge(zero) >= kf, zero, jnp.full(shape, INT_MIN, I32))

    def bit_body(it, t):
        cand = t | jnp.left_shift(jnp.int32(1), 30 - it)
        return jnp.where(count_ge(cand) >= kf, cand, t)

    return lax.fori_loop(0, 31, bit_body, t0)


def _kth_largest_t(sct_ref, n_chunks, kf, tq):
    groups = LANES // SUBLANES

    def count_ge(cand):
        def body(kc, a):
            ge = sct_ref[kc].reshape(groups, SUBLANES, tq) >= cand[None]
            return a + jnp.sum(ge.astype(F32), axis=0)

        a = _chunk_loop(n_chunks, body, jnp.zeros((SUBLANES, tq), F32))
        return jnp.sum(a, axis=0, keepdims=True)

    zero = jnp.zeros((SUBLANES, tq), I32)
    t0 = jnp.where(count_ge(zero) >= kf, zero, jnp.full((SUBLANES, tq), INT_MIN, I32))

    def bit_body(it, t):
        cand = t | jnp.left_shift(jnp.int32(1), 30 - it)
        return jnp.where(count_ge(cand) >= kf, cand, t)

    t = lax.fori_loop(0, 31, bit_body, t0)
    hi = jnp.transpose(jnp.broadcast_to((t[0:1] >> 16).astype(F32), (LANES, tq)))
    lo = jnp.transpose(jnp.broadcast_to((t[0:1] & 0xFFFF).astype(F32), (LANES, tq)))
    return (hi.astype(I32) << 16) | lo.astype(I32)


def _select_mask(sc_ref, mb_ref, n_chunks, kf, thr, shape, causal_fn):
    def cnt_body(kc, carry):
        key = sc_ref[kc]
        return carry[0] + (key > thr).astype(F32), carry[1] + (key >= thr).astype(F32)

    zeros = jnp.zeros(shape, F32)
    if isinstance(n_chunks, int):
        keys = sc_ref[...]
        cgt = jnp.sum((keys > thr[None]).astype(F32), axis=0)
        cge = jnp.sum((keys >= thr[None]).astype(F32), axis=0)
    else:
        cgt, cge = _chunk_loop(n_chunks, cnt_body, (zeros, zeros))
    need = kf - jnp.sum(cgt, axis=1, keepdims=True)
    cge = jnp.sum(cge, axis=1, keepdims=True)

    def exact_k():
        def body(kc, carry):
            mb_ref[kc] = jnp.where((sc_ref[kc] >= thr) & causal_fn(kc), 0.0, NEG)
            return carry

        if isinstance(n_chunks, int):
            lax.fori_loop(0, n_chunks, body, 0, unroll=8)
        else:
            _chunk_loop(n_chunks, body, 0)

    def with_ties():
        rr = lax.broadcasted_iota(I32, (LANES, LANES), 0)
        cc = lax.broadcasted_iota(I32, (LANES, LANES), 1)
        tri = (rr <= cc).astype(BF16)

        def body(kc, offs):
            key = sc_ref[kc]
            eq = key == thr
            pre = jnp.dot(eq.astype(BF16), tri, preferred_element_type=F32) + offs
            sel = (key > thr) | (eq & (pre <= need))
            mb_ref[kc] = jnp.where(sel & causal_fn(kc), 0.0, NEG)
            return jnp.broadcast_to(pre[:, LANES - 1:LANES], shape)

        _chunk_loop(n_chunks, body, zeros)

    lax.cond(jnp.max(jnp.abs(cge - kf)) > 0.5, with_ties, exact_k)


def _dsa_prompt_kernel(topk, q_ref, iq_ref, iw_ref, k_ref, v_ref, ik_ref, bias_ref, rb_ref, o_ref,
                       kb, vb, ikd, qm, iqm, iwb, sc, sct, mb, lgs, acc, m_s, l_s):
    i = pl.program_id(1)
    tq = q_ref.shape[0]
    shape = (tq, LANES)
    nt = (((1,), (1,)), ((), ()))

    @pl.when(i == 0)
    def _():
        kb[...] = k_ref[...].astype(BF16)
        vb[...] = v_ref[...].astype(BF16)
        ik = ik_ref[:, 0:D_IDX].astype(BF16)
        ikd[...] = jnp.concatenate([ik, ik], axis=1)

    row = lax.broadcasted_iota(I32, shape, 0) + i * tq
    col = lax.broadcasted_iota(I32, shape, 1)
    lane_lo = col < HD_A
    sub = tq // LANES
    n_chunks = _Pairs((i + 1) * (sub // 2)) if sub % 2 == 0 else (i + 1) * sub
    iw = iw_ref[:, D_IDX:D_IDX + H_IDX]
    for h in range(H_A):
        ps = slice((h // 2) * LANES, (h // 2 + 1) * LANES)
        keep = lane_lo if h % 2 == 0 else jnp.logical_not(lane_lo)
        qm[h] = jnp.where(keep, q_ref[:, ps] * (HD_A ** -0.5), 0.0).astype(BF16)
        iqm[h] = jnp.where(keep, iq_ref[:, ps], 0.0).astype(BF16)
        iwb[h] = jnp.broadcast_to(iw[:, h:h + 1], shape)

    def causal(kc):
        return (col + kc * LANES) <= row

    def score_body(kc, carry):
        ikc = ikd[pl.ds(pl.multiple_of(kc * LANES, LANES), LANES), :]
        ss = [lax.dot_general(iqm[h], ikc, nt, preferred_element_type=F32) for h in range(H_IDX)]
        ws = [iwb[h] * jnp.maximum(ss[h], 0.0) for h in range(H_IDX)]
        tot = ((ws[0] + ws[1]) + (ws[2] + ws[3])) + ((ws[4] + ws[5]) + (ws[6] + ws[7]))
        tot = jnp.where(causal(kc), tot * IDX_SCALE, NEG)
        sc[kc] = _score_key(tot)
        sct[kc] = _score_key(jnp.transpose(tot))
        return carry

    _chunk_loop(n_chunks, score_body, 0)
    kf = float(topk)
    thr = _kth_largest_t(sct, n_chunks, kf, tq)
    _select_mask(sc, mb, n_chunks, kf, thr, shape, causal)

    m_s[...] = jnp.full(m_s.shape, NEG, F32)

    def logit_body(kc, carry):
        off = pl.multiple_of(kc * LANES, LANES)
        mbc = mb[kc]
        for h in range(H_A):
            ps = slice((h // 2) * LANES, (h // 2 + 1) * LANES)
            lg = lax.dot_general(qm[h], kb[pl.ds(off, LANES), ps], nt, preferred_element_type=F32)
            tiles = []
            for s in range(sub):
                d = i * sub + s - kc
                tiles.append(jnp.where(d == 0, bias_ref[0, h],
                                       jnp.where(d == 1, bias_ref[1, h], rb_ref[N_BUCKETS - 1, h])))
            bias = tiles[0] if sub == 1 else jnp.concatenate(tiles, axis=0)
            lg = lg + bias + mbc
            lgs[h, kc] = lg
            m_s[h] = jnp.maximum(m_s[h], lg)
        return carry

    _chunk_loop(n_chunks, logit_body, 0)
    for h in range(H_A):
        m_s[h] = jnp.broadcast_to(jnp.max(m_s[h], axis=1, keepdims=True), shape)
    l_s[...] = jnp.zeros(l_s.shape, F32)
    acc[...] = jnp.zeros(acc.shape, F32)

    def attn_body(kc, carry):
        off = pl.multiple_of(kc * LANES, LANES)
        for h in range(H_A):
            ps = slice((h // 2) * LANES, (h // 2 + 1) * LANES)
            p = jnp.exp(lgs[h, kc] - m_s[h])
            l_s[h] += p
            acc[h] += jnp.dot(p.astype(BF16), vb[pl.ds(off, LANES), ps], preferred_element_type=F32)
        return carry

    _chunk_loop(n_chunks, attn_body, 0)
    for j in range(H_A // 2):
        lo = acc[2 * j] / jnp.sum(l_s[2 * j], axis=1, keepdims=True)
        hi = acc[2 * j + 1] / jnp.sum(l_s[2 * j + 1], axis=1, keepdims=True)
        o_ref[:, j * LANES:(j + 1) * LANES] = jnp.where(lane_lo, lo, hi).astype(o_ref.dtype)


def dsa_prompt(proj, bias_p, rel_bias, tq=2 * LANES):
    b, s, _ = proj.shape
    once = pl.Buffered(1)
    topk = min(TOPK_MAX, s // 4)
    nc = s // LANES
    kern = functools.partial(_dsa_prompt_kernel, topk)
    return pl.pallas_call(
        kern,
        grid=(b, s // tq),
        in_specs=[pl.BlockSpec((None, tq, A_W), lambda bi, i: (bi, i, CB_Q)),
                  pl.BlockSpec((None, tq, A_W), lambda bi, i: (bi, i, CB_IQ)),
                  pl.BlockSpec((None, tq, LANES), lambda bi, i: (bi, i, CB_IK)),
                  pl.BlockSpec((None, s, A_W), lambda bi, i: (bi, 0, CB_K), pipeline_mode=once),
                  pl.BlockSpec((None, s, A_W), lambda bi, i: (bi, 0, CB_V), pipeline_mode=once),
                  pl.BlockSpec((None, s, LANES), lambda bi, i: (bi, 0, CB_IK), pipeline_mode=once),
                  pl.BlockSpec(bias_p.shape, lambda bi, i: (0, 0, 0, 0), pipeline_mode=once),
                  pl.BlockSpec(memory_space=pltpu.SMEM)],
        out_specs=pl.BlockSpec((None, tq, A_W), lambda bi, i: (bi, i, 0)),
        out_shape=jax.ShapeDtypeStruct((b, s, A_W), BF16),
        scratch_shapes=[pltpu.VMEM((s, A_W), BF16), pltpu.VMEM((s, A_W), BF16), pltpu.VMEM((s, LANES), BF16),
                        pltpu.VMEM((H_A, tq, LANES), BF16), pltpu.VMEM((H_IDX, tq, LANES), BF16),
                        pltpu.VMEM((H_IDX, tq, LANES), F32),
                        pltpu.VMEM((nc, tq, LANES), I32), pltpu.VMEM((nc, LANES, tq), I32),
                        pltpu.VMEM((nc, tq, LANES), F32),
                        pltpu.VMEM((H_A, nc, tq, LANES), F32),
                        pltpu.VMEM((H_A, tq, LANES), F32), pltpu.VMEM((H_A, tq, LANES), F32),
                        pltpu.VMEM((H_A, tq, LANES), F32)],
        compiler_params=_cparams(("parallel", "arbitrary")),
        name="dsa_prompt",
    )(proj, proj, proj, proj, proj, proj, bias_p, rel_bias)


def _dsa_sample_select_kernel(pps, n_pages, topk, pt_ref, iq_ref, ikiw_ref, *rest):
    page_refs = rest[:pps]
    mb_ref = rest[pps]
    iq2, wb, sc = rest[pps + 1:]
    j = pl.program_id(1)
    t = iq_ref.shape[0]
    shape = (t, LANES)
    nc = n_pages + 1

    @pl.when(j == 0)
    def _():
        iq = iq_ref[...]
        ikiw = ikiw_ref[...]
        for h in range(H_IDX):
            iq2[h * t:(h + 1) * t, :] = iq[:, h * D_IDX:(h + 1) * D_IDX].astype(BF16)
            wb[h * t:(h + 1) * t, :] = jnp.broadcast_to(ikiw[:, D_IDX + h:D_IDX + h + 1], shape)

    def scores(s):
        n = s.shape[1]
        s = jnp.maximum(s, 0.0) * jnp.tile(wb[...], (1, n // LANES))
        return jnp.sum(s.reshape(H_IDX, t, n), axis=0) * IDX_SCALE

    ikt = jnp.concatenate([r[...] for r in page_refs], axis=1).astype(BF16)
    keys = _score_key(scores(jnp.dot(iq2[...], ikt, preferred_element_type=F32)))
    for u in range(pps):
        sc[j * pps + u] = keys[:, u * LANES:(u + 1) * LANES]

    @pl.when(j == pl.num_programs(1) - 1)
    def _():
        row = lax.broadcasted_iota(I32, shape, 0)
        col = lax.broadcasted_iota(I32, shape, 1)
        ik_new = jnp.concatenate([ikiw_ref[:, 0:D_IDX], jnp.zeros((LANES - t, D_IDX), F32)], axis=0)
        s_new = lax.dot_general(iq2[...], ik_new.astype(BF16), (((1,), (1,)), ((), ())), preferred_element_type=F32)
        s_new = jnp.where(col <= row, scores(s_new), NEG)
        sc[n_pages] = jnp.where(col < t, _score_key(s_new), INT_MIN)
        kf = float(topk)
        thr = _kth_largest(sc, nc, kf, shape)

        def causal(kc):
            return (kc < n_pages) | (col <= row)

        _select_mask(sc, mb_ref, nc, kf, thr, shape, causal)


def _dsa_sample_attn_kernel(pps, n_pages, pt_ref, q_ref, kn_ref, vn_ref, mb_ref, mbn_ref, bias_ref, *rest):
    k_refs = rest[:pps]
    v_refs = rest[pps:2 * pps]
    o_ref = rest[2 * pps]
    qbd, acc, m_s, l_s = rest[2 * pps + 1:]
    j = pl.program_id(1)
    t = q_ref.shape[0]
    rows = H_A * t

    def blockdiag(x):
        r = lax.broadcasted_iota(I32, (rows, A_W), 0) // t
        c = lax.broadcasted_iota(I32, (rows, A_W), 1) // HD_A
        return jnp.where(r == c, jnp.tile(x, (H_A, 1)), 0.0)

    @pl.when(j == 0)
    def _():
        qbd[...] = blockdiag(q_ref[...] * (HD_A ** -0.5)).astype(BF16)
        m_s[...] = jnp.full(m_s.shape, NEG, F32)
        l_s[...] = jnp.zeros(l_s.shape, F32)
        acc[...] = jnp.zeros(acc.shape, F32)

    def step(lg, pv_fn):
        m_old = m_s[...]
        m_new = jnp.maximum(m_old, jnp.max(lg, axis=1, keepdims=True))
        p = jnp.exp(lg - m_new)
        alpha = jnp.exp(m_old - m_new)
        l_s[...] = alpha * l_s[...] + jnp.sum(p, axis=1, keepdims=True)
        acc[...] = alpha * acc[...] + pv_fn(p.astype(BF16))
        m_s[...] = m_new

    kt = jnp.concatenate([r[...].reshape(A_W, PAGE) for r in k_refs], axis=1).astype(BF16)
    vt = jnp.concatenate([r[...].reshape(A_W, PAGE) for r in v_refs], axis=1).astype(BF16)
    last = j == pl.num_programs(1) - 1
    bias = jnp.concatenate([bias_ref[0]] * (pps - 1) + [jnp.where(last, bias_ref[1], bias_ref[0])], axis=1)
    mbc = jnp.concatenate([mb_ref[u] for u in range(pps)], axis=1)
    lg = jnp.dot(qbd[...], kt, preferred_element_type=F32) + bias + jnp.tile(mbc, (H_A, 1))
    step(lg, lambda p: lax.dot_general(p, vt, (((1,), (1,)), ((), ())), preferred_element_type=F32))

    @pl.when(last)
    def _():
        pad = jnp.zeros((LANES - t, A_W), F32)
        kn = jnp.concatenate([kn_ref[...], pad], axis=0).astype(BF16)
        vn = jnp.concatenate([vn_ref[...], pad], axis=0).astype(BF16)
        lgn = lax.dot_general(qbd[...], kn, (((1,), (1,)), ((), ())), preferred_element_type=F32)
        step(lgn + bias_ref[2] + jnp.tile(mbn_ref[0], (H_A, 1)),
             lambda p: jnp.dot(p, vn, preferred_element_type=F32))
        res = blockdiag_sum(acc[...] / l_s[...], t)
        o_ref[...] = res.astype(o_ref.dtype)


def blockdiag_sum(x, t):
    c = lax.broadcasted_iota(I32, (t, A_W), 1) // HD_A
    out = jnp.zeros((t, A_W), F32)
    for h in range(H_A):
        out = out + jnp.where(c == h, x[h * t:(h + 1) * t, :], 0.0)
    return out


def dsa_sample(proj, layer, cache_k, cache_v, cache_idx_k, page_table, bias_s, pps_sel=16, pps_att=16):
    db, t, _ = proj.shape
    n_pages = page_table.shape[1]
    nc = n_pages + 1
    topk = min(TOPK_MAX, (n_pages * PAGE + t) // 4)
    n_pool = cache_k.shape[1]
    ck = jnp.transpose(cache_k, (0, 1, 3, 4, 2))
    cv = jnp.transpose(cache_v, (0, 1, 3, 4, 2))
    cik = jnp.swapaxes(cache_idx_k, 2, 3)
    pt = page_table.reshape(-1)

    def page_spec(dims, pps, u):
        zeros = (0,) * len(dims)
        return pl.BlockSpec((None, None) + dims,
                            lambda b, j, ptr: (layer, ptr[b * n_pages + j * pps + u]) + zeros)

    sel = pl.pallas_call(
        functools.partial(_dsa_sample_select_kernel, pps_sel, n_pages, topk),
        grid_spec=pltpu.PrefetchScalarGridSpec(
            num_scalar_prefetch=1,
            grid=(db, n_pages // pps_sel),
            in_specs=[pl.BlockSpec((None, t, A_W), lambda b, j, ptr: (b, 0, CB_IQ)),
                      pl.BlockSpec((None, t, LANES), lambda b, j, ptr: (b, 0, CB_IK))]
                     + [page_spec((D_IDX, PAGE), pps_sel, u) for u in range(pps_sel)],
            out_specs=pl.BlockSpec((None, nc, t, LANES), lambda b, j, ptr: (b, 0, 0, 0)),
            scratch_shapes=[pltpu.VMEM((H_IDX * t, D_IDX), BF16), pltpu.VMEM((H_IDX * t, LANES), F32),
                            pltpu.VMEM((nc, t, LANES), I32)]),
        out_shape=jax.ShapeDtypeStruct((db, nc, t, LANES), F32),
        compiler_params=_cparams(("parallel", "arbitrary")),
        name="dsa_sample_select",
    )(pt, proj, proj, *([cik] * pps_sel))

    return pl.pallas_call(
        functools.partial(_dsa_sample_attn_kernel, pps_att, n_pages),
        grid_spec=pltpu.PrefetchScalarGridSpec(
            num_scalar_prefetch=1,
            grid=(db, n_pages // pps_att),
            in_specs=[pl.BlockSpec((None, t, A_W), lambda b, j, ptr: (b, 0, CB_Q)),
                      pl.BlockSpec((None, t, A_W), lambda b, j, ptr: (b, 0, CB_K)),
                      pl.BlockSpec((None, t, A_W), lambda b, j, ptr: (b, 0, CB_V)),
                      pl.BlockSpec((None, pps_att, t, LANES), lambda b, j, ptr: (b, j, 0, 0)),
                      pl.BlockSpec((None, 1, t, LANES), lambda b, j, ptr: (b, n_pages, 0, 0)),
                      pl.BlockSpec(bias_s.shape, lambda b, j, ptr: (0, 0, 0))]
                     + [page_spec((H_A, HD_A, PAGE), pps_att, u) for u in range(pps_att)] * 2,
            out_specs=pl.BlockSpec((None, t, A_W), lambda b, j, ptr: (b, 0, 0)),
            scratch_shapes=[pltpu.VMEM((H_A * t, A_W), BF16), pltpu.VMEM((H_A * t, A_W), F32),
                            pltpu.VMEM((H_A * t, 1), F32), pltpu.VMEM((H_A * t, 1), F32)]),
        out_shape=jax.ShapeDtypeStruct((db, t, A_W), BF16),
        compiler_params=_cparams(("parallel", "arbitrary")),
        name="dsa_sample_attn",
    )(pt, proj, proj, proj, sel, sel, bias_s, *([ck] * pps_att), *([cv] * pps_att))


def _mm(a, b):
    return jnp.dot(a.astype(BF16), b.astype(BF16), preferred_element_type=F32)


def _mm_nt(a, b):
    return lax.dot_general(a.astype(BF16), b.astype(BF16), (((1,), (1,)), ((), ())), preferred_element_type=F32)


def _split3(x):
    hi = x.astype(BF16)
    r1 = x - hi.astype(F32)
    mid = r1.astype(BF16)
    lo = (r1 - mid.astype(F32)).astype(BF16)
    return hi, mid, lo


def _rwkv_kernel(has_state, t_valid, r_ref, k_ref, v_ref, wa_ref, g_ref, sr_ref, sk_ref, sv_ref, swa_ref, sg_ref,
                 mr_ref, mk_ref, mv_ref, mwa_ref, mg_ref, w2_ref, a2_ref, g2_ref, vec_ref, *rest):
    if has_state:
        s0_ref, y_ref, so_ref, pr, pk, pv, pwa, pg, st = rest
    else:
        y_ref, so_ref, pr, pk, pv, pwa, pg, st = rest
    c = pl.program_id(1)
    C = RW_C
    nb, tb = r_ref.shape[0], r_ref.shape[1]
    n_pairs = H_B // 2
    zero_blk = jnp.zeros((HD_B, HD_B), F32)

    @pl.when(c == 0)
    def _():
        if has_state:
            for bi in range(nb):
                for p in range(n_pairs):
                    st[bi * n_pairs + p] = jnp.concatenate(
                        [jnp.concatenate([s0_ref[bi, 2 * p], zero_blk], axis=1),
                         jnp.concatenate([zero_blk, s0_ref[bi, 2 * p + 1]], axis=1)], axis=0)
        else:
            st[...] = jnp.zeros(st.shape, F32)

    def shifted(bi, p_ref, prev_scr, shift_ref, mu_ref):
        p = p_ref[bi]
        if tb < C:
            p = jnp.concatenate([p, jnp.zeros((C - tb, p.shape[1]), F32)], axis=0)
        prev = jnp.where(c == 0, shift_ref[bi], prev_scr[bi])
        rowi = lax.broadcasted_iota(I32, p.shape, 0)
        ps = jnp.where(rowi == 0, prev, pltpu.roll(p, 1, 0))
        prev_scr[bi] = p[C - 1:C, :]
        return p + (ps - p) * mu_ref[...]

    vec = vec_ref[...]
    w0, a0, kkp, kap, rkp, lnw, lnb = (vec[n:n + 1, :] for n in range(7))
    r2 = lax.broadcasted_iota(I32, (2 * C, 2 * C), 0)
    c2 = lax.broadcasted_iota(I32, (2 * C, 2 * C), 1)
    same_head = (r2 // C) == (c2 // C)
    head_ones = same_head.astype(BF16)
    strict = same_head & ((c2 % C) < (r2 % C))
    incl = same_head & ((c2 % C) <= (r2 % C))

    def pairs(x):
        return jnp.concatenate([x[:, p * LANES:(p + 1) * LANES] for p in range(n_pairs)], axis=0)

    def unpairs(x):
        return jnp.concatenate([x[p * C:(p + 1) * C] for p in range(n_pairs)], axis=1)

    def head_sum(x):
        xs = pairs(x)
        hi = xs.astype(BF16)
        mid = (xs - hi.astype(F32)).astype(BF16)
        return unpairs(jnp.dot(hi, head_ones, preferred_element_type=F32)
                       + jnp.dot(mid, head_ones, preferred_element_type=F32))

    rc = lax.broadcasted_iota(I32, (C, C), 0)
    cc = lax.broadcasted_iota(I32, (C, C), 1)
    tril = (cc <= rc).astype(BF16)

    def prep(bi):
        xr = shifted(bi, r_ref, pr, sr_ref, mr_ref)
        xk = shifted(bi, k_ref, pk, sk_ref, mk_ref)
        xv = shifted(bi, v_ref, pv, sv_ref, mv_ref)
        xwa = shifted(bi, wa_ref, pwa, swa_ref, mwa_ref)
        xg = shifted(bi, g_ref, pg, sg_ref, mg_ref)
        zw = w0 + _mm(jnp.tanh(xwa[:, 0:W_LORA]), w2_ref[...])
        w_log = -(jnp.maximum(-zw, 0.0) + jnp.log(1.0 + jnp.exp(-jnp.abs(zw)))) - 0.5
        ld = -jnp.exp(w_log)
        za = a0 + _mm(xwa[:, W_LORA:W_LORA + A_LORA], a2_ref[...])
        a = 1.0 / (1.0 + jnp.exp(-za))
        g = _mm(1.0 / (1.0 + jnp.exp(-xg)), g2_ref[...])
        kk = xk * kkp
        kk = kk / jnp.maximum(jnp.sqrt(head_sum(kk * kk)), 1e-12)
        k2 = xk * (1.0 + (a - 1.0) * kap)
        bonus_w = head_sum(xr * k2 * rkp)
        if t_valid < C:
            valid = lax.broadcasted_iota(I32, (C, B_W), 0) < t_valid
            ld = jnp.where(valid, ld, 0.0)
            xv = jnp.where(valid, xv, 0.0)
            kk = jnp.where(valid, kk, 0.0)
            k2m = jnp.where(valid, k2, 0.0)
        else:
            k2m = k2
        hi, mid, lo = _split3(ld)
        lgc = (jnp.dot(tril, hi, preferred_element_type=F32) + jnp.dot(tril, mid, preferred_element_type=F32)
               + jnp.dot(tril, lo, preferred_element_type=F32))
        lg_end = lgc[C - 1:C, :]
        ginv = jnp.exp(-lgc)
        e_end = jnp.exp(lg_end - lgc)
        kb_ = kk * a
        return dict(al=kk * jnp.exp(lgc - ld), be=kb_ * ginv, kt=k2m * ginv, rt=xr * jnp.exp(lgc),
                    bp=kb_ * e_end, kp=k2m * e_end, xv=xv, g_end=jnp.exp(lg_end), bonus_w=bonus_w, g=g)

    seqs = [prep(bi) for bi in range(nb)]
    lane_lo = lax.broadcasted_iota(I32, (C, LANES), 1) < HD_B

    def stack(name, q):
        bi, p = divmod(q, n_pairs)
        xp = seqs[bi][name][:, p * LANES:(p + 1) * LANES]
        return jnp.concatenate([jnp.where(lane_lo, xp, 0.0), jnp.where(lane_lo, 0.0, xp)], axis=0)

    P = range(nb * n_pairs)
    bf = lambda x: x.astype(BF16)
    lhs = [bf(jnp.concatenate([stack('al', p), stack('rt', p)], axis=0)) for p in P]
    vs = [bf(stack('xv', p)) for p in P]
    g3 = [_mm_nt(lhs[p], jnp.concatenate([bf(stack('be', p)), bf(stack('kt', p)), bf(st[p])], axis=0)) for p in P]
    gb = [g3[p][:, 0:2 * C] for p in P]
    gk = [g3[p][:, 2 * C:4 * C] for p in P]
    p0 = [g3[p][:, 4 * C:] for p in P]
    qn = [bf(jnp.where(strict, -gb[p][0:2 * C], 0.0)) for p in P]
    u = [p0[p][0:2 * C] + _mm(jnp.where(strict, gk[p][0:2 * C], 0.0), vs[p]) for p in P]
    n_levels = int(math.log2(C))
    for m in range(n_levels):
        if m + 1 < n_levels:
            qu = [_mm(qn[p], jnp.concatenate([qn[p], bf(u[p])], axis=1)) for p in P]
            u = [u[p] + qu[p][:, 2 * C:] for p in P]
            qn = [bf(qu[p][:, 0:2 * C]) for p in P]
        else:
            u = [u[p] + _mm(qn[p], u[p]) for p in P]
    m2 = [jnp.concatenate([bf(jnp.where(incl, -gb[p][2 * C:], 0.0)), bf(jnp.where(incl, gk[p][2 * C:], 0.0))], axis=1)
          for p in P]
    uv = [jnp.concatenate([bf(u[p]), vs[p]], axis=0) for p in P]
    yp = [p0[p][2 * C:] + _mm(m2[p], uv[p]) for p in P]
    upd = [lax.dot_general(jnp.concatenate([bf(-u[p]), vs[p]], axis=0),
                           bf(jnp.concatenate([stack('bp', p), stack('kp', p)], axis=0)),
                           (((0,), (0,)), ((), ())), preferred_element_type=F32) for p in P]
    for q in P:
        bi, p = divmod(q, n_pairs)
        st[q] = st[q] * seqs[bi]['g_end'][:, p * LANES:(p + 1) * LANES] + upd[q]
    for bi in range(nb):
        sq = seqs[bi]
        y = jnp.concatenate([yp[q][0:C] + yp[q][C:2 * C] for q in range(bi * n_pairs, (bi + 1) * n_pairs)],
                            axis=1)
        mu = head_sum(y) * (1.0 / HD_B)
        dy = y - mu
        var = head_sum(dy * dy) * (1.0 / HD_B)
        yn = dy * lax.rsqrt(var + GN_EPS) * lnw + lnb
        out = (yn + sq['bonus_w'] * sq['xv']) * sq['g']
        y_ref[bi] = out[0:tb].astype(y_ref.dtype)

    @pl.when(c == pl.num_programs(1) - 1)
    def _():
        for bi in range(nb):
            for p in range(n_pairs):
                s = st[bi * n_pairs + p]
                so_ref[bi, 2 * p] = s[0:HD_B, 0:HD_B]
                so_ref[bi, 2 * p + 1] = s[HD_B:, HD_B:]


def rwkv_mix(proj, shift_full, mu_full, w2, a2, g2p, vecs, s0):
    b, t, _ = proj.shape
    tb = min(t, RW_C)
    nch = t // tb
    has_state = s0 is not None
    cbs = [(B_W, OFF_R // B_W), (B_W, OFF_R // B_W + 1), (B_W, OFF_R // B_W + 2), (LANES, CB_WA), (256, CB_G)]

    nb = RW_SEQS_PER_STEP
    assert b % nb == 0
    in_specs = ([pl.BlockSpec((nb, tb, w), functools.partial(lambda cb, bi, c: (bi, c, cb), cb)) for w, cb in cbs]
                + [pl.BlockSpec((nb, 1, w), functools.partial(lambda cb, bi, c: (bi, 0, cb), cb)) for w, cb in cbs]
                + [pl.BlockSpec((1, w), functools.partial(lambda cb, bi, c: (0, cb), cb)) for w, cb in cbs]
                + [pl.BlockSpec(w2.shape, lambda bi, c: (0, 0)),
                   pl.BlockSpec(a2.shape, lambda bi, c: (0, 0)),
                   pl.BlockSpec(g2p.shape, lambda bi, c: (0, 0)),
                   pl.BlockSpec(vecs.shape, lambda bi, c: (0, 0))])
    args = [proj] * 5 + [shift_full] * 5 + [mu_full] * 5 + [w2, a2, g2p, vecs]
    if has_state:
        in_specs.append(pl.BlockSpec((nb, H_B, HD_B, HD_B), lambda bi, c: (bi, 0, 0, 0)))
        args.append(s0)
    return pl.pallas_call(
        functools.partial(_rwkv_kernel, has_state, tb),
        grid=(b // nb, nch),
        in_specs=in_specs,
        out_specs=[pl.BlockSpec((nb, tb, B_W), lambda bi, c: (bi, c, 0)),
                   pl.BlockSpec((nb, H_B, HD_B, HD_B), lambda bi, c: (bi, 0, 0, 0))],
        out_shape=[jax.ShapeDtypeStruct((b, t, B_W), BF16),
                   jax.ShapeDtypeStruct((b, H_B, HD_B, HD_B), F32)],
        scratch_shapes=[pltpu.VMEM((nb, 1, B_W), F32)] * 3
                       + [pltpu.VMEM((nb, 1, LANES), F32), pltpu.VMEM((nb, 1, 256), F32),
                          pltpu.VMEM((nb * H_B // 2, 2 * HD_B, 2 * HD_B), F32)],
        compiler_params=_cparams(("parallel", "arbitrary")),
        name="rwkv_mix",
    )(*args)


def _gelu(x):
    return 0.5 * x * (1.0 + jnp.tanh(math.sqrt(2.0 / math.pi) * (x + 0.044715 * (x * x * x))))


def _gmlp_kernel(pu_ref, pv_ref, lnw_ref, lnb_ref, ws_ref, bst_ref, y_ref, gv_ref):
    tc = pu_ref.shape[0]
    u = _gelu(pu_ref[...])
    vf = _gelu(pv_ref[...])
    mu = jnp.mean(vf, axis=1, keepdims=True)
    var = jnp.mean(jnp.square(vf - mu), axis=1, keepdims=True)
    v = (vf - mu) * lax.rsqrt(var + 1e-5) * lnw_ref[...] + lnb_ref[...]
    gv_ref[...] = v
    ri = lax.broadcasted_iota(I32, (CHUNK, CHUNK), 0)
    ci = lax.broadcasted_iota(I32, (CHUNK, CHUNK), 1)
    if tc < CHUNK:
        v = jnp.concatenate([v, jnp.zeros((CHUNK - tc, C_W), F32)], axis=0)
    vb = v.astype(BF16)
    for g in range(G_C):
        sl = slice(g * CD, (g + 1) * CD)
        wm = jnp.where(ci <= ri, ws_ref[g], 0.0).astype(BF16)
        sp = jnp.dot(wm, vb[:, sl], preferred_element_type=F32)[0:tc] + bst_ref[0:tc, g:g + 1]
        y_ref[:, sl] = (u[:, sl] * sp).astype(y_ref.dtype)


def gmlp_mix(proj, lnw, lnb, ws, bst):
    b, t, _ = proj.shape
    tc = min(t, CHUNK)
    return pl.pallas_call(
        _gmlp_kernel,
        grid=(b, t // tc),
        in_specs=[pl.BlockSpec((None, tc, C_W), lambda bi, c: (bi, c, CB_U)),
                  pl.BlockSpec((None, tc, C_W), lambda bi, c: (bi, c, CB_PV)),
                  pl.BlockSpec((1, C_W), lambda bi, c: (0, 0)),
                  pl.BlockSpec((1, C_W), lambda bi, c: (0, 0)),
                  pl.BlockSpec(ws.shape, lambda bi, c: (0, 0, 0)),
                  pl.BlockSpec(bst.shape, lambda bi, c: (0, 0))],
        out_specs=[pl.BlockSpec((None, tc, C_W), lambda bi, c: (bi, c, 0)),
                   pl.BlockSpec((None, tc, C_W), lambda bi, c: (bi, c, 0))],
        out_shape=[jax.ShapeDtypeStruct((b, t, C_W), BF16), jax.ShapeDtypeStruct((b, t, C_W), F32)],
        compiler_params=_cparams(("parallel", "parallel")),
        name="gmlp_mix",
    )(proj, proj, lnw.reshape(1, C_W), lnb.reshape(1, C_W), ws, bst)


def _layer(x, lw, attend, shift_full, s0, tm):
    b, t, d = x.shape
    x2 = x.reshape(b * t, d)
    l = lw['layer']
    proj = norm_matmul(x2, lw['norm_mix'], lw['w_in'], l, min(2 * tm, b * t), 512).reshape(b, t, PROJ_P)
    ya = attend(proj)
    yb, wkv = rwkv_mix(proj, shift_full, lw['mu_full'], lw['w2'], lw['a2'], lw['g2p'], lw['vecs'], s0)
    yc, gv = gmlp_mix(proj, lw['gmlp_ln_w'], lw['gmlp_ln_b'], lw['gmlp_ws'], lw['gmlp_bst'])
    h = out_proj(x2, ya.reshape(b * t, A_W), yb.reshape(b * t, B_W), yc.reshape(b * t, C_W), lw['w_out'], l, tm)
    y = ffn(h, lw['norm_ffn'], lw['ffn_gate'], lw['ffn_up'], lw['ffn_down'], l, tm, 512)
    k = proj[:, :, A_W:2 * A_W].reshape(b, t, H_A, HD_A)
    v = proj[:, :, 2 * A_W:3 * A_W].reshape(b, t, H_A, HD_A)
    ik = proj[:, :, OFF_IK:OFF_IK + D_IDX]
    shift_new = _rwkv_cols(proj[:, t - 1])
    return y.reshape(b, t, d), k, v, ik, shift_new, wkv, gv


def kernel(x_prompt, x_sample, cache_k, cache_v, cache_idx_k, state_rwkv_shift, state_rwkv_wkv, page_table, rel_bias, norm_mix, w_in, rwkv_mu, rwkv_w0, rwkv_w2, rwkv_a0, rwkv_a2, rwkv_g2, rwkv_kk, rwkv_ka, rwkv_rk, rwkv_ln_w, rwkv_ln_b, gmlp_ln_w, gmlp_ln_b, gmlp_ws, gmlp_b, w_out, norm_ffn, ffn_gate, ffn_up, ffn_down, norm_final):
    depth = w_in.shape[0]
    nbp, seq, d = x_prompt.shape
    db, dec_seq, _ = x_sample.shape
    assert d == D_MODEL and seq % 512 == 0 and dec_seq == SUBLANES and w_in.shape[2] == ORIG_GM + 2 * C_W

    w_in_p = _pad_cols(jnp.swapaxes(w_in, 1, 2), axis=1).astype(BF16)
    w_out_b = w_out.astype(BF16)
    wg_b, wu_b, wd_b = ffn_gate.astype(BF16), ffn_up.astype(BF16), ffn_down.astype(BF16)
    mu_full = _pad_rwkv_cols(rwkv_mu)
    g2p = jnp.concatenate([rwkv_g2, jnp.zeros((depth, 256 - G_LORA, B_W), F32)], axis=1)
    vecs = jnp.stack([rwkv_w0, rwkv_a0, rwkv_kk, rwkv_ka, rwkv_rk, rwkv_ln_w, rwkv_ln_b,
                      jnp.zeros_like(rwkv_w0)], axis=1)
    bst = jnp.swapaxes(gmlp_b, 1, 2)
    shift_s = _pad_rwkv_cols(state_rwkv_shift)[:, :, None, :]
    shift_p = jnp.zeros((nbp, 1, PROJ_P), F32)
    bias_p, bias_s = bias_tiles(rel_bias, dec_seq)

    xp, xs = x_prompt, x_sample
    outs = [[] for _ in range(11)]
    for l in range(depth):
        lw = {'layer': l, 'norm_mix': norm_mix[l], 'w_in': w_in_p, 'mu_full': mu_full[l:l + 1], 'w2': rwkv_w2[l],
              'a2': rwkv_a2[l], 'g2p': g2p[l], 'vecs': vecs[l], 'gmlp_ln_w': gmlp_ln_w[l],
              'gmlp_ln_b': gmlp_ln_b[l], 'gmlp_ws': gmlp_ws[l], 'gmlp_bst': bst[l], 'w_out': w_out_b,
              'norm_ffn': norm_ffn[l], 'ffn_gate': wg_b, 'ffn_up': wu_b, 'ffn_down': wd_b}
        attend_p = functools.partial(dsa_prompt, bias_p=bias_p, rel_bias=rel_bias)
        xp, kp, vp, ikp, shp, wkvp, _ = _layer(xp, lw, attend_p, shift_p, None, 512)
        attend_s = functools.partial(dsa_sample, layer=l, cache_k=cache_k, cache_v=cache_v,
                                     cache_idx_k=cache_idx_k, page_table=page_table, bias_s=bias_s)
        xs, k_s, v_s, ik_s, sh_s, wkv_s, gv_s = _layer(xs, lw, attend_s, shift_s[l], state_rwkv_wkv[l],
                                                       db * dec_seq)
        for lst, val in zip(outs, (kp, vp, ikp, shp, wkvp, k_s, v_s, ik_s, sh_s, wkv_s, gv_s)):
            lst.append(val)
    y_prompt = rms_final(xp.reshape(nbp * seq, d), norm_final, 512).reshape(nbp, seq, d)
    y_sample = rms_final(xs.reshape(db * dec_seq, d), norm_final, db * dec_seq).reshape(db, dec_seq, d)
    return (y_prompt, y_sample) + tuple(jnp.stack(o) for o in outs)
```

```python
import functools
import math

import numpy as np
import jax
import jax.numpy as jnp
from jax import lax
from jax.experimental import pallas as pl
from jax.experimental.pallas import tpu as pltpu

F32 = jnp.float32
BF16 = jnp.bfloat16
I32 = jnp.int32

LANES = 128
SUBLANES = 8
VMEM_LIMIT = 56 * 1024 * 1024

D_MODEL = 2048
HD_A = 64
H_A = 8
A_W = H_A * HD_A
H_IDX = 8
D_IDX = 64
IDX_SCALE = (D_IDX ** -0.5) * (H_IDX ** -0.5)
TOPK_MAX = 256
N_BUCKETS = 32
MAX_DISTANCE = 128
HD_B = 64
B_W = 1024
H_B = B_W // HD_B
W_LORA = 64
A_LORA = 64
G_LORA = 160
RWKV_PROJ_W = 3 * B_W + W_LORA + A_LORA + G_LORA
GN_EPS = 64e-5
C_W = 512
CHUNK = 128
CD = 64
G_C = C_W // CD
D_FF = 5632
NORM_EPS = 1e-6
NEG = -1e30
PAGE = 128
INT_MIN = -(2 ** 31)

ORIG_IK = 3 * A_W + H_IDX * D_IDX
ORIG_R = ORIG_IK + D_IDX + H_IDX
ORIG_GM = ORIG_R + RWKV_PROJ_W
OFF_R = 2048
OFF_IK = OFF_R + 3 * B_W
OFF_WA = OFF_IK + LANES
OFF_G = OFF_WA + W_LORA + A_LORA
OFF_GM = OFF_G + 256
PROJ_P = OFF_GM + 2 * C_W
CB_Q, CB_K, CB_V, CB_IQ = 0, 1, 2, 3
CB_IK = OFF_IK // LANES
CB_WA = OFF_WA // LANES
CB_G = OFF_G // 256
CB_U = OFF_GM // C_W
CB_PV = CB_U + 1
RW_C = 64
RW_SEQS_PER_STEP = 4
assert OFF_R % B_W == 0 and OFF_G % 256 == 0 and OFF_GM % C_W == 0


def _cparams(sem):
    return pltpu.CompilerParams(dimension_semantics=sem, vmem_limit_bytes=VMEM_LIMIT)


def _pad_cols(a, axis=-1):
    a = jnp.moveaxis(a, axis, -1)
    z = lambda n: jnp.zeros(a.shape[:-1] + (n,), a.dtype)
    lora = ORIG_R + 3 * B_W
    out = jnp.concatenate([a[..., :ORIG_IK], a[..., ORIG_R:lora], a[..., ORIG_IK:ORIG_R], z(OFF_WA - OFF_IK - D_IDX - H_IDX),
                           a[..., lora:ORIG_GM], z(OFF_GM - OFF_G - G_LORA), a[..., ORIG_GM:]], axis=-1)
    return jnp.moveaxis(out, -1, axis)


def _pad_rwkv_cols(a):
    z = lambda n: jnp.zeros(a.shape[:-1] + (n,), a.dtype)
    return jnp.concatenate([z(OFF_R), a[..., :3 * B_W], z(LANES), a[..., 3 * B_W:], z(PROJ_P - OFF_G - G_LORA)], axis=-1)


def _rwkv_cols(p):
    return jnp.concatenate([p[..., OFF_R:OFF_R + 3 * B_W], p[..., OFF_WA:OFF_WA + RWKV_PROJ_W - 3 * B_W]], axis=-1)


def _norm_matmul_kernel(x_ref, g_ref, w_ref, o_ref, xn_ref):
    @pl.when(pl.program_id(1) == 0)
    def _():
        x = x_ref[...]
        y = x * lax.rsqrt(jnp.mean(x * x, axis=-1, keepdims=True) + NORM_EPS)
        xn_ref[...] = (y * g_ref[...]).astype(BF16)

    o_ref[...] = lax.dot_general(xn_ref[...], w_ref[...], (((1,), (1,)), ((), ())), preferred_element_type=F32)


def norm_matmul(x, g, wt, layer, tm, tn):
    m, d = x.shape
    n = wt.shape[1]
    return pl.pallas_call(
        _norm_matmul_kernel,
        grid=(m // tm, n // tn),
        in_specs=[pl.BlockSpec((tm, d), lambda i, j: (i, 0)),
                  pl.BlockSpec((1, d), lambda i, j: (0, 0)),
                  pl.BlockSpec((None, tn, d), lambda i, j: (layer, j, 0))],
        out_specs=pl.BlockSpec((tm, tn), lambda i, j: (i, j)),
        out_shape=jax.ShapeDtypeStruct((m, n), F32),
        scratch_shapes=[pltpu.VMEM((tm, d), BF16)],
        compiler_params=_cparams(("parallel", "arbitrary")),
        name="norm_matmul",
    )(x, g.reshape(1, d), wt)


def _out_proj_kernel(x_ref, ya_ref, yb_ref, yc_ref, w_ref, o_ref):
    acc = jnp.dot(ya_ref[...], w_ref[0:A_W, :], preferred_element_type=F32)
    acc += jnp.dot(yb_ref[...], w_ref[A_W:A_W + B_W, :], preferred_element_type=F32)
    acc += jnp.dot(yc_ref[...], w_ref[A_W + B_W:, :], preferred_element_type=F32)
    o_ref[...] = x_ref[...] + acc


def out_proj(x, ya, yb, yc, w, layer, tm):
    m, d = x.shape
    return pl.pallas_call(
        _out_proj_kernel,
        grid=(m // tm,),
        in_specs=[pl.BlockSpec((tm, d), lambda i: (i, 0)),
                  pl.BlockSpec((tm, A_W), lambda i: (i, 0)),
                  pl.BlockSpec((tm, B_W), lambda i: (i, 0)),
                  pl.BlockSpec((tm, C_W), lambda i: (i, 0)),
                  pl.BlockSpec((None,) + w.shape[1:], lambda i: (layer, 0, 0))],
        out_specs=pl.BlockSpec((tm, d), lambda i: (i, 0)),
        out_shape=jax.ShapeDtypeStruct((m, d), F32),
        compiler_params=_cparams(("parallel",)),
        name="out_proj",
    )(x, ya, yb, yc, w)


def _ffn_kernel(h_ref, g_ref, wg_ref, wu_ref, wd_ref, o_ref, hn_ref, acc_ref):
    f = pl.program_id(1)

    @pl.when(f == 0)
    def _():
        x = h_ref[...]
        y = x * lax.rsqrt(jnp.mean(x * x, axis=-1, keepdims=True) + NORM_EPS)
        hn_ref[...] = (y * g_ref[...]).astype(BF16)
        acc_ref[...] = jnp.zeros_like(acc_ref)

    hn = hn_ref[...]
    gate = jnp.dot(hn, wg_ref[...], preferred_element_type=F32)
    up = jnp.dot(hn, wu_ref[...], preferred_element_type=F32)
    act = (gate / (1.0 + jnp.exp(-gate))) * up
    acc_ref[...] += jnp.dot(act.astype(BF16), wd_ref[...], preferred_element_type=F32)

    @pl.when(f == pl.num_programs(1) - 1)
    def _():
        o_ref[...] = h_ref[...] + acc_ref[...]


def ffn(h, g, wg, wu, wd, layer, tm, tf):
    m, d = h.shape
    nf = wg.shape[2]
    return pl.pallas_call(
        _ffn_kernel,
        grid=(m // tm, nf // tf),
        in_specs=[pl.BlockSpec((tm, d), lambda i, f: (i, 0)),
                  pl.BlockSpec((1, d), lambda i, f: (0, 0)),
                  pl.BlockSpec((None, d, tf), lambda i, f: (layer, 0, f)),
                  pl.BlockSpec((None, d, tf), lambda i, f: (layer, 0, f)),
                  pl.BlockSpec((None, tf, d), lambda i, f: (layer, f, 0))],
        out_specs=pl.BlockSpec((tm, d), lambda i, f: (i, 0)),
        out_shape=jax.ShapeDtypeStruct((m, d), F32),
        scratch_shapes=[pltpu.VMEM((tm, d), BF16), pltpu.VMEM((tm, d), F32)],
        compiler_params=_cparams(("parallel", "arbitrary")),
        name="ffn",
    )(h, g.reshape(1, d), wg, wu, wd)


def _rms_kernel(x_ref, g_ref, o_ref):
    x = x_ref[...]
    o_ref[...] = x * lax.rsqrt(jnp.mean(x * x, axis=-1, keepdims=True) + NORM_EPS) * g_ref[...]


def rms_final(x, g, tm):
    m, d = x.shape
    return pl.pallas_call(
        _rms_kernel,
        grid=(m // tm,),
        in_specs=[pl.BlockSpec((tm, d), lambda i: (i, 0)), pl.BlockSpec((1, d), lambda i: (0, 0))],
        out_specs=pl.BlockSpec((tm, d), lambda i: (i, 0)),
        out_shape=jax.ShapeDtypeStruct((m, d), F32),
        compiler_params=_cparams(("parallel",)),
        name="rms_final",
    )(x, g.reshape(1, d))


def _bucket_np(dist):
    max_exact = N_BUCKETS // 2
    n = np.maximum(dist, 0)
    nf = np.maximum(n, 1).astype(np.float64)
    large = max_exact + (np.log(nf / max_exact) / math.log(MAX_DISTANCE / max_exact)
                         * (N_BUCKETS - max_exact)).astype(np.int32)
    return np.where(n < max_exact, n, np.minimum(large, N_BUCKETS - 1)).astype(np.int32)


def _bias_tiles_kernel(rb_ref, bp_ref, bs_ref, op_ref, os_ref):
    for d in range(2):
        bk = bp_ref[d]
        for h in range(H_A):
            t = jnp.zeros(bk.shape, F32)
            for b in range(N_BUCKETS):
                t = jnp.where(bk == b, rb_ref[b, h], t)
            op_ref[d, h] = t
    for d in range(3):
        bk = bs_ref[d]
        for h in range(H_A):
            t = jnp.zeros(bk.shape, F32)
            for b in range(N_BUCKETS):
                t = jnp.where(bk == b, rb_ref[b, h], t)
            os_ref[d, h * SUBLANES:(h + 1) * SUBLANES, :] = t


def bias_tiles(rel_bias, dec_seq):
    r = np.arange(LANES)[:, None]
    c = np.arange(LANES)[None, :]
    bp = np.stack([_bucket_np(r - c), _bucket_np(LANES + r - c)])
    t = np.arange(dec_seq)[:, None]
    bs = np.stack([_bucket_np(np.full((dec_seq, LANES), 4 * MAX_DISTANCE)),
                   _bucket_np(PAGE + t - c), _bucket_np(t - c)])
    vm = pl.BlockSpec(memory_space=pltpu.VMEM)
    return pl.pallas_call(
        _bias_tiles_kernel,
        in_specs=[pl.BlockSpec(memory_space=pltpu.SMEM), vm, vm],
        out_specs=[vm, vm],
        out_shape=[jax.ShapeDtypeStruct((2, H_A, LANES, LANES), F32),
                   jax.ShapeDtypeStruct((3, H_A * dec_seq, LANES), F32)],
        name="bias_tiles",
    )(rel_bias, jnp.asarray(bp), jnp.asarray(bs))


def _score_key(s):
    s = jnp.where(s == 0.0, 0.0, s)
    bits = lax.bitcast_convert_type(s, I32)
    return jnp.where(bits < 0, bits ^ 0x7FFFFFFF, bits)


class _Pairs:
    def __init__(self, n_pairs):
        self.n_pairs = n_pairs


def _chunk_loop(n_chunks, body, init):
    if isinstance(n_chunks, _Pairs):
        return lax.fori_loop(0, n_chunks.n_pairs, lambda m, c: body(2 * m + 1, body(2 * m, c)), init)
    return lax.fori_loop(0, n_chunks, body, init)


def _kth_largest(sc_ref, n_chunks, kf, shape):
    def count_ge(cand):
        if isinstance(n_chunks, int):
            ge = (sc_ref[...] >= cand[None]).astype(F32)
            q = -(-n_chunks // 4)
            a = ((jnp.sum(ge[0:q], axis=0) + jnp.sum(ge[q:2 * q], axis=0))
                 + (jnp.sum(ge[2 * q:3 * q], axis=0) + jnp.sum(ge[3 * q:], axis=0)))
        else:
            parts = []
            for r0 in range(0, shape[0], LANES):
                cs = cand[r0:r0 + LANES]

                def body(kc, a, r0=r0, cs=cs):
                    return a + (sc_ref[kc, r0:r0 + LANES, :] >= cs).astype(F32)

                parts.append(_chunk_loop(n_chunks, body, jnp.zeros((LANES, LANES), F32)))
            a = parts[0] if len(parts) == 1 else jnp.concatenate(parts, axis=0)
        return jnp.sum(a, axis=1, keepdims=True)

    zero = jnp.zeros(shape, I32)
    t0 = jnp.where(count_ge(zero) >= kf, zero, jnp.full(shape, INT_MIN, I32))

    def bit_body(it, t):
        cand = t | jnp.left_shift(jnp.int32(1), 30 - it)
        return jnp.where(count_ge(cand) >= kf, cand, t)

    return lax.fori_loop(0, 31, bit_body, t0)


def _kth_largest_t(sct_ref, n_chunks, kf, tq):
    groups = LANES // SUBLANES

    def count_ge(cand):
        def body(kc, a):
            ge = sct_ref[kc].reshape(groups, SUBLANES, tq) >= cand[None]
            return a + jnp.sum(ge.astype(F32), axis=0)

        a = _chunk_loop(n_chunks, body, jnp.zeros((SUBLANES, tq), F32))
        return jnp.sum(a, axis=0, keepdims=True)

    zero = jnp.zeros((SUBLANES, tq), I32)
    t0 = jnp.where(count_ge(zero) >= kf, zero, jnp.full((SUBLANES, tq), INT_MIN, I32))

    def bit_body(it, t):
        cand = t | jnp.left_shift(jnp.int32(1), 30 - it)
        return jnp.where(count_ge(cand) >= kf, cand, t)

    t = lax.fori_loop(0, 31, bit_body, t0)
    hi = jnp.transpose(jnp.broadcast_to((t[0:1] >> 16).astype(F32), (LANES, tq)))
    lo = jnp.transpose(jnp.broadcast_to((t[0:1] & 0xFFFF).astype(F32), (LANES, tq)))
    return (hi.astype(I32) << 16) | lo.astype(I32)


def _select_mask(sc_ref, mb_ref, n_chunks, kf, thr, shape, causal_fn):
    def cnt_body(kc, carry):
        key = sc_ref[kc]
        return carry[0] + (key > thr).astype(F32), carry[1] + (key >= thr).astype(F32)

    zeros = jnp.zeros(shape, F32)
    if isinstance(n_chunks, int):
        keys = sc_ref[...]
        cgt = jnp.sum((keys > thr[None]).astype(F32), axis=0)
        cge = jnp.sum((keys >= thr[None]).astype(F32), axis=0)
    else:
        cgt, cge = _chunk_loop(n_chunks, cnt_body, (zeros, zeros))
    need = kf - jnp.sum(cgt, axis=1, keepdims=True)
    cge = jnp.sum(cge, axis=1, keepdims=True)

    def exact_k():
        def body(kc, carry):
            mb_ref[kc] = jnp.where((sc_ref[kc] >= thr) & causal_fn(kc), 0.0, NEG)
            return carry

        if isinstance(n_chunks, int):
            lax.fori_loop(0, n_chunks, body, 0, unroll=8)
        else:
            _chunk_loop(n_chunks, body, 0)

    def with_ties():
        rr = lax.broadcasted_iota(I32, (LANES, LANES), 0)
        cc = lax.broadcasted_iota(I32, (LANES, LANES), 1)
        tri = (rr <= cc).astype(BF16)

        def body(kc, offs):
            key = sc_ref[kc]
            eq = key == thr
            pre = jnp.dot(eq.astype(BF16), tri, preferred_element_type=F32) + offs
            sel = (key > thr) | (eq & (pre <= need))
            mb_ref[kc] = jnp.where(sel & causal_fn(kc), 0.0, NEG)
            return jnp.broadcast_to(pre[:, LANES - 1:LANES], shape)

        _chunk_loop(n_chunks, body, zeros)

    lax.cond(jnp.max(jnp.abs(cge - kf)) > 0.5, with_ties, exact_k)


def _dsa_prompt_kernel(topk, q_ref, iq_ref, iw_ref, k_ref, v_ref, ik_ref, bias_ref, rb_ref, o_ref,
                       kb, vb, ikd, qm, iqm, iwb, sc, sct, mb, lgs, acc, m_s, l_s):
    i = pl.program_id(1)
    tq = q_ref.shape[0]
    shape = (tq, LANES)
    nt = (((1,), (1,)), ((), ()))

    @pl.when(i == 0)
    def _():
        kb[...] = k_ref[...].astype(BF16)
        vb[...] = v_ref[...].astype(BF16)
        ik = ik_ref[:, 0:D_IDX].astype(BF16)
        ikd[...] = jnp.concatenate([ik, ik], axis=1)

    row = lax.broadcasted_iota(I32, shape, 0) + i * tq
    col = lax.broadcasted_iota(I32, shape, 1)
    lane_lo = col < HD_A
    sub = tq // LANES
    n_chunks = _Pairs((i + 1) * (sub // 2)) if sub % 2 == 0 else (i + 1) * sub
    iw = iw_ref[:, D_IDX:D_IDX + H_IDX]
    for h in range(H_A):
        ps = slice((h // 2) * LANES, (h // 2 + 1) * LANES)
        keep = lane_lo if h % 2 == 0 else jnp.logical_not(lane_lo)
        qm[h] = jnp.where(keep, q_ref[:, ps] * (HD_A ** -0.5), 0.0).astype(BF16)
        iqm[h] = jnp.where(keep, iq_ref[:, ps], 0.0).astype(BF16)
        iwb[h] = jnp.broadcast_to(iw[:, h:h + 1], shape)

    def causal(kc):
        return (col + kc * LANES) <= row

    def score_body(kc, carry):
        ikc = ikd[pl.ds(pl.multiple_of(kc * LANES, LANES), LANES), :]
        ss = [lax.dot_general(iqm[h], ikc, nt, preferred_element_type=F32) for h in range(H_IDX)]
        ws = [iwb[h] * jnp.maximum(ss[h], 0.0) for h in range(H_IDX)]
        tot = ((ws[0] + ws[1]) + (ws[2] + ws[3])) + ((ws[4] + ws[5]) + (ws[6] + ws[7]))
        tot = jnp.where(causal(kc), tot * IDX_SCALE, NEG)
        sc[kc] = _score_key(tot)
        sct[kc] = _score_key(jnp.transpose(tot))
        return carry

    _chunk_loop(n_chunks, score_body, 0)
    kf = float(topk)
    thr = _kth_largest_t(sct, n_chunks, kf, tq)
    _select_mask(sc, mb, n_chunks, kf, thr, shape, causal)

    m_s[...] = jnp.full(m_s.shape, NEG, F32)

    def logit_body(kc, carry):
        off = pl.multiple_of(kc * LANES, LANES)
        mbc = mb[kc]
        for h in range(H_A):
            ps = slice((h // 2) * LANES, (h // 2 + 1) * LANES)
            lg = lax.dot_general(qm[h], kb[pl.ds(off, LANES), ps], nt, preferred_element_type=F32)
            tiles = []
            for s in range(sub):
                d = i * sub + s - kc
                tiles.append(jnp.where(d == 0, bias_ref[0, h],
                                       jnp.where(d == 1, bias_ref[1, h], rb_ref[N_BUCKETS - 1, h])))
            bias = tiles[0] if sub == 1 else jnp.concatenate(tiles, axis=0)
            lg = lg + bias + mbc
            lgs[h, kc] = lg
            m_s[h] = jnp.maximum(m_s[h], lg)
        return carry

    _chunk_loop(n_chunks, logit_body, 0)
    for h in range(H_A):
        m_s[h] = jnp.broadcast_to(jnp.max(m_s[h], axis=1, keepdims=True), shape)
    l_s[...] = jnp.zeros(l_s.shape, F32)
    acc[...] = jnp.zeros(acc.shape, F32)

    def attn_body(kc, carry):
        off = pl.multiple_of(kc * LANES, LANES)
        for h in range(H_A):
            ps = slice((h // 2) * LANES, (h // 2 + 1) * LANES)
            p = jnp.exp(lgs[h, kc] - m_s[h])
            l_s[h] += p
            acc[h] += jnp.dot(p.astype(BF16), vb[pl.ds(off, LANES), ps], preferred_element_type=F32)
        return carry

    _chunk_loop(n_chunks, attn_body, 0)
    for j in range(H_A // 2):
        lo = acc[2 * j] / jnp.sum(l_s[2 * j], axis=1, keepdims=True)
        hi = acc[2 * j + 1] / jnp.sum(l_s[2 * j + 1], axis=1, keepdims=True)
        o_ref[:, j * LANES:(j + 1) * LANES] = jnp.where(lane_lo, lo, hi).astype(o_ref.dtype)


def dsa_prompt(proj, bias_p, rel_bias, tq=2 * LANES):
    b, s, _ = proj.shape
    once = pl.Buffered(1)
    topk = min(TOPK_MAX, s // 4)
    nc = s // LANES
    kern = functools.partial(_dsa_prompt_kernel, topk)
    return pl.pallas_call(
        kern,
        grid=(b, s // tq),
        in_specs=[pl.BlockSpec((None, tq, A_W), lambda bi, i: (bi, i, CB_Q)),
                  pl.BlockSpec((None, tq, A_W), lambda bi, i: (bi, i, CB_IQ)),
                  pl.BlockSpec((None, tq, LANES), lambda bi, i: (bi, i, CB_IK)),
                  pl.BlockSpec((None, s, A_W), lambda bi, i: (bi, 0, CB_K), pipeline_mode=once),
                  pl.BlockSpec((None, s, A_W), lambda bi, i: (bi, 0, CB_V), pipeline_mode=once),
                  pl.BlockSpec((None, s, LANES), lambda bi, i: (bi, 0, CB_IK), pipeline_mode=once),
                  pl.BlockSpec(bias_p.shape, lambda bi, i: (0, 0, 0, 0), pipeline_mode=once),
                  pl.BlockSpec(memory_space=pltpu.SMEM)],
        out_specs=pl.BlockSpec((None, tq, A_W), lambda bi, i: (bi, i, 0)),
        out_shape=jax.ShapeDtypeStruct((b, s, A_W), BF16),
        scratch_shapes=[pltpu.VMEM((s, A_W), BF16), pltpu.VMEM((s, A_W), BF16), pltpu.VMEM((s, LANES), BF16),
                        pltpu.VMEM((H_A, tq, LANES), BF16), pltpu.VMEM((H_IDX, tq, LANES), BF16),
                        pltpu.VMEM((H_IDX, tq, LANES), F32),
                        pltpu.VMEM((nc, tq, LANES), I32), pltpu.VMEM((nc, LANES, tq), I32),
                        pltpu.VMEM((nc, tq, LANES), F32),
                        pltpu.VMEM((H_A, nc, tq, LANES), F32),
                        pltpu.VMEM((H_A, tq, LANES), F32), pltpu.VMEM((H_A, tq, LANES), F32),
                        pltpu.VMEM((H_A, tq, LANES), F32)],
        compiler_params=_cparams(("parallel", "arbitrary")),
        name="dsa_prompt",
    )(proj, proj, proj, proj, proj, proj, bias_p, rel_bias)


def _dsa_sample_select_kernel(pps, n_pages, topk, pt_ref, iq_ref, ikiw_ref, *rest):
    page_refs = rest[:pps]
    mb_ref = rest[pps]
    iq2, wb, sc = rest[pps + 1:]
    j = pl.program_id(1)
    t = iq_ref.shape[0]
    shape = (t, LANES)
    nc = n_pages + 1

    @pl.when(j == 0)
    def _():
        iq = iq_ref[...]
        ikiw = ikiw_ref[...]
        for h in range(H_IDX):
            iq2[h * t:(h + 1) * t, :] = iq[:, h * D_IDX:(h + 1) * D_IDX].astype(BF16)
            wb[h * t:(h + 1) * t, :] = jnp.broadcast_to(ikiw[:, D_IDX + h:D_IDX + h + 1], shape)

    def scores(s):
        n = s.shape[1]
        s = jnp.maximum(s, 0.0) * jnp.tile(wb[...], (1, n // LANES))
        return jnp.sum(s.reshape(H_IDX, t, n), axis=0) * IDX_SCALE

    ikt = jnp.concatenate([r[...] for r in page_refs], axis=1).astype(BF16)
    keys = _score_key(scores(jnp.dot(iq2[...], ikt, preferred_element_type=F32)))
    for u in range(pps):
        sc[j * pps + u] = keys[:, u * LANES:(u + 1) * LANES]

    @pl.when(j == pl.num_programs(1) - 1)
    def _():
        row = lax.broadcasted_iota(I32, shape, 0)
        col = lax.broadcasted_iota(I32, shape, 1)
        ik_new = jnp.concatenate([ikiw_ref[:, 0:D_IDX], jnp.zeros((LANES - t, D_IDX), F32)], axis=0)
        s_new = lax.dot_general(iq2[...], ik_new.astype(BF16), (((1,), (1,)), ((), ())), preferred_element_type=F32)
        s_new = jnp.where(col <= row, scores(s_new), NEG)
        sc[n_pages] = jnp.where(col < t, _score_key(s_new), INT_MIN)
        kf = float(topk)
        thr = _kth_largest(sc, nc, kf, shape)

        def causal(kc):
            return (kc < n_pages) | (col <= row)

        _select_mask(sc, mb_ref, nc, kf, thr, shape, causal)


def _dsa_sample_attn_kernel(pps, n_pages, pt_ref, q_ref, kn_ref, vn_ref, mb_ref, mbn_ref, bias_ref, *rest):
    k_refs = rest[:pps]
    v_refs = rest[pps:2 * pps]
    o_ref = rest[2 * pps]
    qbd, acc, m_s, l_s = rest[2 * pps + 1:]
    j = pl.program_id(1)
    t = q_ref.shape[0]
    rows = H_A * t

    def blockdiag(x):
        r = lax.broadcasted_iota(I32, (rows, A_W), 0) // t
        c = lax.broadcasted_iota(I32, (rows, A_W), 1) // HD_A
        return jnp.where(r == c, jnp.tile(x, (H_A, 1)), 0.0)

    @pl.when(j == 0)
    def _():
        qbd[...] = blockdiag(q_ref[...] * (HD_A ** -0.5)).astype(BF16)
        m_s[...] = jnp.full(m_s.shape, NEG, F32)
        l_s[...] = jnp.zeros(l_s.shape, F32)
        acc[...] = jnp.zeros(acc.shape, F32)

    def step(lg, pv_fn):
        m_old = m_s[...]
        m_new = jnp.maximum(m_old, jnp.max(lg, axis=1, keepdims=True))
        p = jnp.exp(lg - m_new)
        alpha = jnp.exp(m_old - m_new)
        l_s[...] = alpha * l_s[...] + jnp.sum(p, axis=1, keepdims=True)
        acc[...] = alpha * acc[...] + pv_fn(p.astype(BF16))
        m_s[...] = m_new

    kt = jnp.concatenate([r[...].reshape(A_W, PAGE) for r in k_refs], axis=1).astype(BF16)
    vt = jnp.concatenate([r[...].reshape(A_W, PAGE) for r in v_refs], axis=1).astype(BF16)
    last = j == pl.num_programs(1) - 1
    bias = jnp.concatenate([bias_ref[0]] * (pps - 1) + [jnp.where(last, bias_ref[1], bias_ref[0])], axis=1)
    mbc = jnp.concatenate([mb_ref[u] for u in range(pps)], axis=1)
    lg = jnp.dot(qbd[...], kt, preferred_element_type=F32) + bias + jnp.tile(mbc, (H_A, 1))
    step(lg, lambda p: lax.dot_general(p, vt, (((1,), (1,)), ((), ())), preferred_element_type=F32))

    @pl.when(last)
    def _():
        pad = jnp.zeros((LANES - t, A_W), F32)
        kn = jnp.concatenate([kn_ref[...], pad], axis=0).astype(BF16)
        vn = jnp.concatenate([vn_ref[...], pad], axis=0).astype(BF16)
        lgn = lax.dot_general(qbd[...], kn, (((1,), (1,)), ((), ())), preferred_element_type=F32)
        step(lgn + bias_ref[2] + jnp.tile(mbn_ref[0], (H_A, 1)),
             lambda p: jnp.dot(p, vn, preferred_element_type=F32))
        res = blockdiag_sum(acc[...] / l_s[...], t)
        o_ref[...] = res.astype(o_ref.dtype)


def blockdiag_sum(x, t):
    c = lax.broadcasted_iota(I32, (t, A_W), 1) // HD_A
    out = jnp.zeros((t, A_W), F32)
    for h in range(H_A):
        out = out + jnp.where(c == h, x[h * t:(h + 1) * t, :], 0.0)
    return out


def dsa_sample(proj, layer, cache_k, cache_v, cache_idx_k, page_table, bias_s, pps_sel=32, pps_att=16):
    db, t, _ = proj.shape
    n_pages = page_table.shape[1]
    nc = n_pages + 1
    topk = min(TOPK_MAX, (n_pages * PAGE + t) // 4)
    n_pool = cache_k.shape[1]
    ck = jnp.transpose(cache_k, (0, 1, 3, 4, 2))
    cv = jnp.transpose(cache_v, (0, 1, 3, 4, 2))
    cik = jnp.swapaxes(cache_idx_k, 2, 3)
    pt = page_table.reshape(-1)

    def page_spec(dims, pps, u):
        zeros = (0,) * len(dims)
        return pl.BlockSpec((None, None) + dims,
                            lambda b, j, ptr: (layer, ptr[b * n_pages + j * pps + u]) + zeros)

    sel = pl.pallas_call(
        functools.partial(_dsa_sample_select_kernel, pps_sel, n_pages, topk),
        grid_spec=pltpu.PrefetchScalarGridSpec(
            num_scalar_prefetch=1,
            grid=(db, n_pages // pps_sel),
            in_specs=[pl.BlockSpec((None, t, A_W), lambda b, j, ptr: (b, 0, CB_IQ)),
                      pl.BlockSpec((None, t, LANES), lambda b, j, ptr: (b, 0, CB_IK))]
                     + [page_spec((D_IDX, PAGE), pps_sel, u) for u in range(pps_sel)],
            out_specs=pl.BlockSpec((None, nc, t, LANES), lambda b, j, ptr: (b, 0, 0, 0)),
            scratch_shapes=[pltpu.VMEM((H_IDX * t, D_IDX), BF16), pltpu.VMEM((H_IDX * t, LANES), F32),
                            pltpu.VMEM((nc, t, LANES), I32)]),
        out_shape=jax.ShapeDtypeStruct((db, nc, t, LANES), F32),
        compiler_params=_cparams(("parallel", "arbitrary")),
        name="dsa_sample_select",
    )(pt, proj, proj, *([cik] * pps_sel))

    return pl.pallas_call(
        functools.partial(_dsa_sample_attn_kernel, pps_att, n_pages),
        grid_spec=pltpu.PrefetchScalarGridSpec(
            num_scalar_prefetch=1,
            grid=(db, n_pages // pps_att),
            in_specs=[pl.BlockSpec((None, t, A_W), lambda b, j, ptr: (b, 0, CB_Q)),
                      pl.BlockSpec((None, t, A_W), lambda b, j, ptr: (b, 0, CB_K)),
                      pl.BlockSpec((None, t, A_W), lambda b, j, ptr: (b, 0, CB_V)),
                      pl.BlockSpec((None, pps_att, t, LANES), lambda b, j, ptr: (b, j, 0, 0)),
                      pl.BlockSpec((None, 1, t, LANES), lambda b, j, ptr: (b, n_pages, 0, 0)),
                      pl.BlockSpec(bias_s.shape, lambda b, j, ptr: (0, 0, 0))]
                     + [page_spec((H_A, HD_A, PAGE), pps_att, u) for u in range(pps_att)] * 2,
            out_specs=pl.BlockSpec((None, t, A_W), lambda b, j, ptr: (b, 0, 0)),
            scratch_shapes=[pltpu.VMEM((H_A * t, A_W), BF16), pltpu.VMEM((H_A * t, A_W), F32),
                            pltpu.VMEM((H_A * t, 1), F32), pltpu.VMEM((H_A * t, 1), F32)]),
        out_shape=jax.ShapeDtypeStruct((db, t, A_W), BF16),
        compiler_params=_cparams(("parallel", "arbitrary")),
        name="dsa_sample_attn",
    )(pt, proj, proj, proj, sel, sel, bias_s, *([ck] * pps_att), *([cv] * pps_att))


def _mm(a, b):
    return jnp.dot(a.astype(BF16), b.astype(BF16), preferred_element_type=F32)


def _mm_nt(a, b):
    return lax.dot_general(a.astype(BF16), b.astype(BF16), (((1,), (1,)), ((), ())), preferred_element_type=F32)


def _split3(x):
    hi = x.astype(BF16)
    r1 = x - hi.astype(F32)
    mid = r1.astype(BF16)
    lo = (r1 - mid.astype(F32)).astype(BF16)
    return hi, mid, lo


def _rwkv_kernel(has_state, t_valid, r_ref, k_ref, v_ref, wa_ref, g_ref, sr_ref, sk_ref, sv_ref, swa_ref, sg_ref,
                 mr_ref, mk_ref, mv_ref, mwa_ref, mg_ref, w2_ref, a2_ref, g2_ref, vec_ref, *rest):
    if has_state:
        s0_ref, y_ref, so_ref, pr, pk, pv, pwa, pg, st = rest
    else:
        y_ref, so_ref, pr, pk, pv, pwa, pg, st = rest
    c = pl.program_id(1)
    C = RW_C
    nb, tb = r_ref.shape[0], r_ref.shape[1]
    n_pairs = H_B // 2
    zero_blk = jnp.zeros((HD_B, HD_B), F32)

    @pl.when(c == 0)
    def _():
        if has_state:
            for bi in range(nb):
                for p in range(n_pairs):
                    st[bi * n_pairs + p] = jnp.concatenate(
                        [jnp.concatenate([s0_ref[bi, 2 * p], zero_blk], axis=1),
                         jnp.concatenate([zero_blk, s0_ref[bi, 2 * p + 1]], axis=1)], axis=0)
        else:
            st[...] = jnp.zeros(st.shape, F32)

    def shifted(bi, p_ref, prev_scr, shift_ref, mu_ref):
        p = p_ref[bi]
        if tb < C:
            p = jnp.concatenate([p, jnp.zeros((C - tb, p.shape[1]), F32)], axis=0)
        prev = jnp.where(c == 0, shift_ref[bi], prev_scr[bi])
        rowi = lax.broadcasted_iota(I32, p.shape, 0)
        ps = jnp.where(rowi == 0, prev, pltpu.roll(p, 1, 0))
        prev_scr[bi] = p[C - 1:C, :]
        return p + (ps - p) * mu_ref[...]

    vec = vec_ref[...]
    w0, a0, kkp, kap, rkp, lnw, lnb = (vec[n:n + 1, :] for n in range(7))
    r2 = lax.broadcasted_iota(I32, (2 * C, 2 * C), 0)
    c2 = lax.broadcasted_iota(I32, (2 * C, 2 * C), 1)
    same_head = (r2 // C) == (c2 // C)
    head_ones = same_head.astype(BF16)
    strict = same_head & ((c2 % C) < (r2 % C))
    incl = same_head & ((c2 % C) <= (r2 % C))

    def pairs(x):
        return jnp.concatenate([x[:, p * LANES:(p + 1) * LANES] for p in range(n_pairs)], axis=0)

    def unpairs(x):
        return jnp.concatenate([x[p * C:(p + 1) * C] for p in range(n_pairs)], axis=1)

    def head_sum(x):
        xs = pairs(x)
        hi = xs.astype(BF16)
        mid = (xs - hi.astype(F32)).astype(BF16)
        return unpairs(jnp.dot(hi, head_ones, preferred_element_type=F32)
                       + jnp.dot(mid, head_ones, preferred_element_type=F32))

    rc = lax.broadcasted_iota(I32, (C, C), 0)
    cc = lax.broadcasted_iota(I32, (C, C), 1)
    tril = (cc <= rc).astype(BF16)

    def prep(bi):
        xr = shifted(bi, r_ref, pr, sr_ref, mr_ref)
        xk = shifted(bi, k_ref, pk, sk_ref, mk_ref)
        xv = shifted(bi, v_ref, pv, sv_ref, mv_ref)
        xwa = shifted(bi, wa_ref, pwa, swa_ref, mwa_ref)
        xg = shifted(bi, g_ref, pg, sg_ref, mg_ref)
        zw = w0 + _mm(jnp.tanh(xwa[:, 0:W_LORA]), w2_ref[...])
        w_log = -(jnp.maximum(-zw, 0.0) + jnp.log(1.0 + jnp.exp(-jnp.abs(zw)))) - 0.5
        ld = -jnp.exp(w_log)
        za = a0 + _mm(xwa[:, W_LORA:W_LORA + A_LORA], a2_ref[...])
        a = 1.0 / (1.0 + jnp.exp(-za))
        g = _mm(1.0 / (1.0 + jnp.exp(-xg)), g2_ref[...])
        kk = xk * kkp
        kk = kk / jnp.maximum(jnp.sqrt(head_sum(kk * kk)), 1e-12)
        k2 = xk * (1.0 + (a - 1.0) * kap)
        bonus_w = head_sum(xr * k2 * rkp)
        if t_valid < C:
            valid = lax.broadcasted_iota(I32, (C, B_W), 0) < t_valid
            ld = jnp.where(valid, ld, 0.0)
            xv = jnp.where(valid, xv, 0.0)
            kk = jnp.where(valid, kk, 0.0)
            k2m = jnp.where(valid, k2, 0.0)
        else:
            k2m = k2
        hi, mid, lo = _split3(ld)
        lgc = (jnp.dot(tril, hi, preferred_element_type=F32) + jnp.dot(tril, mid, preferred_element_type=F32)
               + jnp.dot(tril, lo, preferred_element_type=F32))
        lg_end = lgc[C - 1:C, :]
        ginv = jnp.exp(-lgc)
        e_end = jnp.exp(lg_end - lgc)
        kb_ = kk * a
        return dict(al=kk * jnp.exp(lgc - ld), be=kb_ * ginv, kt=k2m * ginv, rt=xr * jnp.exp(lgc),
                    bp=kb_ * e_end, kp=k2m * e_end, xv=xv, g_end=jnp.exp(lg_end), bonus_w=bonus_w, g=g)

    seqs = [prep(bi) for bi in range(nb)]
    lane_lo = lax.broadcasted_iota(I32, (C, LANES), 1) < HD_B

    def stack(name, q):
        bi, p = divmod(q, n_pairs)
        xp = seqs[bi][name][:, p * LANES:(p + 1) * LANES]
        return jnp.concatenate([jnp.where(lane_lo, xp, 0.0), jnp.where(lane_lo, 0.0, xp)], axis=0)

    P = range(nb * n_pairs)
    bf = lambda x: x.astype(BF16)
    lhs = [bf(jnp.concatenate([stack('al', p), stack('rt', p)], axis=0)) for p in P]
    vs = [bf(stack('xv', p)) for p in P]
    g3 = [_mm_nt(lhs[p], jnp.concatenate([bf(stack('be', p)), bf(stack('kt', p)), bf(st[p])], axis=0)) for p in P]
    gb = [g3[p][:, 0:2 * C] for p in P]
    gk = [g3[p][:, 2 * C:4 * C] for p in P]
    p0 = [g3[p][:, 4 * C:] for p in P]
    qn = [bf(jnp.where(strict, -gb[p][0:2 * C], 0.0)) for p in P]
    u = [p0[p][0:2 * C] + _mm(jnp.where(strict, gk[p][0:2 * C], 0.0), vs[p]) for p in P]
    n_levels = int(math.log2(C))
    for m in range(n_levels):
        if m + 1 < n_levels:
            qu = [_mm(qn[p], jnp.concatenate([qn[p], bf(u[p])], axis=1)) for p in P]
            u = [u[p] + qu[p][:, 2 * C:] for p in P]
            qn = [bf(qu[p][:, 0:2 * C]) for p in P]
        else:
            u = [u[p] + _mm(qn[p], u[p]) for p in P]
    m2 = [jnp.concatenate([bf(jnp.where(incl, -gb[p][2 * C:], 0.0)), bf(jnp.where(incl, gk[p][2 * C:], 0.0))], axis=1)
          for p in P]
    uv = [jnp.concatenate([bf(u[p]), vs[p]], axis=0) for p in P]
    yp = [p0[p][2 * C:] + _mm(m2[p], uv[p]) for p in P]
    upd = [lax.dot_general(jnp.concatenate([bf(-u[p]), vs[p]], axis=0),
                           bf(jnp.concatenate([stack('bp', p), stack('kp', p)], axis=0)),
                           (((0,), (0,)), ((), ())), preferred_element_type=F32) for p in P]
    for q in P:
        bi, p = divmod(q, n_pairs)
        st[q] = st[q] * seqs[bi]['g_end'][:, p * LANES:(p + 1) * LANES] + upd[q]
    for bi in range(nb):
        sq = seqs[bi]
        y = jnp.concatenate([yp[q][0:C] + yp[q][C:2 * C] for q in range(bi * n_pairs, (bi + 1) * n_pairs)],
                            axis=1)
        mu = head_sum(y) * (1.0 / HD_B)
        dy = y - mu
        var = head_sum(dy * dy) * (1.0 / HD_B)
        yn = dy * lax.rsqrt(var + GN_EPS) * lnw + lnb
        out = (yn + sq['bonus_w'] * sq['xv']) * sq['g']
        y_ref[bi] = out[0:tb].astype(y_ref.dtype)

    @pl.when(c == pl.num_programs(1) - 1)
    def _():
        for bi in range(nb):
            for p in range(n_pairs):
                s = st[bi * n_pairs + p]
                so_ref[bi, 2 * p] = s[0:HD_B, 0:HD_B]
                so_ref[bi, 2 * p + 1] = s[HD_B:, HD_B:]


def rwkv_mix(proj, shift_full, mu_full, w2, a2, g2p, vecs, s0):
    b, t, _ = proj.shape
    tb = min(t, RW_C)
    nch = t // tb
    has_state = s0 is not None
    cbs = [(B_W, OFF_R // B_W), (B_W, OFF_R // B_W + 1), (B_W, OFF_R // B_W + 2), (LANES, CB_WA), (256, CB_G)]

    nb = RW_SEQS_PER_STEP
    assert b % nb == 0
    in_specs = ([pl.BlockSpec((nb, tb, w), functools.partial(lambda cb, bi, c: (bi, c, cb), cb)) for w, cb in cbs]
                + [pl.BlockSpec((nb, 1, w), functools.partial(lambda cb, bi, c: (bi, 0, cb), cb)) for w, cb in cbs]
                + [pl.BlockSpec((1, w), functools.partial(lambda cb, bi, c: (0, cb), cb)) for w, cb in cbs]
                + [pl.BlockSpec(w2.shape, lambda bi, c: (0, 0)),
                   pl.BlockSpec(a2.shape, lambda bi, c: (0, 0)),
                   pl.BlockSpec(g2p.shape, lambda bi, c: (0, 0)),
                   pl.BlockSpec(vecs.shape, lambda bi, c: (0, 0))])
    args = [proj] * 5 + [shift_full] * 5 + [mu_full] * 5 + [w2, a2, g2p, vecs]
    if has_state:
        in_specs.append(pl.BlockSpec((nb, H_B, HD_B, HD_B), lambda bi, c: (bi, 0, 0, 0)))
        args.append(s0)
    return pl.pallas_call(
        functools.partial(_rwkv_kernel, has_state, tb),
        grid=(b // nb, nch),
        in_specs=in_specs,
        out_specs=[pl.BlockSpec((nb, tb, B_W), lambda bi, c: (bi, c, 0)),
                   pl.BlockSpec((nb, H_B, HD_B, HD_B), lambda bi, c: (bi, 0, 0, 0))],
        out_shape=[jax.ShapeDtypeStruct((b, t, B_W), BF16),
                   jax.ShapeDtypeStruct((b, H_B, HD_B, HD_B), F32)],
        scratch_shapes=[pltpu.VMEM((nb, 1, B_W), F32)] * 3
                       + [pltpu.VMEM((nb, 1, LANES), F32), pltpu.VMEM((nb, 1, 256), F32),
                          pltpu.VMEM((nb * H_B // 2, 2 * HD_B, 2 * HD_B), F32)],
        compiler_params=_cparams(("parallel", "arbitrary")),
        name="rwkv_mix",
    )(*args)


def _gelu(x):
    return 0.5 * x * (1.0 + jnp.tanh(math.sqrt(2.0 / math.pi) * (x + 0.044715 * (x * x * x))))


def _gmlp_kernel(pu_ref, pv_ref, lnw_ref, lnb_ref, ws_ref, bst_ref, y_ref, gv_ref):
    tc = pu_ref.shape[0]
    u = _gelu(pu_ref[...])
    vf = _gelu(pv_ref[...])
    mu = jnp.mean(vf, axis=1, keepdims=True)
    var = jnp.mean(jnp.square(vf - mu), axis=1, keepdims=True)
    v = (vf - mu) * lax.rsqrt(var + 1e-5) * lnw_ref[...] + lnb_ref[...]
    gv_ref[...] = v
    ri = lax.broadcasted_iota(I32, (CHUNK, CHUNK), 0)
    ci = lax.broadcasted_iota(I32, (CHUNK, CHUNK), 1)
    if tc < CHUNK:
        v = jnp.concatenate([v, jnp.zeros((CHUNK - tc, C_W), F32)], axis=0)
    vb = v.astype(BF16)
    for g in range(G_C):
        sl = slice(g * CD, (g + 1) * CD)
        wm = jnp.where(ci <= ri, ws_ref[g], 0.0).astype(BF16)
        sp = jnp.dot(wm, vb[:, sl], preferred_element_type=F32)[0:tc] + bst_ref[0:tc, g:g + 1]
        y_ref[:, sl] = (u[:, sl] * sp).astype(y_ref.dtype)


def gmlp_mix(proj, lnw, lnb, ws, bst):
    b, t, _ = proj.shape
    tc = min(t, CHUNK)
    return pl.pallas_call(
        _gmlp_kernel,
        grid=(b, t // tc),
        in_specs=[pl.BlockSpec((None, tc, C_W), lambda bi, c: (bi, c, CB_U)),
                  pl.BlockSpec((None, tc, C_W), lambda bi, c: (bi, c, CB_PV)),
                  pl.BlockSpec((1, C_W), lambda bi, c: (0, 0)),
                  pl.BlockSpec((1, C_W), lambda bi, c: (0, 0)),
                  pl.BlockSpec(ws.shape, lambda bi, c: (0, 0, 0)),
                  pl.BlockSpec(bst.shape, lambda bi, c: (0, 0))],
        out_specs=[pl.BlockSpec((None, tc, C_W), lambda bi, c: (bi, c, 0)),
                   pl.BlockSpec((None, tc, C_W), lambda bi, c: (bi, c, 0))],
        out_shape=[jax.ShapeDtypeStruct((b, t, C_W), BF16), jax.ShapeDtypeStruct((b, t, C_W), F32)],
        compiler_params=_cparams(("parallel", "parallel")),
        name="gmlp_mix",
    )(proj, proj, lnw.reshape(1, C_W), lnb.reshape(1, C_W), ws, bst)


def _layer(x, lw, attend, shift_full, s0, tm):
    b, t, d = x.shape
    x2 = x.reshape(b * t, d)
    l = lw['layer']
    proj = norm_matmul(x2, lw['norm_mix'], lw['w_in'], l, min(2 * tm, b * t), 512).reshape(b, t, PROJ_P)
    ya = attend(proj)
    yb, wkv = rwkv_mix(proj, shift_full, lw['mu_full'], lw['w2'], lw['a2'], lw['g2p'], lw['vecs'], s0)
    yc, gv = gmlp_mix(proj, lw['gmlp_ln_w'], lw['gmlp_ln_b'], lw['gmlp_ws'], lw['gmlp_bst'])
    h = out_proj(x2, ya.reshape(b * t, A_W), yb.reshape(b * t, B_W), yc.reshape(b * t, C_W), lw['w_out'], l, tm)
    y = ffn(h, lw['norm_ffn'], lw['ffn_gate'], lw['ffn_up'], lw['ffn_down'], l, tm, 512)
    k = proj[:, :, A_W:2 * A_W].reshape(b, t, H_A, HD_A)
    v = proj[:, :, 2 * A_W:3 * A_W].reshape(b, t, H_A, HD_A)
    ik = proj[:, :, OFF_IK:OFF_IK + D_IDX]
    shift_new = _rwkv_cols(proj[:, t - 1])
    return y.reshape(b, t, d), k, v, ik, shift_new, wkv, gv


def kernel(x_prompt, x_sample, cache_k, cache_v, cache_idx_k, state_rwkv_shift, state_rwkv_wkv, page_table, rel_bias, norm_mix, w_in, rwkv_mu, rwkv_w0, rwkv_w2, rwkv_a0, rwkv_a2, rwkv_g2, rwkv_kk, rwkv_ka, rwkv_rk, rwkv_ln_w, rwkv_ln_b, gmlp_ln_w, gmlp_ln_b, gmlp_ws, gmlp_b, w_out, norm_ffn, ffn_gate, ffn_up, ffn_down, norm_final):
    depth = w_in.shape[0]
    nbp, seq, d = x_prompt.shape
    db, dec_seq, _ = x_sample.shape
    assert d == D_MODEL and seq % 512 == 0 and dec_seq == SUBLANES and w_in.shape[2] == ORIG_GM + 2 * C_W

    w_in_p = _pad_cols(jnp.swapaxes(w_in, 1, 2), axis=1).astype(BF16)
    w_out_b = w_out.astype(BF16)
    wg_b, wu_b, wd_b = ffn_gate.astype(BF16), ffn_up.astype(BF16), ffn_down.astype(BF16)
    mu_full = _pad_rwkv_cols(rwkv_mu)
    g2p = jnp.concatenate([rwkv_g2, jnp.zeros((depth, 256 - G_LORA, B_W), F32)], axis=1)
    vecs = jnp.stack([rwkv_w0, rwkv_a0, rwkv_kk, rwkv_ka, rwkv_rk, rwkv_ln_w, rwkv_ln_b,
                      jnp.zeros_like(rwkv_w0)], axis=1)
    bst = jnp.swapaxes(gmlp_b, 1, 2)
    shift_s = _pad_rwkv_cols(state_rwkv_shift)[:, :, None, :]
    shift_p = jnp.zeros((nbp, 1, PROJ_P), F32)
    bias_p, bias_s = bias_tiles(rel_bias, dec_seq)

    xp, xs = x_prompt, x_sample
    outs = [[] for _ in range(11)]
    for l in range(depth):
        lw = {'layer': l, 'norm_mix': norm_mix[l], 'w_in': w_in_p, 'mu_full': mu_full[l:l + 1], 'w2': rwkv_w2[l],
              'a2': rwkv_a2[l], 'g2p': g2p[l], 'vecs': vecs[l], 'gmlp_ln_w': gmlp_ln_w[l],
              'gmlp_ln_b': gmlp_ln_b[l], 'gmlp_ws': gmlp_ws[l], 'gmlp_bst': bst[l], 'w_out': w_out_b,
              'norm_ffn': norm_ffn[l], 'ffn_gate': wg_b, 'ffn_up': wu_b, 'ffn_down': wd_b}
        attend_p = functools.partial(dsa_prompt, bias_p=bias_p, rel_bias=rel_bias)
        xp, kp, vp, ikp, shp, wkvp, _ = _layer(xp, lw, attend_p, shift_p, None, 512)
        attend_s = functools.partial(dsa_sample, layer=l, cache_k=cache_k, cache_v=cache_v,
                                     cache_idx_k=cache_idx_k, page_table=page_table, bias_s=bias_s)
        xs, k_s, v_s, ik_s, sh_s, wkv_s, gv_s = _layer(xs, lw, attend_s, shift_s[l], state_rwkv_wkv[l],
                                                       db * dec_seq)
        for lst, val in zip(outs, (kp, vp, ikp, shp, wkvp, k_s, v_s, ik_s, sh_s, wkv_s, gv_s)):
            lst.append(val)
    y_prompt = rms_final(xp.reshape(nbp * seq, d), norm_final, 512).reshape(nbp, seq, d)
    y_sample = rms_final(xs.reshape(db * dec_seq, d), norm_final, db * dec_seq).reshape(db, dec_seq, d)
    return (y_prompt, y_sample) + tuple(jnp.stack(o) for o in outs)
```

```python
import functools
import math

import numpy as np
import jax
import jax.numpy as jnp
from jax import lax
from jax.experimental import pallas as pl
from jax.experimental.pallas import tpu as pltpu

F32 = jnp.float32
BF16 = jnp.bfloat16
I32 = jnp.int32

LANES = 128
SUBLANES = 8
VMEM_LIMIT = 56 * 1024 * 1024

D_MODEL = 2048
HD_A = 64
H_A = 8
A_W = H_A * HD_A
H_IDX = 8
D_IDX = 64
IDX_SCALE = (D_IDX ** -0.5) * (H_IDX ** -0.5)
TOPK_MAX = 256
N_BUCKETS = 32
MAX_DISTANCE = 128
HD_B = 64
B_W = 1024
H_B = B_W // HD_B
W_LORA = 64
A_LORA = 64
G_LORA = 160
RWKV_PROJ_W = 3 * B_W + W_LORA + A_LORA + G_LORA
GN_EPS = 64e-5
C_W = 512
CHUNK = 128
CD = 64
G_C = C_W // CD
D_FF = 5632
NORM_EPS = 1e-6
NEG = -1e30
PAGE = 128
INT_MIN = -(2 ** 31)

ORIG_IK = 3 * A_W + H_IDX * D_IDX
ORIG_R = ORIG_IK + D_IDX + H_IDX
ORIG_GM = ORIG_R + RWKV_PROJ_W
OFF_R = 2048
OFF_IK = OFF_R + 3 * B_W
OFF_WA = OFF_IK + LANES
OFF_G = OFF_WA + W_LORA + A_LORA
OFF_GM = OFF_G + 256
PROJ_P = OFF_GM + 2 * C_W
CB_Q, CB_K, CB_V, CB_IQ = 0, 1, 2, 3
CB_IK = OFF_IK // LANES
CB_WA = OFF_WA // LANES
CB_G = OFF_G // 256
CB_U = OFF_GM // C_W
CB_PV = CB_U + 1
RW_C = 64
RW_SEQS_PER_STEP = 4
assert OFF_R % B_W == 0 and OFF_G % 256 == 0 and OFF_GM % C_W == 0


def _cparams(sem):
    return pltpu.CompilerParams(dimension_semantics=sem, vmem_limit_bytes=VMEM_LIMIT)


def _pad_cols(a, axis=-1):
    a = jnp.moveaxis(a, axis, -1)
    z = lambda n: jnp.zeros(a.shape[:-1] + (n,), a.dtype)
    lora = ORIG_R + 3 * B_W
    out = jnp.concatenate([a[..., :ORIG_IK], a[..., ORIG_R:lora], a[..., ORIG_IK:ORIG_R], z(OFF_WA - OFF_IK - D_IDX - H_IDX),
                           a[..., lora:ORIG_GM], z(OFF_GM - OFF_G - G_LORA), a[..., ORIG_GM:]], axis=-1)
    return jnp.moveaxis(out, -1, axis)


def _pad_rwkv_cols(a):
    z = lambda n: jnp.zeros(a.shape[:-1] + (n,), a.dtype)
    return jnp.concatenate([z(OFF_R), a[..., :3 * B_W], z(LANES), a[..., 3 * B_W:], z(PROJ_P - OFF_G - G_LORA)], axis=-1)


def _rwkv_cols(p):
    return jnp.concatenate([p[..., OFF_R:OFF_R + 3 * B_W], p[..., OFF_WA:OFF_WA + RWKV_PROJ_W - 3 * B_W]], axis=-1)


def _norm_matmul_kernel(x_ref, g_ref, w_ref, o_ref, xn_ref):
    @pl.when(pl.program_id(1) == 0)
    def _():
        x = x_ref[...]
        y = x * lax.rsqrt(jnp.mean(x * x, axis=-1, keepdims=True) + NORM_EPS)
        xn_ref[...] = (y * g_ref[...]).astype(BF16)

    o_ref[...] = lax.dot_general(xn_ref[...], w_ref[...], (((1,), (1,)), ((), ())), preferred_element_type=F32)


def norm_matmul(x, g, wt, layer, tm, tn):
    m, d = x.shape
    n = wt.shape[1]
    return pl.pallas_call(
        _norm_matmul_kernel,
        grid=(m // tm, n // tn),
        in_specs=[pl.BlockSpec((tm, d), lambda i, j: (i, 0)),
                  pl.BlockSpec((1, d), lambda i, j: (0, 0)),
                  pl.BlockSpec((None, tn, d), lambda i, j: (layer, j, 0))],
        out_specs=pl.BlockSpec((tm, tn), lambda i, j: (i, j)),
        out_shape=jax.ShapeDtypeStruct((m, n), F32),
        scratch_shapes=[pltpu.VMEM((tm, d), BF16)],
        compiler_params=_cparams(("parallel", "arbitrary")),
        name="norm_matmul",
    )(x, g.reshape(1, d), wt)


def _out_proj_kernel(x_ref, ya_ref, yb_ref, yc_ref, w_ref, o_ref):
    acc = jnp.dot(ya_ref[...], w_ref[0:A_W, :], preferred_element_type=F32)
    acc += jnp.dot(yb_ref[...], w_ref[A_W:A_W + B_W, :], preferred_element_type=F32)
    acc += jnp.dot(yc_ref[...], w_ref[A_W + B_W:, :], preferred_element_type=F32)
    o_ref[...] = x_ref[...] + acc


def out_proj(x, ya, yb, yc, w, layer, tm):
    m, d = x.shape
    return pl.pallas_call(
        _out_proj_kernel,
        grid=(m // tm,),
        in_specs=[pl.BlockSpec((tm, d), lambda i: (i, 0)),
                  pl.BlockSpec((tm, A_W), lambda i: (i, 0)),
                  pl.BlockSpec((tm, B_W), lambda i: (i, 0)),
                  pl.BlockSpec((tm, C_W), lambda i: (i, 0)),
                  pl.BlockSpec((None,) + w.shape[1:], lambda i: (layer, 0, 0))],
        out_specs=pl.BlockSpec((tm, d), lambda i: (i, 0)),
        out_shape=jax.ShapeDtypeStruct((m, d), F32),
        compiler_params=_cparams(("parallel",)),
        name="out_proj",
    )(x, ya, yb, yc, w)


def _ffn_kernel(h_ref, g_ref, wg_ref, wu_ref, wd_ref, o_ref, hn_ref, acc_ref):
    f = pl.program_id(1)

    @pl.when(f == 0)
    def _():
        x = h_ref[...]
        y = x * lax.rsqrt(jnp.mean(x * x, axis=-1, keepdims=True) + NORM_EPS)
        hn_ref[...] = (y * g_ref[...]).astype(BF16)
        acc_ref[...] = jnp.zeros_like(acc_ref)

    hn = hn_ref[...]
    gate = jnp.dot(hn, wg_ref[...], preferred_element_type=F32)
    up = jnp.dot(hn, wu_ref[...], preferred_element_type=F32)
    act = (gate / (1.0 + jnp.exp(-gate))) * up
    acc_ref[...] += jnp.dot(act.astype(BF16), wd_ref[...], preferred_element_type=F32)

    @pl.when(f == pl.num_programs(1) - 1)
    def _():
        o_ref[...] = h_ref[...] + acc_ref[...]


def ffn(h, g, wg, wu, wd, layer, tm, tf):
    m, d = h.shape
    nf = wg.shape[2]
    return pl.pallas_call(
        _ffn_kernel,
        grid=(m // tm, nf // tf),
        in_specs=[pl.BlockSpec((tm, d), lambda i, f: (i, 0)),
                  pl.BlockSpec((1, d), lambda i, f: (0, 0)),
                  pl.BlockSpec((None, d, tf), lambda i, f: (layer, 0, f)),
                  pl.BlockSpec((None, d, tf), lambda i, f: (layer, 0, f)),
                  pl.BlockSpec((None, tf, d), lambda i, f: (layer, f, 0))],
        out_specs=pl.BlockSpec((tm, d), lambda i, f: (i, 0)),
        out_shape=jax.ShapeDtypeStruct((m, d), F32),
        scratch_shapes=[pltpu.VMEM((tm, d), BF16), pltpu.VMEM((tm, d), F32)],
        compiler_params=_cparams(("parallel", "arbitrary")),
        name="ffn",
    )(h, g.reshape(1, d), wg, wu, wd)


def _rms_kernel(x_ref, g_ref, o_ref):
    x = x_ref[...]
    o_ref[...] = x * lax.rsqrt(jnp.mean(x * x, axis=-1, keepdims=True) + NORM_EPS) * g_ref[...]


def rms_final(x, g, tm):
    m, d = x.shape
    return pl.pallas_call(
        _rms_kernel,
        grid=(m // tm,),
        in_specs=[pl.BlockSpec((tm, d), lambda i: (i, 0)), pl.BlockSpec((1, d), lambda i: (0, 0))],
        out_specs=pl.BlockSpec((tm, d), lambda i: (i, 0)),
        out_shape=jax.ShapeDtypeStruct((m, d), F32),
        compiler_params=_cparams(("parallel",)),
        name="rms_final",
    )(x, g.reshape(1, d))


def _bucket_np(dist):
    max_exact = N_BUCKETS // 2
    n = np.maximum(dist, 0)
    nf = np.maximum(n, 1).astype(np.float64)
    large = max_exact + (np.log(nf / max_exact) / math.log(MAX_DISTANCE / max_exact)
                         * (N_BUCKETS - max_exact)).astype(np.int32)
    return np.where(n < max_exact, n, np.minimum(large, N_BUCKETS - 1)).astype(np.int32)


def _bias_tiles_kernel(rb_ref, bp_ref, bs_ref, op_ref, os_ref):
    for d in range(2):
        bk = bp_ref[d]
        for h in range(H_A):
            t = jnp.zeros(bk.shape, F32)
            for b in range(N_BUCKETS):
                t = jnp.where(bk == b, rb_ref[b, h], t)
            op_ref[d, h] = t
    for d in range(3):
        bk = bs_ref[d]
        for h in range(H_A):
            t = jnp.zeros(bk.shape, F32)
            for b in range(N_BUCKETS):
                t = jnp.where(bk == b, rb_ref[b, h], t)
            os_ref[d, h * SUBLANES:(h + 1) * SUBLANES, :] = t


def bias_tiles(rel_bias, dec_seq):
    r = np.arange(LANES)[:, None]
    c = np.arange(LANES)[None, :]
    bp = np.stack([_bucket_np(r - c), _bucket_np(LANES + r - c)])
    t = np.arange(dec_seq)[:, None]
    bs = np.stack([_bucket_np(np.full((dec_seq, LANES), 4 * MAX_DISTANCE)),
                   _bucket_np(PAGE + t - c), _bucket_np(t - c)])
    vm = pl.BlockSpec(memory_space=pltpu.VMEM)
    return pl.pallas_call(
        _bias_tiles_kernel,
        in_specs=[pl.BlockSpec(memory_space=pltpu.SMEM), vm, vm],
        out_specs=[vm, vm],
        out_shape=[jax.ShapeDtypeStruct((2, H_A, LANES, LANES), F32),
                   jax.ShapeDtypeStruct((3, H_A * dec_seq, LANES), F32)],
        name="bias_tiles",
    )(rel_bias, jnp.asarray(bp), jnp.asarray(bs))


def _score_key(s):
    s = jnp.where(s == 0.0, 0.0, s)
    bits = lax.bitcast_convert_type(s, I32)
    return jnp.where(bits < 0, bits ^ 0x7FFFFFFF, bits)


class _Pairs:
    def __init__(self, n_pairs):
        self.n_pairs = n_pairs


def _chunk_loop(n_chunks, body, init):
    if isinstance(n_chunks, _Pairs):
        return lax.fori_loop(0, n_chunks.n_pairs, lambda m, c: body(2 * m + 1, body(2 * m, c)), init)
    return lax.fori_loop(0, n_chunks, body, init)


def _kth_largest(sc_ref, n_chunks, kf, shape):
    def count_ge(cand):
        if isinstance(n_chunks, int):
            ge = (sc_ref[...] >= cand[None]).astype(F32)
            q = -(-n_chunks // 4)
            a = ((jnp.sum(ge[0:q], axis=0) + jnp.sum(ge[q:2 * q], axis=0))
                 + (jnp.sum(ge[2 * q:3 * q], axis=0) + jnp.sum(ge[3 * q:], axis=0)))
        else:
            parts = []
            for r0 in range(0, shape[0], LANES):
                cs = cand[r0:r0 + LANES]

                def body(kc, a, r0=r0, cs=cs):
                    return a + (sc_ref[kc, r0:r0 + LANES, :] >= cs).astype(F32)

                parts.append(_chunk_loop(n_chunks, body, jnp.zeros((LANES, LANES), F32)))
            a = parts[0] if len(parts) == 1 else jnp.concatenate(parts, axis=0)
        return jnp.sum(a, axis=1, keepdims=True)

    zero = jnp.zeros(shape, I32)
    t0 = jnp.where(count_ge(zero) >= kf, zero, jnp.full(shape, INT_MIN, I32))

    def bit_body(it, t):
        cand = t | jnp.left_shift(jnp.int32(1), 30 - it)
        return jnp.where(count_ge(cand) >= kf, cand, t)

    return lax.fori_loop(0, 31, bit_body, t0)


def _kth_largest_t(sct_ref, n_chunks, kf, tq):
    groups = LANES // SUBLANES

    def count_ge(cand):
        def body(kc, a):
            ge = sct_ref[kc].reshape(groups, SUBLANES, tq) >= cand[None]
            return a + jnp.sum(ge.astype(F32), axis=0)

        a = _chunk_loop(n_chunks, body, jnp.zeros((SUBLANES, tq), F32))
        return jnp.sum(a, axis=0, keepdims=True)

    zero = jnp.zeros((SUBLANES, tq), I32)
    t0 = jnp.where(count_ge(zero) >= kf, zero, jnp.full((SUBLANES, tq), INT_MIN, I32))

    def bit_body(it, t):
        cand = t | jnp.left_shift(jnp.int32(1), 30 - it)
        return jnp.where(count_ge(cand) >= kf, cand, t)

    t = lax.fori_loop(0, 31, bit_body, t0)
    hi = jnp.transpose(jnp.broadcast_to((t[0:1] >> 16).astype(F32), (LANES, tq)))
    lo = jnp.transpose(jnp.broadcast_to((t[0:1] & 0xFFFF).astype(F32), (LANES, tq)))
    return (hi.astype(I32) << 16) | lo.astype(I32)


def _select_mask(sc_ref, mb_ref, n_chunks, kf, thr, shape, causal_fn):
    def cnt_body(kc, carry):
        key = sc_ref[kc]
        return carry[0] + (key > thr).astype(F32), carry[1] + (key >= thr).astype(F32)

    zeros = jnp.zeros(shape, F32)
    if isinstance(n_chunks, int):
        keys = sc_ref[...]
        cgt = jnp.sum((keys > thr[None]).astype(F32), axis=0)
        cge = jnp.sum((keys >= thr[None]).astype(F32), axis=0)
    else:
        cgt, cge = _chunk_loop(n_chunks, cnt_body, (zeros, zeros))
    need = kf - jnp.sum(cgt, axis=1, keepdims=True)
    cge = jnp.sum(cge, axis=1, keepdims=True)

    def exact_k():
        def body(kc, carry):
            mb_ref[kc] = jnp.where((sc_ref[kc] >= thr) & causal_fn(kc), 0.0, NEG)
            return carry

        if isinstance(n_chunks, int):
            lax.fori_loop(0, n_chunks, body, 0, unroll=8)
        else:
            _chunk_loop(n_chunks, body, 0)

    def with_ties():
        rr = lax.broadcasted_iota(I32, (LANES, LANES), 0)
        cc = lax.broadcasted_iota(I32, (LANES, LANES), 1)
        tri = (rr <= cc).astype(BF16)

        def body(kc, offs):
            key = sc_ref[kc]
            eq = key == thr
            pre = jnp.dot(eq.astype(BF16), tri, preferred_element_type=F32) + offs
            sel = (key > thr) | (eq & (pre <= need))
            mb_ref[kc] = jnp.where(sel & causal_fn(kc), 0.0, NEG)
            return jnp.broadcast_to(pre[:, LANES - 1:LANES], shape)

        _chunk_loop(n_chunks, body, zeros)

    lax.cond(jnp.max(jnp.abs(cge - kf)) > 0.5, with_ties, exact_k)


def _dsa_prompt_kernel(topk, q_ref, iq_ref, iw_ref, k_ref, v_ref, ik_ref, bias_ref, rb_ref, o_ref,
                       kb, vb, ikd, qm, iqm, iwb, sc, sct, mb, lgs, acc, m_s, l_s):
    i = pl.program_id(1)
    tq = q_ref.shape[0]
    shape = (tq, LANES)
    nt = (((1,), (1,)), ((), ()))

    @pl.when(i == 0)
    def _():
        kb[...] = k_ref[...].astype(BF16)
        vb[...] = v_ref[...].astype(BF16)
        ik = ik_ref[:, 0:D_IDX].astype(BF16)
        ikd[...] = jnp.concatenate([ik, ik], axis=1)

    row = lax.broadcasted_iota(I32, shape, 0) + i * tq
    col = lax.broadcasted_iota(I32, shape, 1)
    lane_lo = col < HD_A
    sub = tq // LANES
    n_chunks = _Pairs((i + 1) * (sub // 2))
    iw = iw_ref[:, D_IDX:D_IDX + H_IDX]
    for h in range(H_A):
        ps = slice((h // 2) * LANES, (h // 2 + 1) * LANES)
        keep = lane_lo if h % 2 == 0 else jnp.logical_not(lane_lo)
        qm[h] = jnp.where(keep, q_ref[:, ps] * (HD_A ** -0.5), 0.0).astype(BF16)
        iqm[h] = jnp.where(keep, iq_ref[:, ps], 0.0).astype(BF16)
        iwb[h] = jnp.broadcast_to(iw[:, h:h + 1], shape)

    def causal(kc):
        return (col + kc * LANES) <= row

    def score_body(kc, carry):
        ikc = ikd[pl.ds(pl.multiple_of(kc * LANES, LANES), LANES), :]
        ss = [lax.dot_general(iqm[h], ikc, nt, preferred_element_type=F32) for h in range(H_IDX)]
        ws = [iwb[h] * jnp.maximum(ss[h], 0.0) for h in range(H_IDX)]
        tot = ((ws[0] + ws[1]) + (ws[2] + ws[3])) + ((ws[4] + ws[5]) + (ws[6] + ws[7]))
        tot = jnp.where(causal(kc), tot * IDX_SCALE, NEG)
        sc[kc] = _score_key(tot)
        sct[kc] = _score_key(jnp.transpose(tot))
        return carry

    _chunk_loop(n_chunks, score_body, 0)
    kf = float(topk)
    thr = _kth_largest_t(sct, n_chunks, kf, tq)
    _select_mask(sc, mb, n_chunks, kf, thr, shape, causal)

    m_s[...] = jnp.full(m_s.shape, NEG, F32)

    def bias_tile(h, kc):
        tiles = []
        for s in range(sub):
            d = i * sub + s - kc
            tiles.append(jnp.where(d == 0, bias_ref[0, h],
                                   jnp.where(d == 1, bias_ref[1, h], rb_ref[N_BUCKETS - 1, h])))
        return jnp.concatenate(tiles, axis=0)

    def logit_body(m, carry):
        off = pl.multiple_of(m * 2 * LANES, 2 * LANES)
        mb2 = jnp.concatenate([mb[2 * m], mb[2 * m + 1]], axis=1)
        for h in range(H_A):
            ps = slice((h // 2) * LANES, (h // 2 + 1) * LANES)
            lg = lax.dot_general(qm[h], kb[pl.ds(off, 2 * LANES), ps], nt, preferred_element_type=F32)
            lg = lg + jnp.concatenate([bias_tile(h, 2 * m), bias_tile(h, 2 * m + 1)], axis=1) + mb2
            lgs[h, 2 * m] = lg[:, 0:LANES]
            lgs[h, 2 * m + 1] = lg[:, LANES:]
            m_s[h] = jnp.maximum(m_s[h], jnp.maximum(lg[:, 0:LANES], lg[:, LANES:]))
        return carry

    lax.fori_loop(0, (i + 1) * (sub // 2), logit_body, 0)
    for h in range(H_A):
        m_s[h] = jnp.broadcast_to(jnp.max(m_s[h], axis=1, keepdims=True), shape)
    l_s[...] = jnp.zeros(l_s.shape, F32)
    acc[...] = jnp.zeros(acc.shape, F32)

    def attn_body(kc, carry):
        off = pl.multiple_of(kc * LANES, LANES)
        for h in range(H_A):
            ps = slice((h // 2) * LANES, (h // 2 + 1) * LANES)
            p = jnp.exp(lgs[h, kc] - m_s[h])
            l_s[h] += p
            acc[h] += jnp.dot(p.astype(BF16), vb[pl.ds(off, LANES), ps], preferred_element_type=F32)
        return carry

    _chunk_loop(n_chunks, attn_body, 0)
    for j in range(H_A // 2):
        lo = acc[2 * j] / jnp.sum(l_s[2 * j], axis=1, keepdims=True)
        hi = acc[2 * j + 1] / jnp.sum(l_s[2 * j + 1], axis=1, keepdims=True)
        o_ref[:, j * LANES:(j + 1) * LANES] = jnp.where(lane_lo, lo, hi).astype(o_ref.dtype)


def dsa_prompt(proj, bias_p, rel_bias, tq=2 * LANES):
    b, s, _ = proj.shape
    assert tq % (2 * LANES) == 0 and s % tq == 0
    once = pl.Buffered(1)
    topk = min(TOPK_MAX, s // 4)
    nc = s // LANES
    kern = functools.partial(_dsa_prompt_kernel, topk)
    return pl.pallas_call(
        kern,
        grid=(b, s // tq),
        in_specs=[pl.BlockSpec((None, tq, A_W), lambda bi, i: (bi, i, CB_Q)),
                  pl.BlockSpec((None, tq, A_W), lambda bi, i: (bi, i, CB_IQ)),
                  pl.BlockSpec((None, tq, LANES), lambda bi, i: (bi, i, CB_IK)),
                  pl.BlockSpec((None, s, A_W), lambda bi, i: (bi, 0, CB_K), pipeline_mode=once),
                  pl.BlockSpec((None, s, A_W), lambda bi, i: (bi, 0, CB_V), pipeline_mode=once),
                  pl.BlockSpec((None, s, LANES), lambda bi, i: (bi, 0, CB_IK), pipeline_mode=once),
                  pl.BlockSpec(bias_p.shape, lambda bi, i: (0, 0, 0, 0), pipeline_mode=once),
                  pl.BlockSpec(memory_space=pltpu.SMEM)],
        out_specs=pl.BlockSpec((None, tq, A_W), lambda bi, i: (bi, i, 0)),
        out_shape=jax.ShapeDtypeStruct((b, s, A_W), BF16),
        scratch_shapes=[pltpu.VMEM((s, A_W), BF16), pltpu.VMEM((s, A_W), BF16), pltpu.VMEM((s, LANES), BF16),
                        pltpu.VMEM((H_A, tq, LANES), BF16), pltpu.VMEM((H_IDX, tq, LANES), BF16),
                        pltpu.VMEM((H_IDX, tq, LANES), F32),
                        pltpu.VMEM((nc, tq, LANES), I32), pltpu.VMEM((nc, LANES, tq), I32),
                        pltpu.VMEM((nc, tq, LANES), F32),
                        pltpu.VMEM((H_A, nc, tq, LANES), F32),
                        pltpu.VMEM((H_A, tq, LANES), F32), pltpu.VMEM((H_A, tq, LANES), F32),
                        pltpu.VMEM((H_A, tq, LANES), F32)],
        compiler_params=_cparams(("parallel", "arbitrary")),
        name="dsa_prompt",
    )(proj, proj, proj, proj, proj, proj, bias_p, rel_bias)


def _dsa_sample_select_kernel(pps, n_pages, topk, pt_ref, iq_ref, ikiw_ref, *rest):
    page_refs = rest[:pps]
    mb_ref = rest[pps]
    iq2, wb, sc = rest[pps + 1:]
    j = pl.program_id(1)
    t = iq_ref.shape[0]
    shape = (t, LANES)
    nc = n_pages + 1

    @pl.when(j == 0)
    def _():
        iq = iq_ref[...]
        ikiw = ikiw_ref[...]
        for h in range(H_IDX):
            iq2[h * t:(h + 1) * t, :] = iq[:, h * D_IDX:(h + 1) * D_IDX].astype(BF16)
            wb[h * t:(h + 1) * t, :] = jnp.broadcast_to(ikiw[:, D_IDX + h:D_IDX + h + 1], shape)

    def scores(s):
        n = s.shape[1]
        s = jnp.maximum(s, 0.0) * jnp.tile(wb[...], (1, n // LANES))
        return jnp.sum(s.reshape(H_IDX, t, n), axis=0) * IDX_SCALE

    ikt = jnp.concatenate([r[...] for r in page_refs], axis=1).astype(BF16)
    keys = _score_key(scores(jnp.dot(iq2[...], ikt, preferred_element_type=F32)))
    for u in range(pps):
        sc[j * pps + u] = keys[:, u * LANES:(u + 1) * LANES]

    @pl.when(j == pl.num_programs(1) - 1)
    def _():
        row = lax.broadcasted_iota(I32, shape, 0)
        col = lax.broadcasted_iota(I32, shape, 1)
        ik_new = jnp.concatenate([ikiw_ref[:, 0:D_IDX], jnp.zeros((LANES - t, D_IDX), F32)], axis=0)
        s_new = lax.dot_general(iq2[...], ik_new.astype(BF16), (((1,), (1,)), ((), ())), preferred_element_type=F32)
        s_new = jnp.where(col <= row, scores(s_new), NEG)
        sc[n_pages] = jnp.where(col < t, _score_key(s_new), INT_MIN)
        kf = float(topk)
        thr = _kth_largest(sc, nc, kf, shape)

        def causal(kc):
            return (kc < n_pages) | (col <= row)

        _select_mask(sc, mb_ref, nc, kf, thr, shape, causal)


def _dsa_sample_attn_kernel(pps, n_pages, pt_ref, q_ref, kn_ref, vn_ref, mb_ref, mbn_ref, bias_ref, *rest):
    k_refs = rest[:pps]
    v_refs = rest[pps:2 * pps]
    o_ref = rest[2 * pps]
    qbd, acc, m_s, l_s = rest[2 * pps + 1:]
    j = pl.program_id(1)
    t = q_ref.shape[0]
    rows = H_A * t

    def blockdiag(x):
        r = lax.broadcasted_iota(I32, (rows, A_W), 0) // t
        c = lax.broadcasted_iota(I32, (rows, A_W), 1) // HD_A
        return jnp.where(r == c, jnp.tile(x, (H_A, 1)), 0.0)

    @pl.when(j == 0)
    def _():
        qbd[...] = blockdiag(q_ref[...] * (HD_A ** -0.5)).astype(BF16)
        m_s[...] = jnp.full(m_s.shape, NEG, F32)
        l_s[...] = jnp.zeros(l_s.shape, F32)
        acc[...] = jnp.zeros(acc.shape, F32)

    def step(lg, pv_fn):
        m_old = m_s[...]
        m_new = jnp.maximum(m_old, jnp.max(lg, axis=1, keepdims=True))
        p = jnp.exp(lg - m_new)
        alpha = jnp.exp(m_old - m_new)
        l_s[...] = alpha * l_s[...] + jnp.sum(p, axis=1, keepdims=True)
        acc[...] = alpha * acc[...] + pv_fn(p.astype(BF16))
        m_s[...] = m_new

    kt = jnp.concatenate([r[...].reshape(A_W, PAGE) for r in k_refs], axis=1).astype(BF16)
    vt = jnp.concatenate([r[...].reshape(A_W, PAGE) for r in v_refs], axis=1).astype(BF16)
    last = j == pl.num_programs(1) - 1
    bias = jnp.concatenate([bias_ref[0]] * (pps - 1) + [jnp.where(last, bias_ref[1], bias_ref[0])], axis=1)
    mbc = jnp.concatenate([mb_ref[u] for u in range(pps)], axis=1)
    lg = jnp.dot(qbd[...], kt, preferred_element_type=F32) + bias + jnp.tile(mbc, (H_A, 1))
    step(lg, lambda p: lax.dot_general(p, vt, (((1,), (1,)), ((), ())), preferred_element_type=F32))

    @pl.when(last)
    def _():
        pad = jnp.zeros((LANES - t, A_W), F32)
        kn = jnp.concatenate([kn_ref[...], pad], axis=0).astype(BF16)
        vn = jnp.concatenate([vn_ref[...], pad], axis=0).astype(BF16)
        lgn = lax.dot_general(qbd[...], kn, (((1,), (1,)), ((), ())), preferred_element_type=F32)
        step(lgn + bias_ref[2] + jnp.tile(mbn_ref[0], (H_A, 1)),
             lambda p: jnp.dot(p, vn, preferred_element_type=F32))
        res = blockdiag_sum(acc[...] / l_s[...], t)
        o_ref[...] = res.astype(o_ref.dtype)


def blockdiag_sum(x, t):
    c = lax.broadcasted_iota(I32, (t, A_W), 1) // HD_A
    out = jnp.zeros((t, A_W), F32)
    for h in range(H_A):
        out = out + jnp.where(c == h, x[h * t:(h + 1) * t, :], 0.0)
    return out


def dsa_sample(proj, layer, cache_k, cache_v, cache_idx_k, page_table, bias_s, pps_sel=32, pps_att=16):
    db, t, _ = proj.shape
    n_pages = page_table.shape[1]
    nc = n_pages + 1
    topk = min(TOPK_MAX, (n_pages * PAGE + t) // 4)
    n_pool = cache_k.shape[1]
    ck = jnp.transpose(cache_k, (0, 1, 3, 4, 2))
    cv = jnp.transpose(cache_v, (0, 1, 3, 4, 2))
    cik = jnp.swapaxes(cache_idx_k, 2, 3)
    pt = page_table.reshape(-1)

    def page_spec(dims, pps, u):
        zeros = (0,) * len(dims)
        return pl.BlockSpec((None, None) + dims,
                            lambda b, j, ptr: (layer, ptr[b * n_pages + j * pps + u]) + zeros)

    sel = pl.pallas_call(
        functools.partial(_dsa_sample_select_kernel, pps_sel, n_pages, topk),
        grid_spec=pltpu.PrefetchScalarGridSpec(
            num_scalar_prefetch=1,
            grid=(db, n_pages // pps_sel),
            in_specs=[pl.BlockSpec((None, t, A_W), lambda b, j, ptr: (b, 0, CB_IQ)),
                      pl.BlockSpec((None, t, LANES), lambda b, j, ptr: (b, 0, CB_IK))]
                     + [page_spec((D_IDX, PAGE), pps_sel, u) for u in range(pps_sel)],
            out_specs=pl.BlockSpec((None, nc, t, LANES), lambda b, j, ptr: (b, 0, 0, 0)),
            scratch_shapes=[pltpu.VMEM((H_IDX * t, D_IDX), BF16), pltpu.VMEM((H_IDX * t, LANES), F32),
                            pltpu.VMEM((nc, t, LANES), I32)]),
        out_shape=jax.ShapeDtypeStruct((db, nc, t, LANES), F32),
        compiler_params=_cparams(("parallel", "arbitrary")),
        name="dsa_sample_select",
    )(pt, proj, proj, *([cik] * pps_sel))

    return pl.pallas_call(
        functools.partial(_dsa_sample_attn_kernel, pps_att, n_pages),
        grid_spec=pltpu.PrefetchScalarGridSpec(
            num_scalar_prefetch=1,
            grid=(db, n_pages // pps_att),
            in_specs=[pl.BlockSpec((None, t, A_W), lambda b, j, ptr: (b, 0, CB_Q)),
                      pl.BlockSpec((None, t, A_W), lambda b, j, ptr: (b, 0, CB_K)),
                      pl.BlockSpec((None, t, A_W), lambda b, j, ptr: (b, 0, CB_V)),
                      pl.BlockSpec((None, pps_att, t, LANES), lambda b, j, ptr: (b, j, 0, 0)),
                      pl.BlockSpec((None, 1, t, LANES), lambda b, j, ptr: (b, n_pages, 0, 0)),
                      pl.BlockSpec(bias_s.shape, lambda b, j, ptr: (0, 0, 0))]
                     + [page_spec((H_A, HD_A, PAGE), pps_att, u) for u in range(pps_att)] * 2,
            out_specs=pl.BlockSpec((None, t, A_W), lambda b, j, ptr: (b, 0, 0)),
            scratch_shapes=[pltpu.VMEM((H_A * t, A_W), BF16), pltpu.VMEM((H_A * t, A_W), F32),
                            pltpu.VMEM((H_A * t, 1), F32), pltpu.VMEM((H_A * t, 1), F32)]),
        out_shape=jax.ShapeDtypeStruct((db, t, A_W), BF16),
        compiler_params=_cparams(("parallel", "arbitrary")),
        name="dsa_sample_attn",
    )(pt, proj, proj, proj, sel, sel, bias_s, *([ck] * pps_att), *([cv] * pps_att))


def _mm(a, b):
    return jnp.dot(a.astype(BF16), b.astype(BF16), preferred_element_type=F32)


def _mm_nt(a, b):
    return lax.dot_general(a.astype(BF16), b.astype(BF16), (((1,), (1,)), ((), ())), preferred_element_type=F32)


def _split3(x):
    hi = x.astype(BF16)
    r1 = x - hi.astype(F32)
    mid = r1.astype(BF16)
    lo = (r1 - mid.astype(F32)).astype(BF16)
    return hi, mid, lo


def _rwkv_kernel(has_state, t_valid, r_ref, k_ref, v_ref, wa_ref, g_ref, sr_ref, sk_ref, sv_ref, swa_ref, sg_ref,
                 mr_ref, mk_ref, mv_ref, mwa_ref, mg_ref, w2_ref, a2_ref, g2_ref, vec_ref, *rest):
    if has_state:
        s0_ref, y_ref, so_ref, pr, pk, pv, pwa, pg, st = rest
    else:
        y_ref, so_ref, pr, pk, pv, pwa, pg, st = rest
    c = pl.program_id(1)
    C = RW_C
    nb, tb = r_ref.shape[0], r_ref.shape[1]
    n_pairs = H_B // 2
    zero_blk = jnp.zeros((HD_B, HD_B), F32)

    @pl.when(c == 0)
    def _():
        if has_state:
            for bi in range(nb):
                for p in range(n_pairs):
                    st[bi * n_pairs + p] = jnp.concatenate(
                        [jnp.concatenate([s0_ref[bi, 2 * p], zero_blk], axis=1),
                         jnp.concatenate([zero_blk, s0_ref[bi, 2 * p + 1]], axis=1)], axis=0)
        else:
            st[...] = jnp.zeros(st.shape, F32)

    def shifted(bi, p_ref, prev_scr, shift_ref, mu_ref):
        p = p_ref[bi]
        if tb < C:
            p = jnp.concatenate([p, jnp.zeros((C - tb, p.shape[1]), F32)], axis=0)
        prev = jnp.where(c == 0, shift_ref[bi], prev_scr[bi])
        rowi = lax.broadcasted_iota(I32, p.shape, 0)
        ps = jnp.where(rowi == 0, prev, pltpu.roll(p, 1, 0))
        prev_scr[bi] = p[C - 1:C, :]
        return p + (ps - p) * mu_ref[...]

    vec = vec_ref[...]
    w0, a0, kkp, kap, rkp, lnw, lnb = (vec[n:n + 1, :] for n in range(7))
    r2 = lax.broadcasted_iota(I32, (2 * C, 2 * C), 0)
    c2 = lax.broadcasted_iota(I32, (2 * C, 2 * C), 1)
    same_head = (r2 // C) == (c2 // C)
    head_ones = same_head.astype(BF16)
    strict = same_head & ((c2 % C) < (r2 % C))
    incl = same_head & ((c2 % C) <= (r2 % C))

    def pairs(x):
        return jnp.concatenate([x[:, p * LANES:(p + 1) * LANES] for p in range(n_pairs)], axis=0)

    def unpairs(x):
        return jnp.concatenate([x[p * C:(p + 1) * C] for p in range(n_pairs)], axis=1)

    def head_sum(x):
        xs = pairs(x)
        hi = xs.astype(BF16)
        mid = (xs - hi.astype(F32)).astype(BF16)
        return unpairs(jnp.dot(hi, head_ones, preferred_element_type=F32)
                       + jnp.dot(mid, head_ones, preferred_element_type=F32))

    rc = lax.broadcasted_iota(I32, (C, C), 0)
    cc = lax.broadcasted_iota(I32, (C, C), 1)
    tril = (cc <= rc).astype(BF16)

    def prep(bi):
        xr = shifted(bi, r_ref, pr, sr_ref, mr_ref)
        xk = shifted(bi, k_ref, pk, sk_ref, mk_ref)
        xv = shifted(bi, v_ref, pv, sv_ref, mv_ref)
        xwa = shifted(bi, wa_ref, pwa, swa_ref, mwa_ref)
        xg = shifted(bi, g_ref, pg, sg_ref, mg_ref)
        zw = w0 + _mm(jnp.tanh(xwa[:, 0:W_LORA]), w2_ref[...])
        w_log = -(jnp.maximum(-zw, 0.0) + jnp.log(1.0 + jnp.exp(-jnp.abs(zw)))) - 0.5
        ld = -jnp.exp(w_log)
        za = a0 + _mm(xwa[:, W_LORA:W_LORA + A_LORA], a2_ref[...])
        a = 1.0 / (1.0 + jnp.exp(-za))
        g = _mm(1.0 / (1.0 + jnp.exp(-xg)), g2_ref[...])
        kk = xk * kkp
        kk = kk / jnp.maximum(jnp.sqrt(head_sum(kk * kk)), 1e-12)
        k2 = xk * (1.0 + (a - 1.0) * kap)
        bonus_w = head_sum(xr * k2 * rkp)
        if t_valid < C:
            valid = lax.broadcasted_iota(I32, (C, B_W), 0) < t_valid
            ld = jnp.where(valid, ld, 0.0)
            xv = jnp.where(valid, xv, 0.0)
            kk = jnp.where(valid, kk, 0.0)
            k2m = jnp.where(valid, k2, 0.0)
        else:
            k2m = k2
        hi, mid, lo = _split3(ld)
        lgc = (jnp.dot(tril, hi, preferred_element_type=F32) + jnp.dot(tril, mid, preferred_element_type=F32)
               + jnp.dot(tril, lo, preferred_element_type=F32))
        lg_end = lgc[C - 1:C, :]
        ginv = jnp.exp(-lgc)
        e_end = jnp.exp(lg_end - lgc)
        kb_ = kk * a
        return dict(al=kk * jnp.exp(lgc - ld), be=kb_ * ginv, kt=k2m * ginv, rt=xr * jnp.exp(lgc),
                    bp=kb_ * e_end, kp=k2m * e_end, xv=xv, g_end=jnp.exp(lg_end), bonus_w=bonus_w, g=g)

    seqs = [prep(bi) for bi in range(nb)]
    lane_lo = lax.broadcasted_iota(I32, (C, LANES), 1) < HD_B

    def stack(name, q):
        bi, p = divmod(q, n_pairs)
        xp = seqs[bi][name][:, p * LANES:(p + 1) * LANES]
        return jnp.concatenate([jnp.where(lane_lo, xp, 0.0), jnp.where(lane_lo, 0.0, xp)], axis=0)

    P = range(nb * n_pairs)
    bf = lambda x: x.astype(BF16)
    lhs = [bf(jnp.concatenate([stack('al', p), stack('rt', p)], axis=0)) for p in P]
    vs = [bf(stack('xv', p)) for p in P]
    g3 = [_mm_nt(lhs[p], jnp.concatenate([bf(stack('be', p)), bf(stack('kt', p)), bf(st[p])], axis=0)) for p in P]
    gb = [g3[p][:, 0:2 * C] for p in P]
    gk = [g3[p][:, 2 * C:4 * C] for p in P]
    p0 = [g3[p][:, 4 * C:] for p in P]
    qn = [bf(jnp.where(strict, -gb[p][0:2 * C], 0.0)) for p in P]
    u = [p0[p][0:2 * C] + _mm(jnp.where(strict, gk[p][0:2 * C], 0.0), vs[p]) for p in P]
    n_levels = int(math.log2(C))
    for m in range(n_levels):
        if m + 1 < n_levels:
            qu = [_mm(qn[p], jnp.concatenate([qn[p], bf(u[p])], axis=1)) for p in P]
            u = [u[p] + qu[p][:, 2 * C:] for p in P]
            qn = [bf(qu[p][:, 0:2 * C]) for p in P]
        else:
            u = [u[p] + _mm(qn[p], u[p]) for p in P]
    m2 = [jnp.concatenate([bf(jnp.where(incl, -gb[p][2 * C:], 0.0)), bf(jnp.where(incl, gk[p][2 * C:], 0.0))], axis=1)
          for p in P]
    uv = [jnp.concatenate([bf(u[p]), vs[p]], axis=0) for p in P]
    yp = [p0[p][2 * C:] + _mm(m2[p], uv[p]) for p in P]
    upd = [lax.dot_general(jnp.concatenate([bf(-u[p]), vs[p]], axis=0),
                           bf(jnp.concatenate([stack('bp', p), stack('kp', p)], axis=0)),
                           (((0,), (0,)), ((), ())), preferred_element_type=F32) for p in P]
    for q in P:
        bi, p = divmod(q, n_pairs)
        st[q] = st[q] * seqs[bi]['g_end'][:, p * LANES:(p + 1) * LANES] + upd[q]
    for bi in range(nb):
        sq = seqs[bi]
        y = jnp.concatenate([yp[q][0:C] + yp[q][C:2 * C] for q in range(bi * n_pairs, (bi + 1) * n_pairs)],
                            axis=1)
        mu = head_sum(y) * (1.0 / HD_B)
        dy = y - mu
        var = head_sum(dy * dy) * (1.0 / HD_B)
        yn = dy * lax.rsqrt(var + GN_EPS) * lnw + lnb
        out = (yn + sq['bonus_w'] * sq['xv']) * sq['g']
        y_ref[bi] = out[0:tb].astype(y_ref.dtype)

    @pl.when(c == pl.num_programs(1) - 1)
    def _():
        for bi in range(nb):
            for p in range(n_pairs):
                s = st[bi * n_pairs + p]
                so_ref[bi, 2 * p] = s[0:HD_B, 0:HD_B]
                so_ref[bi, 2 * p + 1] = s[HD_B:, HD_B:]


def rwkv_mix(proj, shift_full, mu_full, w2, a2, g2p, vecs, s0):
    b, t, _ = proj.shape
    tb = min(t, RW_C)
    nch = t // tb
    has_state = s0 is not None
    cbs = [(B_W, OFF_R // B_W), (B_W, OFF_R // B_W + 1), (B_W, OFF_R // B_W + 2), (LANES, CB_WA), (256, CB_G)]

    nb = RW_SEQS_PER_STEP
    assert b % nb == 0
    in_specs = ([pl.BlockSpec((nb, tb, w), functools.partial(lambda cb, bi, c: (bi, c, cb), cb)) for w, cb in cbs]
                + [pl.BlockSpec((nb, 1, w), functools.partial(lambda cb, bi, c: (bi, 0, cb), cb)) for w, cb in cbs]
                + [pl.BlockSpec((1, w), functools.partial(lambda cb, bi, c: (0, cb), cb)) for w, cb in cbs]
                + [pl.BlockSpec(w2.shape, lambda bi, c: (0, 0)),
                   pl.BlockSpec(a2.shape, lambda bi, c: (0, 0)),
                   pl.BlockSpec(g2p.shape, lambda bi, c: (0, 0)),
                   pl.BlockSpec(vecs.shape, lambda bi, c: (0, 0))])
    args = [proj] * 5 + [shift_full] * 5 + [mu_full] * 5 + [w2, a2, g2p, vecs]
    if has_state:
        in_specs.append(pl.BlockSpec((nb, H_B, HD_B, HD_B), lambda bi, c: (bi, 0, 0, 0)))
        args.append(s0)
    return pl.pallas_call(
        functools.partial(_rwkv_kernel, has_state, tb),
        grid=(b // nb, nch),
        in_specs=in_specs,
        out_specs=[pl.BlockSpec((nb, tb, B_W), lambda bi, c: (bi, c, 0)),
                   pl.BlockSpec((nb, H_B, HD_B, HD_B), lambda bi, c: (bi, 0, 0, 0))],
        out_shape=[jax.ShapeDtypeStruct((b, t, B_W), BF16),
                   jax.ShapeDtypeStruct((b, H_B, HD_B, HD_B), F32)],
        scratch_shapes=[pltpu.VMEM((nb, 1, B_W), F32)] * 3
                       + [pltpu.VMEM((nb, 1, LANES), F32), pltpu.VMEM((nb, 1, 256), F32),
                          pltpu.VMEM((nb * H_B // 2, 2 * HD_B, 2 * HD_B), F32)],
        compiler_params=_cparams(("parallel", "arbitrary")),
        name="rwkv_mix",
    )(*args)


def _gelu(x):
    return 0.5 * x * (1.0 + jnp.tanh(math.sqrt(2.0 / math.pi) * (x + 0.044715 * (x * x * x))))


def _gmlp_kernel(pu_ref, pv_ref, lnw_ref, lnb_ref, ws_ref, bst_ref, y_ref, gv_ref):
    tc = pu_ref.shape[0]
    u = _gelu(pu_ref[...])
    vf = _gelu(pv_ref[...])
    mu = jnp.mean(vf, axis=1, keepdims=True)
    var = jnp.mean(jnp.square(vf - mu), axis=1, keepdims=True)
    v = (vf - mu) * lax.rsqrt(var + 1e-5) * lnw_ref[...] + lnb_ref[...]
    gv_ref[...] = v
    ri = lax.broadcasted_iota(I32, (CHUNK, CHUNK), 0)
    ci = lax.broadcasted_iota(I32, (CHUNK, CHUNK), 1)
    if tc < CHUNK:
        v = jnp.concatenate([v, jnp.zeros((CHUNK - tc, C_W), F32)], axis=0)
    vb = v.astype(BF16)
    for g in range(G_C):
        sl = slice(g * CD, (g + 1) * CD)
        wm = jnp.where(ci <= ri, ws_ref[g], 0.0).astype(BF16)
        sp = jnp.dot(wm, vb[:, sl], preferred_element_type=F32)[0:tc] + bst_ref[0:tc, g:g + 1]
        y_ref[:, sl] = (u[:, sl] * sp).astype(y_ref.dtype)


def gmlp_mix(proj, lnw, lnb, ws, bst):
    b, t, _ = proj.shape
    tc = min(t, CHUNK)
    return pl.pallas_call(
        _gmlp_kernel,
        grid=(b, t // tc),
        in_specs=[pl.BlockSpec((None, tc, C_W), lambda bi, c: (bi, c, CB_U)),
                  pl.BlockSpec((None, tc, C_W), lambda bi, c: (bi, c, CB_PV)),
                  pl.BlockSpec((1, C_W), lambda bi, c: (0, 0)),
                  pl.BlockSpec((1, C_W), lambda bi, c: (0, 0)),
                  pl.BlockSpec(ws.shape, lambda bi, c: (0, 0, 0)),
                  pl.BlockSpec(bst.shape, lambda bi, c: (0, 0))],
        out_specs=[pl.BlockSpec((None, tc, C_W), lambda bi, c: (bi, c, 0)),
                   pl.BlockSpec((None, tc, C_W), lambda bi, c: (bi, c, 0))],
        out_shape=[jax.ShapeDtypeStruct((b, t, C_W), BF16), jax.ShapeDtypeStruct((b, t, C_W), F32)],
        compiler_params=_cparams(("parallel", "parallel")),
        name="gmlp_mix",
    )(proj, proj, lnw.reshape(1, C_W), lnb.reshape(1, C_W), ws, bst)


def _layer(x, lw, attend, shift_full, s0, tm):
    b, t, d = x.shape
    x2 = x.reshape(b * t, d)
    l = lw['layer']
    proj = norm_matmul(x2, lw['norm_mix'], lw['w_in'], l, min(2 * tm, b * t), 512).reshape(b, t, PROJ_P)
    ya = attend(proj)
    yb, wkv = rwkv_mix(proj, shift_full, lw['mu_full'], lw['w2'], lw['a2'], lw['g2p'], lw['vecs'], s0)
    yc, gv = gmlp_mix(proj, lw['gmlp_ln_w'], lw['gmlp_ln_b'], lw['gmlp_ws'], lw['gmlp_bst'])
    h = out_proj(x2, ya.reshape(b * t, A_W), yb.reshape(b * t, B_W), yc.reshape(b * t, C_W), lw['w_out'], l, tm)
    y = ffn(h, lw['norm_ffn'], lw['ffn_gate'], lw['ffn_up'], lw['ffn_down'], l, tm, 512)
    k = proj[:, :, A_W:2 * A_W].reshape(b, t, H_A, HD_A)
    v = proj[:, :, 2 * A_W:3 * A_W].reshape(b, t, H_A, HD_A)
    ik = proj[:, :, OFF_IK:OFF_IK + D_IDX]
    shift_new = _rwkv_cols(proj[:, t - 1])
    return y.reshape(b, t, d), k, v, ik, shift_new, wkv, gv


def kernel(x_prompt, x_sample, cache_k, cache_v, cache_idx_k, state_rwkv_shift, state_rwkv_wkv, page_table, rel_bias, norm_mix, w_in, rwkv_mu, rwkv_w0, rwkv_w2, rwkv_a0, rwkv_a2, rwkv_g2, rwkv_kk, rwkv_ka, rwkv_rk, rwkv_ln_w, rwkv_ln_b, gmlp_ln_w, gmlp_ln_b, gmlp_ws, gmlp_b, w_out, norm_ffn, ffn_gate, ffn_up, ffn_down, norm_final):
    depth = w_in.shape[0]
    nbp, seq, d = x_prompt.shape
    db, dec_seq, _ = x_sample.shape
    assert d == D_MODEL and seq % 512 == 0 and dec_seq == SUBLANES and w_in.shape[2] == ORIG_GM + 2 * C_W

    w_in_p = _pad_cols(jnp.swapaxes(w_in, 1, 2), axis=1).astype(BF16)
    w_out_b = w_out.astype(BF16)
    wg_b, wu_b, wd_b = ffn_gate.astype(BF16), ffn_up.astype(BF16), ffn_down.astype(BF16)
    mu_full = _pad_rwkv_cols(rwkv_mu)
    g2p = jnp.concatenate([rwkv_g2, jnp.zeros((depth, 256 - G_LORA, B_W), F32)], axis=1)
    vecs = jnp.stack([rwkv_w0, rwkv_a0, rwkv_kk, rwkv_ka, rwkv_rk, rwkv_ln_w, rwkv_ln_b,
                      jnp.zeros_like(rwkv_w0)], axis=1)
    bst = jnp.swapaxes(gmlp_b, 1, 2)
    shift_s = _pad_rwkv_cols(state_rwkv_shift)[:, :, None, :]
    shift_p = jnp.zeros((nbp, 1, PROJ_P), F32)
    bias_p, bias_s = bias_tiles(rel_bias, dec_seq)

    xp, xs = x_prompt, x_sample
    outs = [[] for _ in range(11)]
    for l in range(depth):
        lw = {'layer': l, 'norm_mix': norm_mix[l], 'w_in': w_in_p, 'mu_full': mu_full[l:l + 1], 'w2': rwkv_w2[l],
              'a2': rwkv_a2[l], 'g2p': g2p[l], 'vecs': vecs[l], 'gmlp_ln_w': gmlp_ln_w[l],
              'gmlp_ln_b': gmlp_ln_b[l], 'gmlp_ws': gmlp_ws[l], 'gmlp_bst': bst[l], 'w_out': w_out_b,
              'norm_ffn': norm_ffn[l], 'ffn_gate': wg_b, 'ffn_up': wu_b, 'ffn_down': wd_b}
        attend_p = functools.partial(dsa_prompt, bias_p=bias_p, rel_bias=rel_bias)
        xp, kp, vp, ikp, shp, wkvp, _ = _layer(xp, lw, attend_p, shift_p, None, 512)
        attend_s = functools.partial(dsa_sample, layer=l, cache_k=cache_k, cache_v=cache_v,
                                     cache_idx_k=cache_idx_k, page_table=page_table, bias_s=bias_s)
        xs, k_s, v_s, ik_s, sh_s, wkv_s, gv_s = _layer(xs, lw, attend_s, shift_s[l], state_rwkv_wkv[l],
                                                       db * dec_seq)
        for lst, val in zip(outs, (kp, vp, ikp, shp, wkvp, k_s, v_s, ik_s, sh_s, wkv_s, gv_s)):
            lst.append(val)
    y_prompt = rms_final(xp.reshape(nbp * seq, d), norm_final, 512).reshape(nbp, seq, d)
    y_sample = rms_final(xs.reshape(db * dec_seq, d), norm_final, db * dec_seq).reshape(db, dec_seq, d)
    return (y_prompt, y_sample) + tuple(jnp.stack(o) for o in outs)
```

```python
import functools
import math

import numpy as np
import jax
import jax.numpy as jnp
from jax import lax
from jax.experimental import pallas as pl
from jax.experimental.pallas import tpu as pltpu

F32 = jnp.float32
BF16 = jnp.bfloat16
I32 = jnp.int32

LANES = 128
SUBLANES = 8
VMEM_LIMIT = 56 * 1024 * 1024

D_MODEL = 2048
HD_A = 64
H_A = 8
A_W = H_A * HD_A
H_IDX = 8
D_IDX = 64
IDX_SCALE = (D_IDX ** -0.5) * (H_IDX ** -0.5)
TOPK_MAX = 256
N_BUCKETS = 32
MAX_DISTANCE = 128
HD_B = 64
B_W = 1024
H_B = B_W // HD_B
W_LORA = 64
A_LORA = 64
G_LORA = 160
RWKV_PROJ_W = 3 * B_W + W_LORA + A_LORA + G_LORA
GN_EPS = 64e-5
C_W = 512
CHUNK = 128
CD = 64
G_C = C_W // CD
D_FF = 5632
NORM_EPS = 1e-6
NEG = -1e30
PAGE = 128
INT_MIN = -(2 ** 31)

ORIG_IK = 3 * A_W + H_IDX * D_IDX
ORIG_R = ORIG_IK + D_IDX + H_IDX
ORIG_GM = ORIG_R + RWKV_PROJ_W
OFF_R = 2048
OFF_IK = OFF_R + 3 * B_W
OFF_WA = OFF_IK + LANES
OFF_G = OFF_WA + W_LORA + A_LORA
OFF_GM = OFF_G + 256
PROJ_P = OFF_GM + 2 * C_W
CB_Q, CB_K, CB_V, CB_IQ = 0, 1, 2, 3
CB_IK = OFF_IK // LANES
CB_WA = OFF_WA // LANES
CB_G = OFF_G // 256
CB_U = OFF_GM // C_W
CB_PV = CB_U + 1
RW_C = 64
RW_SEQS_PER_STEP = 4
assert OFF_R % B_W == 0 and OFF_G % 256 == 0 and OFF_GM % C_W == 0


def _cparams(sem):
    return pltpu.CompilerParams(dimension_semantics=sem, vmem_limit_bytes=VMEM_LIMIT)


def _pad_cols(a, axis=-1):
    a = jnp.moveaxis(a, axis, -1)
    z = lambda n: jnp.zeros(a.shape[:-1] + (n,), a.dtype)
    lora = ORIG_R + 3 * B_W
    out = jnp.concatenate([a[..., :ORIG_IK], a[..., ORIG_R:lora], a[..., ORIG_IK:ORIG_R], z(OFF_WA - OFF_IK - D_IDX - H_IDX),
                           a[..., lora:ORIG_GM], z(OFF_GM - OFF_G - G_LORA), a[..., ORIG_GM:]], axis=-1)
    return jnp.moveaxis(out, -1, axis)


def _pad_rwkv_cols(a):
    z = lambda n: jnp.zeros(a.shape[:-1] + (n,), a.dtype)
    return jnp.concatenate([z(OFF_R), a[..., :3 * B_W], z(LANES), a[..., 3 * B_W:], z(PROJ_P - OFF_G - G_LORA)], axis=-1)


def _rwkv_cols(p):
    return jnp.concatenate([p[..., OFF_R:OFF_R + 3 * B_W], p[..., OFF_WA:OFF_WA + RWKV_PROJ_W - 3 * B_W]], axis=-1)


def _norm_matmul_kernel(x_ref, g_ref, w_ref, o_ref, xn_ref):
    @pl.when(pl.program_id(1) == 0)
    def _():
        x = x_ref[...]
        y = x * lax.rsqrt(jnp.mean(x * x, axis=-1, keepdims=True) + NORM_EPS)
        xn_ref[...] = (y * g_ref[...]).astype(BF16)

    o_ref[...] = lax.dot_general(xn_ref[...], w_ref[...], (((1,), (1,)), ((), ())), preferred_element_type=F32)


def norm_matmul(x, g, wt, layer, tm, tn):
    m, d = x.shape
    n = wt.shape[1]
    return pl.pallas_call(
        _norm_matmul_kernel,
        grid=(m // tm, n // tn),
        in_specs=[pl.BlockSpec((tm, d), lambda i, j: (i, 0)),
                  pl.BlockSpec((1, d), lambda i, j: (0, 0)),
                  pl.BlockSpec((None, tn, d), lambda i, j: (layer, j, 0))],
        out_specs=pl.BlockSpec((tm, tn), lambda i, j: (i, j)),
        out_shape=jax.ShapeDtypeStruct((m, n), F32),
        scratch_shapes=[pltpu.VMEM((tm, d), BF16)],
        compiler_params=_cparams(("parallel", "arbitrary")),
        name="norm_matmul",
    )(x, g.reshape(1, d), wt)


def _out_proj_kernel(x_ref, ya_ref, yb_ref, yc_ref, w_ref, o_ref):
    acc = jnp.dot(ya_ref[...], w_ref[0:A_W, :], preferred_element_type=F32)
    acc += jnp.dot(yb_ref[...], w_ref[A_W:A_W + B_W, :], preferred_element_type=F32)
    acc += jnp.dot(yc_ref[...], w_ref[A_W + B_W:, :], preferred_element_type=F32)
    o_ref[...] = x_ref[...] + acc


def out_proj(x, ya, yb, yc, w, layer, tm):
    m, d = x.shape
    return pl.pallas_call(
        _out_proj_kernel,
        grid=(m // tm,),
        in_specs=[pl.BlockSpec((tm, d), lambda i: (i, 0)),
                  pl.BlockSpec((tm, A_W), lambda i: (i, 0)),
                  pl.BlockSpec((tm, B_W), lambda i: (i, 0)),
                  pl.BlockSpec((tm, C_W), lambda i: (i, 0)),
                  pl.BlockSpec((None,) + w.shape[1:], lambda i: (layer, 0, 0))],
        out_specs=pl.BlockSpec((tm, d), lambda i: (i, 0)),
        out_shape=jax.ShapeDtypeStruct((m, d), F32),
        compiler_params=_cparams(("parallel",)),
        name="out_proj",
    )(x, ya, yb, yc, w)


def _ffn_kernel(h_ref, g_ref, wg_ref, wu_ref, wd_ref, o_ref, hn_ref, acc_ref):
    f = pl.program_id(1)

    @pl.when(f == 0)
    def _():
        x = h_ref[...]
        y = x * lax.rsqrt(jnp.mean(x * x, axis=-1, keepdims=True) + NORM_EPS)
        hn_ref[...] = (y * g_ref[...]).astype(BF16)
        acc_ref[...] = jnp.zeros_like(acc_ref)

    hn = hn_ref[...]
    gate = jnp.dot(hn, wg_ref[...], preferred_element_type=F32)
    up = jnp.dot(hn, wu_ref[...], preferred_element_type=F32)
    act = (gate / (1.0 + jnp.exp(-gate))) * up
    acc_ref[...] += jnp.dot(act.astype(BF16), wd_ref[...], preferred_element_type=F32)

    @pl.when(f == pl.num_programs(1) - 1)
    def _():
        o_ref[...] = h_ref[...] + acc_ref[...]


def ffn(h, g, wg, wu, wd, layer, tm, tf):
    m, d = h.shape
    nf = wg.shape[2]
    return pl.pallas_call(
        _ffn_kernel,
        grid=(m // tm, nf // tf),
        in_specs=[pl.BlockSpec((tm, d), lambda i, f: (i, 0)),
                  pl.BlockSpec((1, d), lambda i, f: (0, 0)),
                  pl.BlockSpec((None, d, tf), lambda i, f: (layer, 0, f)),
                  pl.BlockSpec((None, d, tf), lambda i, f: (layer, 0, f)),
                  pl.BlockSpec((None, tf, d), lambda i, f: (layer, f, 0))],
        out_specs=pl.BlockSpec((tm, d), lambda i, f: (i, 0)),
        out_shape=jax.ShapeDtypeStruct((m, d), F32),
        scratch_shapes=[pltpu.VMEM((tm, d), BF16), pltpu.VMEM((tm, d), F32)],
        compiler_params=_cparams(("parallel", "arbitrary")),
        name="ffn",
    )(h, g.reshape(1, d), wg, wu, wd)


def _rms_kernel(x_ref, g_ref, o_ref):
    x = x_ref[...]
    o_ref[...] = x * lax.rsqrt(jnp.mean(x * x, axis=-1, keepdims=True) + NORM_EPS) * g_ref[...]


def rms_final(x, g, tm):
    m, d = x.shape
    return pl.pallas_call(
        _rms_kernel,
        grid=(m // tm,),
        in_specs=[pl.BlockSpec((tm, d), lambda i: (i, 0)), pl.BlockSpec((1, d), lambda i: (0, 0))],
        out_specs=pl.BlockSpec((tm, d), lambda i: (i, 0)),
        out_shape=jax.ShapeDtypeStruct((m, d), F32),
        compiler_params=_cparams(("parallel",)),
        name="rms_final",
    )(x, g.reshape(1, d))


def _bucket_np(dist):
    max_exact = N_BUCKETS // 2
    n = np.maximum(dist, 0)
    nf = np.maximum(n, 1).astype(np.float64)
    large = max_exact + (np.log(nf / max_exact) / math.log(MAX_DISTANCE / max_exact)
                         * (N_BUCKETS - max_exact)).astype(np.int32)
    return np.where(n < max_exact, n, np.minimum(large, N_BUCKETS - 1)).astype(np.int32)


def _bias_tiles_kernel(rb_ref, bp_ref, bs_ref, op_ref, os_ref):
    for d in range(2):
        bk = bp_ref[d]
        for h in range(H_A):
            t = jnp.zeros(bk.shape, F32)
            for b in range(N_BUCKETS):
                t = jnp.where(bk == b, rb_ref[b, h], t)
            op_ref[d, h] = t
    for d in range(3):
        bk = bs_ref[d]
        for h in range(H_A):
            t = jnp.zeros(bk.shape, F32)
            for b in range(N_BUCKETS):
                t = jnp.where(bk == b, rb_ref[b, h], t)
            os_ref[d, h * SUBLANES:(h + 1) * SUBLANES, :] = t


def bias_tiles(rel_bias, dec_seq):
    r = np.arange(LANES)[:, None]
    c = np.arange(LANES)[None, :]
    bp = np.stack([_bucket_np(r - c), _bucket_np(LANES + r - c)])
    t = np.arange(dec_seq)[:, None]
    bs = np.stack([_bucket_np(np.full((dec_seq, LANES), 4 * MAX_DISTANCE)),
                   _bucket_np(PAGE + t - c), _bucket_np(t - c)])
    vm = pl.BlockSpec(memory_space=pltpu.VMEM)
    return pl.pallas_call(
        _bias_tiles_kernel,
        in_specs=[pl.BlockSpec(memory_space=pltpu.SMEM), vm, vm],
        out_specs=[vm, vm],
        out_shape=[jax.ShapeDtypeStruct((2, H_A, LANES, LANES), F32),
                   jax.ShapeDtypeStruct((3, H_A * dec_seq, LANES), F32)],
        name="bias_tiles",
    )(rel_bias, jnp.asarray(bp), jnp.asarray(bs))


def _score_key(s):
    s = jnp.where(s == 0.0, 0.0, s)
    bits = lax.bitcast_convert_type(s, I32)
    return jnp.where(bits < 0, bits ^ 0x7FFFFFFF, bits)


class _Pairs:
    def __init__(self, n_pairs):
        self.n_pairs = n_pairs


def _chunk_loop(n_chunks, body, init):
    if isinstance(n_chunks, _Pairs):
        return lax.fori_loop(0, n_chunks.n_pairs, lambda m, c: body(2 * m + 1, body(2 * m, c)), init)
    return lax.fori_loop(0, n_chunks, body, init)


def _kth_largest(sc_ref, n_chunks, kf, shape):
    def count_ge(cand):
        if isinstance(n_chunks, int):
            ge = (sc_ref[...] >= cand[None]).astype(F32)
            q = -(-n_chunks // 4)
            a = ((jnp.sum(ge[0:q], axis=0) + jnp.sum(ge[q:2 * q], axis=0))
                 + (jnp.sum(ge[2 * q:3 * q], axis=0) + jnp.sum(ge[3 * q:], axis=0)))
        else:
            parts = []
            for r0 in range(0, shape[0], LANES):
                cs = cand[r0:r0 + LANES]

                def body(kc, a, r0=r0, cs=cs):
                    return a + (sc_ref[kc, r0:r0 + LANES, :] >= cs).astype(F32)

                parts.append(_chunk_loop(n_chunks, body, jnp.zeros((LANES, LANES), F32)))
            a = parts[0] if len(parts) == 1 else jnp.concatenate(parts, axis=0)
        return jnp.sum(a, axis=1, keepdims=True)

    zero = jnp.zeros(shape, I32)
    t0 = jnp.where(count_ge(zero) >= kf, zero, jnp.full(shape, INT_MIN, I32))

    def bit_body(it, t):
        cand = t | jnp.left_shift(jnp.int32(1), 30 - it)
        return jnp.where(count_ge(cand) >= kf, cand, t)

    return lax.fori_loop(0, 31, bit_body, t0)


def _kth_largest_t(sct_ref, n_chunks, kf, tq):
    groups = LANES // SUBLANES

    def count_ge(cand):
        def body(kc, a):
            ge = sct_ref[kc].reshape(groups, SUBLANES, tq) >= cand[None]
            return a + jnp.sum(ge.astype(F32), axis=0)

        a = _chunk_loop(n_chunks, body, jnp.zeros((SUBLANES, tq), F32))
        return jnp.sum(a, axis=0, keepdims=True)

    zero = jnp.zeros((SUBLANES, tq), I32)
    t0 = jnp.where(count_ge(zero) >= kf, zero, jnp.full((SUBLANES, tq), INT_MIN, I32))

    def bit_body(it, t):
        cand = t | jnp.left_shift(jnp.int32(1), 30 - it)
        return jnp.where(count_ge(cand) >= kf, cand, t)

    t = lax.fori_loop(0, 31, bit_body, t0)
    hi = jnp.transpose(jnp.broadcast_to((t[0:1] >> 16).astype(F32), (LANES, tq)))
    lo = jnp.transpose(jnp.broadcast_to((t[0:1] & 0xFFFF).astype(F32), (LANES, tq)))
    return (hi.astype(I32) << 16) | lo.astype(I32)


def _select_mask(sc_ref, mb_ref, n_chunks, kf, thr, shape, causal_fn):
    def cnt_body(kc, carry):
        key = sc_ref[kc]
        return carry[0] + (key > thr).astype(F32), carry[1] + (key >= thr).astype(F32)

    zeros = jnp.zeros(shape, F32)
    if isinstance(n_chunks, int):
        keys = sc_ref[...]
        cgt = jnp.sum((keys > thr[None]).astype(F32), axis=0)
        cge = jnp.sum((keys >= thr[None]).astype(F32), axis=0)
    else:
        cgt, cge = _chunk_loop(n_chunks, cnt_body, (zeros, zeros))
    need = kf - jnp.sum(cgt, axis=1, keepdims=True)
    cge = jnp.sum(cge, axis=1, keepdims=True)

    def exact_k():
        def body(kc, carry):
            mb_ref[kc] = jnp.where((sc_ref[kc] >= thr) & causal_fn(kc), 0.0, NEG)
            return carry

        if isinstance(n_chunks, int):
            lax.fori_loop(0, n_chunks, body, 0, unroll=8)
        else:
            _chunk_loop(n_chunks, body, 0)

    def with_ties():
        rr = lax.broadcasted_iota(I32, (LANES, LANES), 0)
        cc = lax.broadcasted_iota(I32, (LANES, LANES), 1)
        tri = (rr <= cc).astype(BF16)

        def body(kc, offs):
            key = sc_ref[kc]
            eq = key == thr
            pre = jnp.dot(eq.astype(BF16), tri, preferred_element_type=F32) + offs
            sel = (key > thr) | (eq & (pre <= need))
            mb_ref[kc] = jnp.where(sel & causal_fn(kc), 0.0, NEG)
            return jnp.broadcast_to(pre[:, LANES - 1:LANES], shape)

        _chunk_loop(n_chunks, body, zeros)

    lax.cond(jnp.max(jnp.abs(cge - kf)) > 0.5, with_ties, exact_k)


def _dsa_prompt_kernel(topk, q_ref, iq_ref, iw_ref, k_ref, v_ref, ik_ref, bias_ref, rb_ref, o_ref,
                       kb, vb, ikd, qm, iqm, iwb, sc, sct, mb, lgs, acc, m_s, l_s):
    i = pl.program_id(1)
    tq = q_ref.shape[0]
    shape = (tq, LANES)
    nt = (((1,), (1,)), ((), ()))

    @pl.when(i == 0)
    def _():
        kb[...] = k_ref[...].astype(BF16)
        vb[...] = v_ref[...].astype(BF16)
        ik = ik_ref[:, 0:D_IDX].astype(BF16)
        ikd[...] = jnp.concatenate([ik, ik], axis=1)

    row = lax.broadcasted_iota(I32, shape, 0) + i * tq
    col = lax.broadcasted_iota(I32, shape, 1)
    lane_lo = col < HD_A
    sub = tq // LANES
    n_chunks = _Pairs((i + 1) * (sub // 2))
    iw = iw_ref[:, D_IDX:D_IDX + H_IDX]
    for h in range(H_A):
        ps = slice((h // 2) * LANES, (h // 2 + 1) * LANES)
        keep = lane_lo if h % 2 == 0 else jnp.logical_not(lane_lo)
        qm[h] = jnp.where(keep, q_ref[:, ps] * (HD_A ** -0.5), 0.0).astype(BF16)
        iqm[h] = jnp.where(keep, iq_ref[:, ps], 0.0).astype(BF16)
        iwb[h] = jnp.broadcast_to(iw[:, h:h + 1], shape)

    def causal(kc):
        return (col + kc * LANES) <= row

    def score_body(kc, carry):
        ikc = ikd[pl.ds(pl.multiple_of(kc * LANES, LANES), LANES), :]
        ss = [lax.dot_general(iqm[h], ikc, nt, preferred_element_type=F32) for h in range(H_IDX)]
        ws = [iwb[h] * jnp.maximum(ss[h], 0.0) for h in range(H_IDX)]
        tot = ((ws[0] + ws[1]) + (ws[2] + ws[3])) + ((ws[4] + ws[5]) + (ws[6] + ws[7]))
        tot = jnp.where(causal(kc), tot * IDX_SCALE, NEG)
        sc[kc] = _score_key(tot)
        sct[kc] = _score_key(jnp.transpose(tot))
        return carry

    _chunk_loop(n_chunks, score_body, 0)
    kf = float(topk)
    thr = _kth_largest_t(sct, n_chunks, kf, tq)
    _select_mask(sc, mb, n_chunks, kf, thr, shape, causal)

    m_s[...] = jnp.full(m_s.shape, NEG, F32)

    def bias_tile(h, kc):
        tiles = []
        for s in range(sub):
            d = i * sub + s - kc
            tiles.append(jnp.where(d == 0, bias_ref[0, h],
                                   jnp.where(d == 1, bias_ref[1, h], rb_ref[N_BUCKETS - 1, h])))
        return jnp.concatenate(tiles, axis=0)

    def logit_body(m, carry):
        off = pl.multiple_of(m * 2 * LANES, 2 * LANES)
        mb2 = jnp.concatenate([mb[2 * m], mb[2 * m + 1]], axis=1)
        for h in range(H_A):
            ps = slice((h // 2) * LANES, (h // 2 + 1) * LANES)
            lg = lax.dot_general(qm[h], kb[pl.ds(off, 2 * LANES), ps], nt, preferred_element_type=F32)
            lg = lg + jnp.concatenate([bias_tile(h, 2 * m), bias_tile(h, 2 * m + 1)], axis=1) + mb2
            lgs[h, 2 * m] = lg[:, 0:LANES]
            lgs[h, 2 * m + 1] = lg[:, LANES:]
            m_s[h] = jnp.maximum(m_s[h], jnp.maximum(lg[:, 0:LANES], lg[:, LANES:]))
        return carry

    lax.fori_loop(0, (i + 1) * (sub // 2), logit_body, 0)
    for h in range(H_A):
        m_s[h] = jnp.broadcast_to(jnp.max(m_s[h], axis=1, keepdims=True), shape)
    l_s[...] = jnp.zeros(l_s.shape, F32)
    acc[...] = jnp.zeros(acc.shape, F32)

    def attn_body(kc, carry):
        off = pl.multiple_of(kc * LANES, LANES)
        for h in range(H_A):
            ps = slice((h // 2) * LANES, (h // 2 + 1) * LANES)
            p = jnp.exp(lgs[h, kc] - m_s[h])
            l_s[h] += p
            acc[h] += jnp.dot(p.astype(BF16), vb[pl.ds(off, LANES), ps], preferred_element_type=F32)
        return carry

    _chunk_loop(n_chunks, attn_body, 0)
    for j in range(H_A // 2):
        lo = acc[2 * j] / jnp.sum(l_s[2 * j], axis=1, keepdims=True)
        hi = acc[2 * j + 1] / jnp.sum(l_s[2 * j + 1], axis=1, keepdims=True)
        o_ref[:, j * LANES:(j + 1) * LANES] = jnp.where(lane_lo, lo, hi).astype(o_ref.dtype)


def dsa_prompt(proj, bias_p, rel_bias, tq=2 * LANES):
    b, s, _ = proj.shape
    assert tq % (2 * LANES) == 0 and s % tq == 0
    once = pl.Buffered(1)
    topk = min(TOPK_MAX, s // 4)
    nc = s // LANES
    kern = functools.partial(_dsa_prompt_kernel, topk)
    return pl.pallas_call(
        kern,
        grid=(b, s // tq),
        in_specs=[pl.BlockSpec((None, tq, A_W), lambda bi, i: (bi, i, CB_Q)),
                  pl.BlockSpec((None, tq, A_W), lambda bi, i: (bi, i, CB_IQ)),
                  pl.BlockSpec((None, tq, LANES), lambda bi, i: (bi, i, CB_IK)),
                  pl.BlockSpec((None, s, A_W), lambda bi, i: (bi, 0, CB_K), pipeline_mode=once),
                  pl.BlockSpec((None, s, A_W), lambda bi, i: (bi, 0, CB_V), pipeline_mode=once),
                  pl.BlockSpec((None, s, LANES), lambda bi, i: (bi, 0, CB_IK), pipeline_mode=once),
                  pl.BlockSpec(bias_p.shape, lambda bi, i: (0, 0, 0, 0), pipeline_mode=once),
                  pl.BlockSpec(memory_space=pltpu.SMEM)],
        out_specs=pl.BlockSpec((None, tq, A_W), lambda bi, i: (bi, i, 0)),
        out_shape=jax.ShapeDtypeStruct((b, s, A_W), BF16),
        scratch_shapes=[pltpu.VMEM((s, A_W), BF16), pltpu.VMEM((s, A_W), BF16), pltpu.VMEM((s, LANES), BF16),
                        pltpu.VMEM((H_A, tq, LANES), BF16), pltpu.VMEM((H_IDX, tq, LANES), BF16),
                        pltpu.VMEM((H_IDX, tq, LANES), F32),
                        pltpu.VMEM((nc, tq, LANES), I32), pltpu.VMEM((nc, LANES, tq), I32),
                        pltpu.VMEM((nc, tq, LANES), F32),
                        pltpu.VMEM((H_A, nc, tq, LANES), F32),
                        pltpu.VMEM((H_A, tq, LANES), F32), pltpu.VMEM((H_A, tq, LANES), F32),
                        pltpu.VMEM((H_A, tq, LANES), F32)],
        compiler_params=_cparams(("parallel", "arbitrary")),
        name="dsa_prompt",
    )(proj, proj, proj, proj, proj, proj, bias_p, rel_bias)


def _dsa_sample_select_kernel(pps, n_pages, topk, pt_ref, iq_ref, ikiw_ref, *rest):
    page_refs = rest[:pps]
    mb_ref = rest[pps]
    iq2, wb, sc = rest[pps + 1:]
    j = pl.program_id(1)
    t = iq_ref.shape[0]
    shape = (t, LANES)
    nc = n_pages + 1

    @pl.when(j == 0)
    def _():
        iq = iq_ref[...]
        ikiw = ikiw_ref[...]
        for h in range(H_IDX):
            iq2[h * t:(h + 1) * t, :] = iq[:, h * D_IDX:(h + 1) * D_IDX].astype(BF16)
            wb[h * t:(h + 1) * t, :] = jnp.broadcast_to(ikiw[:, D_IDX + h:D_IDX + h + 1], shape)

    def scores(s):
        n = s.shape[1]
        s = jnp.maximum(s, 0.0) * jnp.tile(wb[...], (1, n // LANES))
        return jnp.sum(s.reshape(H_IDX, t, n), axis=0) * IDX_SCALE

    ikt = jnp.concatenate([r[...] for r in page_refs], axis=1).astype(BF16)
    keys = _score_key(scores(jnp.dot(iq2[...], ikt, preferred_element_type=F32)))
    for u in range(pps):
        sc[j * pps + u] = keys[:, u * LANES:(u + 1) * LANES]

    @pl.when(j == pl.num_programs(1) - 1)
    def _():
        row = lax.broadcasted_iota(I32, shape, 0)
        col = lax.broadcasted_iota(I32, shape, 1)
        ik_new = jnp.concatenate([ikiw_ref[:, 0:D_IDX], jnp.zeros((LANES - t, D_IDX), F32)], axis=0)
        s_new = lax.dot_general(iq2[...], ik_new.astype(BF16), (((1,), (1,)), ((), ())), preferred_element_type=F32)
        s_new = jnp.where(col <= row, scores(s_new), NEG)
        sc[n_pages] = jnp.where(col < t, _score_key(s_new), INT_MIN)
        kf = float(topk)
        thr = _kth_largest(sc, nc, kf, shape)

        def causal(kc):
            return (kc < n_pages) | (col <= row)

        _select_mask(sc, mb_ref, nc, kf, thr, shape, causal)


def _dsa_sample_attn_kernel(pps, n_pages, pt_ref, q_ref, kn_ref, vn_ref, mb_ref, mbn_ref, bias_ref, *rest):
    k_refs = rest[:pps]
    v_refs = rest[pps:2 * pps]
    o_ref = rest[2 * pps]
    qbd, acc, m_s, l_s = rest[2 * pps + 1:]
    j = pl.program_id(1)
    t = q_ref.shape[0]
    rows = H_A * t

    def blockdiag(x):
        r = lax.broadcasted_iota(I32, (rows, A_W), 0) // t
        c = lax.broadcasted_iota(I32, (rows, A_W), 1) // HD_A
        return jnp.where(r == c, jnp.tile(x, (H_A, 1)), 0.0)

    @pl.when(j == 0)
    def _():
        qbd[...] = blockdiag(q_ref[...] * (HD_A ** -0.5)).astype(BF16)
        m_s[...] = jnp.full(m_s.shape, NEG, F32)
        l_s[...] = jnp.zeros(l_s.shape, F32)
        acc[...] = jnp.zeros(acc.shape, F32)

    def step(lg, pv_fn):
        m_old = m_s[...]
        m_new = jnp.maximum(m_old, jnp.max(lg, axis=1, keepdims=True))
        p = jnp.exp(lg - m_new)
        alpha = jnp.exp(m_old - m_new)
        l_s[...] = alpha * l_s[...] + jnp.sum(p, axis=1, keepdims=True)
        acc[...] = alpha * acc[...] + pv_fn(p.astype(BF16))
        m_s[...] = m_new

    kt = jnp.concatenate([r[...].reshape(A_W, PAGE) for r in k_refs], axis=1).astype(BF16)
    vt = jnp.concatenate([r[...].reshape(A_W, PAGE) for r in v_refs], axis=1).astype(BF16)
    last = j == pl.num_programs(1) - 1
    bias = jnp.concatenate([bias_ref[0]] * (pps - 1) + [jnp.where(last, bias_ref[1], bias_ref[0])], axis=1)
    mbc = jnp.concatenate([mb_ref[u] for u in range(pps)], axis=1)
    lg = jnp.dot(qbd[...], kt, preferred_element_type=F32) + bias + jnp.tile(mbc, (H_A, 1))
    step(lg, lambda p: lax.dot_general(p, vt, (((1,), (1,)), ((), ())), preferred_element_type=F32))

    @pl.when(last)
    def _():
        pad = jnp.zeros((LANES - t, A_W), F32)
        kn = jnp.concatenate([kn_ref[...], pad], axis=0).astype(BF16)
        vn = jnp.concatenate([vn_ref[...], pad], axis=0).astype(BF16)
        lgn = lax.dot_general(qbd[...], kn, (((1,), (1,)), ((), ())), preferred_element_type=F32)
        step(lgn + bias_ref[2] + jnp.tile(mbn_ref[0], (H_A, 1)),
             lambda p: jnp.dot(p, vn, preferred_element_type=F32))
        res = blockdiag_sum(acc[...] / l_s[...], t)
        o_ref[...] = res.astype(o_ref.dtype)


def blockdiag_sum(x, t):
    c = lax.broadcasted_iota(I32, (t, A_W), 1) // HD_A
    out = jnp.zeros((t, A_W), F32)
    for h in range(H_A):
        out = out + jnp.where(c == h, x[h * t:(h + 1) * t, :], 0.0)
    return out


def dsa_sample(proj, layer, cache_k, cache_v, cache_idx_k, page_table, bias_s, pps_sel=32, pps_att=32):
    db, t, _ = proj.shape
    n_pages = page_table.shape[1]
    nc = n_pages + 1
    topk = min(TOPK_MAX, (n_pages * PAGE + t) // 4)
    n_pool = cache_k.shape[1]
    ck = jnp.transpose(cache_k, (0, 1, 3, 4, 2))
    cv = jnp.transpose(cache_v, (0, 1, 3, 4, 2))
    cik = jnp.swapaxes(cache_idx_k, 2, 3)
    pt = page_table.reshape(-1)

    def page_spec(dims, pps, u):
        zeros = (0,) * len(dims)
        return pl.BlockSpec((None, None) + dims,
                            lambda b, j, ptr: (layer, ptr[b * n_pages + j * pps + u]) + zeros)

    sel = pl.pallas_call(
        functools.partial(_dsa_sample_select_kernel, pps_sel, n_pages, topk),
        grid_spec=pltpu.PrefetchScalarGridSpec(
            num_scalar_prefetch=1,
            grid=(db, n_pages // pps_sel),
            in_specs=[pl.BlockSpec((None, t, A_W), lambda b, j, ptr: (b, 0, CB_IQ)),
                      pl.BlockSpec((None, t, LANES), lambda b, j, ptr: (b, 0, CB_IK))]
                     + [page_spec((D_IDX, PAGE), pps_sel, u) for u in range(pps_sel)],
            out_specs=pl.BlockSpec((None, nc, t, LANES), lambda b, j, ptr: (b, 0, 0, 0)),
            scratch_shapes=[pltpu.VMEM((H_IDX * t, D_IDX), BF16), pltpu.VMEM((H_IDX * t, LANES), F32),
                            pltpu.VMEM((nc, t, LANES), I32)]),
        out_shape=jax.ShapeDtypeStruct((db, nc, t, LANES), F32),
        compiler_params=_cparams(("parallel", "arbitrary")),
        name="dsa_sample_select",
    )(pt, proj, proj, *([cik] * pps_sel))

    return pl.pallas_call(
        functools.partial(_dsa_sample_attn_kernel, pps_att, n_pages),
        grid_spec=pltpu.PrefetchScalarGridSpec(
            num_scalar_prefetch=1,
            grid=(db, n_pages // pps_att),
            in_specs=[pl.BlockSpec((None, t, A_W), lambda b, j, ptr: (b, 0, CB_Q)),
                      pl.BlockSpec((None, t, A_W), lambda b, j, ptr: (b, 0, CB_K)),
                      pl.BlockSpec((None, t, A_W), lambda b, j, ptr: (b, 0, CB_V)),
                      pl.BlockSpec((None, pps_att, t, LANES), lambda b, j, ptr: (b, j, 0, 0)),
                      pl.BlockSpec((None, 1, t, LANES), lambda b, j, ptr: (b, n_pages, 0, 0)),
                      pl.BlockSpec(bias_s.shape, lambda b, j, ptr: (0, 0, 0))]
                     + [page_spec((H_A, HD_A, PAGE), pps_att, u) for u in range(pps_att)] * 2,
            out_specs=pl.BlockSpec((None, t, A_W), lambda b, j, ptr: (b, 0, 0)),
            scratch_shapes=[pltpu.VMEM((H_A * t, A_W), BF16), pltpu.VMEM((H_A * t, A_W), F32),
                            pltpu.VMEM((H_A * t, 1), F32), pltpu.VMEM((H_A * t, 1), F32)]),
        out_shape=jax.ShapeDtypeStruct((db, t, A_W), BF16),
        compiler_params=_cparams(("parallel", "arbitrary")),
        name="dsa_sample_attn",
    )(pt, proj, proj, proj, sel, sel, bias_s, *([ck] * pps_att), *([cv] * pps_att))


def _mm(a, b):
    return jnp.dot(a.astype(BF16), b.astype(BF16), preferred_element_type=F32)


def _mm_nt(a, b):
    return lax.dot_general(a.astype(BF16), b.astype(BF16), (((1,), (1,)), ((), ())), preferred_element_type=F32)


def _split3(x):
    hi = x.astype(BF16)
    r1 = x - hi.astype(F32)
    mid = r1.astype(BF16)
    lo = (r1 - mid.astype(F32)).astype(BF16)
    return hi, mid, lo


def _rwkv_kernel(has_state, t_valid, r_ref, k_ref, v_ref, wa_ref, g_ref, sr_ref, sk_ref, sv_ref, swa_ref, sg_ref,
                 mr_ref, mk_ref, mv_ref, mwa_ref, mg_ref, w2_ref, a2_ref, g2_ref, vec_ref, *rest):
    if has_state:
        s0_ref, y_ref, so_ref, pr, pk, pv, pwa, pg, st = rest
    else:
        y_ref, so_ref, pr, pk, pv, pwa, pg, st = rest
    c = pl.program_id(1)
    C = RW_C
    nb, tb = r_ref.shape[0], r_ref.shape[1]
    n_pairs = H_B // 2
    zero_blk = jnp.zeros((HD_B, HD_B), F32)

    @pl.when(c == 0)
    def _():
        if has_state:
            for bi in range(nb):
                for p in range(n_pairs):
                    st[bi * n_pairs + p] = jnp.concatenate(
                        [jnp.concatenate([s0_ref[bi, 2 * p], zero_blk], axis=1),
                         jnp.concatenate([zero_blk, s0_ref[bi, 2 * p + 1]], axis=1)], axis=0)
        else:
            st[...] = jnp.zeros(st.shape, F32)

    def shifted(bi, p_ref, prev_scr, shift_ref, mu_ref):
        p = p_ref[bi]
        if tb < C:
            p = jnp.concatenate([p, jnp.zeros((C - tb, p.shape[1]), F32)], axis=0)
        prev = jnp.where(c == 0, shift_ref[bi], prev_scr[bi])
        rowi = lax.broadcasted_iota(I32, p.shape, 0)
        ps = jnp.where(rowi == 0, prev, pltpu.roll(p, 1, 0))
        prev_scr[bi] = p[C - 1:C, :]
        return p + (ps - p) * mu_ref[...]

    vec = vec_ref[...]
    w0, a0, kkp, kap, rkp, lnw, lnb = (vec[n:n + 1, :] for n in range(7))
    r2 = lax.broadcasted_iota(I32, (2 * C, 2 * C), 0)
    c2 = lax.broadcasted_iota(I32, (2 * C, 2 * C), 1)
    same_head = (r2 // C) == (c2 // C)
    head_ones = same_head.astype(BF16)
    strict = same_head & ((c2 % C) < (r2 % C))
    incl = same_head & ((c2 % C) <= (r2 % C))

    def pairs(x):
        return jnp.concatenate([x[:, p * LANES:(p + 1) * LANES] for p in range(n_pairs)], axis=0)

    def unpairs(x):
        return jnp.concatenate([x[p * C:(p + 1) * C] for p in range(n_pairs)], axis=1)

    def head_sum(x):
        xs = pairs(x)
        hi = xs.astype(BF16)
        mid = (xs - hi.astype(F32)).astype(BF16)
        return unpairs(jnp.dot(hi, head_ones, preferred_element_type=F32)
                       + jnp.dot(mid, head_ones, preferred_element_type=F32))

    rc = lax.broadcasted_iota(I32, (C, C), 0)
    cc = lax.broadcasted_iota(I32, (C, C), 1)
    tril = (cc <= rc).astype(BF16)

    def prep(bi):
        xr = shifted(bi, r_ref, pr, sr_ref, mr_ref)
        xk = shifted(bi, k_ref, pk, sk_ref, mk_ref)
        xv = shifted(bi, v_ref, pv, sv_ref, mv_ref)
        xwa = shifted(bi, wa_ref, pwa, swa_ref, mwa_ref)
        xg = shifted(bi, g_ref, pg, sg_ref, mg_ref)
        zw = w0 + _mm(jnp.tanh(xwa[:, 0:W_LORA]), w2_ref[...])
        w_log = -(jnp.maximum(-zw, 0.0) + jnp.log(1.0 + jnp.exp(-jnp.abs(zw)))) - 0.5
        ld = -jnp.exp(w_log)
        za = a0 + _mm(xwa[:, W_LORA:W_LORA + A_LORA], a2_ref[...])
        a = 1.0 / (1.0 + jnp.exp(-za))
        g = _mm(1.0 / (1.0 + jnp.exp(-xg)), g2_ref[...])
        kk = xk * kkp
        kk = kk / jnp.maximum(jnp.sqrt(head_sum(kk * kk)), 1e-12)
        k2 = xk * (1.0 + (a - 1.0) * kap)
        bonus_w = head_sum(xr * k2 * rkp)
        if t_valid < C:
            valid = lax.broadcasted_iota(I32, (C, B_W), 0) < t_valid
            ld = jnp.where(valid, ld, 0.0)
            xv = jnp.where(valid, xv, 0.0)
            kk = jnp.where(valid, kk, 0.0)
            k2m = jnp.where(valid, k2, 0.0)
        else:
            k2m = k2
        hi, mid, lo = _split3(ld)
        lgc = (jnp.dot(tril, hi, preferred_element_type=F32) + jnp.dot(tril, mid, preferred_element_type=F32)
               + jnp.dot(tril, lo, preferred_element_type=F32))
        lg_end = lgc[C - 1:C, :]
        ginv = jnp.exp(-lgc)
        e_end = jnp.exp(lg_end - lgc)
        kb_ = kk * a
        return dict(al=kk * jnp.exp(lgc - ld), be=kb_ * ginv, kt=k2m * ginv, rt=xr * jnp.exp(lgc),
                    bp=kb_ * e_end, kp=k2m * e_end, xv=xv, g_end=jnp.exp(lg_end), bonus_w=bonus_w, g=g)

    seqs = [prep(bi) for bi in range(nb)]
    lane_lo = lax.broadcasted_iota(I32, (C, LANES), 1) < HD_B

    def stack(name, q):
        bi, p = divmod(q, n_pairs)
        xp = seqs[bi][name][:, p * LANES:(p + 1) * LANES]
        return jnp.concatenate([jnp.where(lane_lo, xp, 0.0), jnp.where(lane_lo, 0.0, xp)], axis=0)

    P = range(nb * n_pairs)
    bf = lambda x: x.astype(BF16)
    lhs = [bf(jnp.concatenate([stack('al', p), stack('rt', p)], axis=0)) for p in P]
    vs = [bf(stack('xv', p)) for p in P]
    g3 = [_mm_nt(lhs[p], jnp.concatenate([bf(stack('be', p)), bf(stack('kt', p)), bf(st[p])], axis=0)) for p in P]
    gb = [g3[p][:, 0:2 * C] for p in P]
    gk = [g3[p][:, 2 * C:4 * C] for p in P]
    p0 = [g3[p][:, 4 * C:] for p in P]
    qn = [bf(jnp.where(strict, -gb[p][0:2 * C], 0.0)) for p in P]
    u = [p0[p][0:2 * C] + _mm(jnp.where(strict, gk[p][0:2 * C], 0.0), vs[p]) for p in P]
    n_levels = int(math.log2(C))
    for m in range(n_levels):
        if m + 1 < n_levels:
            qu = [_mm(qn[p], jnp.concatenate([qn[p], bf(u[p])], axis=1)) for p in P]
            u = [u[p] + qu[p][:, 2 * C:] for p in P]
            qn = [bf(qu[p][:, 0:2 * C]) for p in P]
        else:
            u = [u[p] + _mm(qn[p], u[p]) for p in P]
    m2 = [jnp.concatenate([bf(jnp.where(incl, -gb[p][2 * C:], 0.0)), bf(jnp.where(incl, gk[p][2 * C:], 0.0))], axis=1)
          for p in P]
    uv = [jnp.concatenate([bf(u[p]), vs[p]], axis=0) for p in P]
    yp = [p0[p][2 * C:] + _mm(m2[p], uv[p]) for p in P]
    upd = [lax.dot_general(jnp.concatenate([bf(-u[p]), vs[p]], axis=0),
                           bf(jnp.concatenate([stack('bp', p), stack('kp', p)], axis=0)),
                           (((0,), (0,)), ((), ())), preferred_element_type=F32) for p in P]
    for q in P:
        bi, p = divmod(q, n_pairs)
        st[q] = st[q] * seqs[bi]['g_end'][:, p * LANES:(p + 1) * LANES] + upd[q]
    for bi in range(nb):
        sq = seqs[bi]
        y = jnp.concatenate([yp[q][0:C] + yp[q][C:2 * C] for q in range(bi * n_pairs, (bi + 1) * n_pairs)],
                            axis=1)
        mu = head_sum(y) * (1.0 / HD_B)
        dy = y - mu
        var = head_sum(dy * dy) * (1.0 / HD_B)
        yn = dy * lax.rsqrt(var + GN_EPS) * lnw + lnb
        out = (yn + sq['bonus_w'] * sq['xv']) * sq['g']
        y_ref[bi] = out[0:tb].astype(y_ref.dtype)

    @pl.when(c == pl.num_programs(1) - 1)
    def _():
        for bi in range(nb):
            for p in range(n_pairs):
                s = st[bi * n_pairs + p]
                so_ref[bi, 2 * p] = s[0:HD_B, 0:HD_B]
                so_ref[bi, 2 * p + 1] = s[HD_B:, HD_B:]


def rwkv_mix(proj, shift_full, mu_full, w2, a2, g2p, vecs, s0):
    b, t, _ = proj.shape
    tb = min(t, RW_C)
    nch = t // tb
    has_state = s0 is not None
    cbs = [(B_W, OFF_R // B_W), (B_W, OFF_R // B_W + 1), (B_W, OFF_R // B_W + 2), (LANES, CB_WA), (256, CB_G)]

    nb = RW_SEQS_PER_STEP
    assert b % nb == 0
    in_specs = ([pl.BlockSpec((nb, tb, w), functools.partial(lambda cb, bi, c: (bi, c, cb), cb)) for w, cb in cbs]
                + [pl.BlockSpec((nb, 1, w), functools.partial(lambda cb, bi, c: (bi, 0, cb), cb)) for w, cb in cbs]
                + [pl.BlockSpec((1, w), functools.partial(lambda cb, bi, c: (0, cb), cb)) for w, cb in cbs]
                + [pl.BlockSpec(w2.shape, lambda bi, c: (0, 0)),
                   pl.BlockSpec(a2.shape, lambda bi, c: (0, 0)),
                   pl.BlockSpec(g2p.shape, lambda bi, c: (0, 0)),
                   pl.BlockSpec(vecs.shape, lambda bi, c: (0, 0))])
    args = [proj] * 5 + [shift_full] * 5 + [mu_full] * 5 + [w2, a2, g2p, vecs]
    if has_state:
        in_specs.append(pl.BlockSpec((nb, H_B, HD_B, HD_B), lambda bi, c: (bi, 0, 0, 0)))
        args.append(s0)
    return pl.pallas_call(
        functools.partial(_rwkv_kernel, has_state, tb),
        grid=(b // nb, nch),
        in_specs=in_specs,
        out_specs=[pl.BlockSpec((nb, tb, B_W), lambda bi, c: (bi, c, 0)),
                   pl.BlockSpec((nb, H_B, HD_B, HD_B), lambda bi, c: (bi, 0, 0, 0))],
        out_shape=[jax.ShapeDtypeStruct((b, t, B_W), BF16),
                   jax.ShapeDtypeStruct((b, H_B, HD_B, HD_B), F32)],
        scratch_shapes=[pltpu.VMEM((nb, 1, B_W), F32)] * 3
                       + [pltpu.VMEM((nb, 1, LANES), F32), pltpu.VMEM((nb, 1, 256), F32),
                          pltpu.VMEM((nb * H_B // 2, 2 * HD_B, 2 * HD_B), F32)],
        compiler_params=_cparams(("parallel", "arbitrary")),
        name="rwkv_mix",
    )(*args)


def _gelu(x):
    return 0.5 * x * (1.0 + jnp.tanh(math.sqrt(2.0 / math.pi) * (x + 0.044715 * (x * x * x))))


def _gmlp_kernel(pu_ref, pv_ref, lnw_ref, lnb_ref, ws_ref, bst_ref, y_ref, gv_ref):
    tc = pu_ref.shape[0]
    u = _gelu(pu_ref[...])
    vf = _gelu(pv_ref[...])
    mu = jnp.mean(vf, axis=1, keepdims=True)
    var = jnp.mean(jnp.square(vf - mu), axis=1, keepdims=True)
    v = (vf - mu) * lax.rsqrt(var + 1e-5) * lnw_ref[...] + lnb_ref[...]
    gv_ref[...] = v
    ri = lax.broadcasted_iota(I32, (CHUNK, CHUNK), 0)
    ci = lax.broadcasted_iota(I32, (CHUNK, CHUNK), 1)
    if tc < CHUNK:
        v = jnp.concatenate([v, jnp.zeros((CHUNK - tc, C_W), F32)], axis=0)
    vb = v.astype(BF16)
    for g in range(G_C):
        sl = slice(g * CD, (g + 1) * CD)
        wm = jnp.where(ci <= ri, ws_ref[g], 0.0).astype(BF16)
        sp = jnp.dot(wm, vb[:, sl], preferred_element_type=F32)[0:tc] + bst_ref[0:tc, g:g + 1]
        y_ref[:, sl] = (u[:, sl] * sp).astype(y_ref.dtype)


def gmlp_mix(proj, lnw, lnb, ws, bst):
    b, t, _ = proj.shape
    tc = min(t, CHUNK)
    return pl.pallas_call(
        _gmlp_kernel,
        grid=(b, t // tc),
        in_specs=[pl.BlockSpec((None, tc, C_W), lambda bi, c: (bi, c, CB_U)),
                  pl.BlockSpec((None, tc, C_W), lambda bi, c: (bi, c, CB_PV)),
                  pl.BlockSpec((1, C_W), lambda bi, c: (0, 0)),
                  pl.BlockSpec((1, C_W), lambda bi, c: (0, 0)),
                  pl.BlockSpec(ws.shape, lambda bi, c: (0, 0, 0)),
                  pl.BlockSpec(bst.shape, lambda bi, c: (0, 0))],
        out_specs=[pl.BlockSpec((None, tc, C_W), lambda bi, c: (bi, c, 0)),
                   pl.BlockSpec((None, tc, C_W), lambda bi, c: (bi, c, 0))],
        out_shape=[jax.ShapeDtypeStruct((b, t, C_W), BF16), jax.ShapeDtypeStruct((b, t, C_W), F32)],
        compiler_params=_cparams(("parallel", "parallel")),
        name="gmlp_mix",
    )(proj, proj, lnw.reshape(1, C_W), lnb.reshape(1, C_W), ws, bst)


def _layer(x, lw, attend, shift_full, s0, tm):
    b, t, d = x.shape
    x2 = x.reshape(b * t, d)
    l = lw['layer']
    proj = norm_matmul(x2, lw['norm_mix'], lw['w_in'], l, min(2 * tm, b * t), 512).reshape(b, t, PROJ_P)
    ya = attend(proj)
    yb, wkv = rwkv_mix(proj, shift_full, lw['mu_full'], lw['w2'], lw['a2'], lw['g2p'], lw['vecs'], s0)
    yc, gv = gmlp_mix(proj, lw['gmlp_ln_w'], lw['gmlp_ln_b'], lw['gmlp_ws'], lw['gmlp_bst'])
    h = out_proj(x2, ya.reshape(b * t, A_W), yb.reshape(b * t, B_W), yc.reshape(b * t, C_W), lw['w_out'], l, tm)
    y = ffn(h, lw['norm_ffn'], lw['ffn_gate'], lw['ffn_up'], lw['ffn_down'], l, tm, 512)
    k = proj[:, :, A_W:2 * A_W].reshape(b, t, H_A, HD_A)
    v = proj[:, :, 2 * A_W:3 * A_W].reshape(b, t, H_A, HD_A)
    ik = proj[:, :, OFF_IK:OFF_IK + D_IDX]
    shift_new = _rwkv_cols(proj[:, t - 1])
    return y.reshape(b, t, d), k, v, ik, shift_new, wkv, gv


def kernel(x_prompt, x_sample, cache_k, cache_v, cache_idx_k, state_rwkv_shift, state_rwkv_wkv, page_table, rel_bias, norm_mix, w_in, rwkv_mu, rwkv_w0, rwkv_w2, rwkv_a0, rwkv_a2, rwkv_g2, rwkv_kk, rwkv_ka, rwkv_rk, rwkv_ln_w, rwkv_ln_b, gmlp_ln_w, gmlp_ln_b, gmlp_ws, gmlp_b, w_out, norm_ffn, ffn_gate, ffn_up, ffn_down, norm_final):
    depth = w_in.shape[0]
    nbp, seq, d = x_prompt.shape
    db, dec_seq, _ = x_sample.shape
    assert d == D_MODEL and seq % 512 == 0 and dec_seq == SUBLANES and w_in.shape[2] == ORIG_GM + 2 * C_W

    w_in_p = _pad_cols(jnp.swapaxes(w_in, 1, 2), axis=1).astype(BF16)
    w_out_b = w_out.astype(BF16)
    wg_b, wu_b, wd_b = ffn_gate.astype(BF16), ffn_up.astype(BF16), ffn_down.astype(BF16)
    mu_full = _pad_rwkv_cols(rwkv_mu)
    g2p = jnp.concatenate([rwkv_g2, jnp.zeros((depth, 256 - G_LORA, B_W), F32)], axis=1)
    vecs = jnp.stack([rwkv_w0, rwkv_a0, rwkv_kk, rwkv_ka, rwkv_rk, rwkv_ln_w, rwkv_ln_b,
                      jnp.zeros_like(rwkv_w0)], axis=1)
    bst = jnp.swapaxes(gmlp_b, 1, 2)
    shift_s = _pad_rwkv_cols(state_rwkv_shift)[:, :, None, :]
    shift_p = jnp.zeros((nbp, 1, PROJ_P), F32)
    bias_p, bias_s = bias_tiles(rel_bias, dec_seq)

    xp, xs = x_prompt, x_sample
    outs = [[] for _ in range(11)]
    for l in range(depth):
        lw = {'layer': l, 'norm_mix': norm_mix[l], 'w_in': w_in_p, 'mu_full': mu_full[l:l + 1], 'w2': rwkv_w2[l],
              'a2': rwkv_a2[l], 'g2p': g2p[l], 'vecs': vecs[l], 'gmlp_ln_w': gmlp_ln_w[l],
              'gmlp_ln_b': gmlp_ln_b[l], 'gmlp_ws': gmlp_ws[l], 'gmlp_bst': bst[l], 'w_out': w_out_b,
              'norm_ffn': norm_ffn[l], 'ffn_gate': wg_b, 'ffn_up': wu_b, 'ffn_down': wd_b}
        attend_p = functools.partial(dsa_prompt, bias_p=bias_p, rel_bias=rel_bias)
        xp, kp, vp, ikp, shp, wkvp, _ = _layer(xp, lw, attend_p, shift_p, None, 512)
        attend_s = functools.partial(dsa_sample, layer=l, cache_k=cache_k, cache_v=cache_v,
                                     cache_idx_k=cache_idx_k, page_table=page_table, bias_s=bias_s)
        xs, k_s, v_s, ik_s, sh_s, wkv_s, gv_s = _layer(xs, lw, attend_s, shift_s[l], state_rwkv_wkv[l],
                                                       db * dec_seq)
        for lst, val in zip(outs, (kp, vp, ikp, shp, wkvp, k_s, v_s, ik_s, sh_s, wkv_s, gv_s)):
            lst.append(val)
    y_prompt = rms_final(xp.reshape(nbp * seq, d), norm_final, 512).reshape(nbp, seq, d)
    y_sample = rms_final(xs.reshape(db * dec_seq, d), norm_final, db * dec_seq).reshape(db, dec_seq, d)
    return (y_prompt, y_sample) + tuple(jnp.stack(o) for o in outs)
```

```python
import functools
import math

import numpy as np
import jax
import jax.numpy as jnp
from jax import lax
from jax.experimental import pallas as pl
from jax.experimental.pallas import tpu as pltpu

F32 = jnp.float32
BF16 = jnp.bfloat16
I32 = jnp.int32

LANES = 128
SUBLANES = 8
VMEM_LIMIT = 56 * 1024 * 1024

D_MODEL = 2048
HD_A = 64
H_A = 8
A_W = H_A * HD_A
H_IDX = 8
D_IDX = 64
IDX_SCALE = (D_IDX ** -0.5) * (H_IDX ** -0.5)
TOPK_MAX = 256
N_BUCKETS = 32
MAX_DISTANCE = 128
HD_B = 64
B_W = 1024
H_B = B_W // HD_B
W_LORA = 64
A_LORA = 64
G_LORA = 160
RWKV_PROJ_W = 3 * B_W + W_LORA + A_LORA + G_LORA
GN_EPS = 64e-5
C_W = 512
CHUNK = 128
CD = 64
G_C = C_W // CD
D_FF = 5632
NORM_EPS = 1e-6
NEG = -1e30
PAGE = 128
INT_MIN = -(2 ** 31)

ORIG_IK = 3 * A_W + H_IDX * D_IDX
ORIG_R = ORIG_IK + D_IDX + H_IDX
ORIG_GM = ORIG_R + RWKV_PROJ_W
OFF_R = 2048
OFF_IK = OFF_R + 3 * B_W
OFF_WA = OFF_IK + LANES
OFF_G = OFF_WA + W_LORA + A_LORA
OFF_GM = OFF_G + 256
PROJ_P = OFF_GM + 2 * C_W
CB_Q, CB_K, CB_V, CB_IQ = 0, 1, 2, 3
CB_IK = OFF_IK // LANES
CB_WA = OFF_WA // LANES
CB_G = OFF_G // 256
CB_U = OFF_GM // C_W
CB_PV = CB_U + 1
RW_C = 64
RW_SEQS_PER_STEP = 4
assert OFF_R % B_W == 0 and OFF_G % 256 == 0 and OFF_GM % C_W == 0


def _cparams(sem):
    return pltpu.CompilerParams(dimension_semantics=sem, vmem_limit_bytes=VMEM_LIMIT)


def _pad_cols(a, axis=-1):
    a = jnp.moveaxis(a, axis, -1)
    z = lambda n: jnp.zeros(a.shape[:-1] + (n,), a.dtype)
    lora = ORIG_R + 3 * B_W
    out = jnp.concatenate([a[..., :ORIG_IK], a[..., ORIG_R:lora], a[..., ORIG_IK:ORIG_R], z(OFF_WA - OFF_IK - D_IDX - H_IDX),
                           a[..., lora:ORIG_GM], z(OFF_GM - OFF_G - G_LORA), a[..., ORIG_GM:]], axis=-1)
    return jnp.moveaxis(out, -1, axis)


def _pad_rwkv_cols(a):
    z = lambda n: jnp.zeros(a.shape[:-1] + (n,), a.dtype)
    return jnp.concatenate([z(OFF_R), a[..., :3 * B_W], z(LANES), a[..., 3 * B_W:], z(PROJ_P - OFF_G - G_LORA)], axis=-1)


def _rwkv_cols(p):
    return jnp.concatenate([p[..., OFF_R:OFF_R + 3 * B_W], p[..., OFF_WA:OFF_WA + RWKV_PROJ_W - 3 * B_W]], axis=-1)


def _norm_matmul_kernel(x_ref, g_ref, w_ref, o_ref, xn_ref):
    @pl.when(pl.program_id(1) == 0)
    def _():
        x = x_ref[...]
        y = x * lax.rsqrt(jnp.mean(x * x, axis=-1, keepdims=True) + NORM_EPS)
        xn_ref[...] = (y * g_ref[...]).astype(BF16)

    o_ref[...] = lax.dot_general(xn_ref[...], w_ref[...], (((1,), (1,)), ((), ())), preferred_element_type=F32)


def norm_matmul(x, g, wt, layer, tm, tn):
    m, d = x.shape
    n = wt.shape[1]
    return pl.pallas_call(
        _norm_matmul_kernel,
        grid=(m // tm, n // tn),
        in_specs=[pl.BlockSpec((tm, d), lambda i, j: (i, 0)),
                  pl.BlockSpec((1, d), lambda i, j: (0, 0)),
                  pl.BlockSpec((None, tn, d), lambda i, j: (layer, j, 0))],
        out_specs=pl.BlockSpec((tm, tn), lambda i, j: (i, j)),
        out_shape=jax.ShapeDtypeStruct((m, n), F32),
        scratch_shapes=[pltpu.VMEM((tm, d), BF16)],
        compiler_params=_cparams(("parallel", "arbitrary")),
        name="norm_matmul",
    )(x, g.reshape(1, d), wt)


def _out_proj_kernel(x_ref, ya_ref, yb_ref, yc_ref, w_ref, o_ref):
    acc = jnp.dot(ya_ref[...], w_ref[0:A_W, :], preferred_element_type=F32)
    acc += jnp.dot(yb_ref[...], w_ref[A_W:A_W + B_W, :], preferred_element_type=F32)
    acc += jnp.dot(yc_ref[...], w_ref[A_W + B_W:, :], preferred_element_type=F32)
    o_ref[...] = x_ref[...] + acc


def out_proj(x, ya, yb, yc, w, layer, tm):
    m, d = x.shape
    return pl.pallas_call(
        _out_proj_kernel,
        grid=(m // tm,),
        in_specs=[pl.BlockSpec((tm, d), lambda i: (i, 0)),
                  pl.BlockSpec((tm, A_W), lambda i: (i, 0)),
                  pl.BlockSpec((tm, B_W), lambda i: (i, 0)),
                  pl.BlockSpec((tm, C_W), lambda i: (i, 0)),
                  pl.BlockSpec((None,) + w.shape[1:], lambda i: (layer, 0, 0))],
        out_specs=pl.BlockSpec((tm, d), lambda i: (i, 0)),
        out_shape=jax.ShapeDtypeStruct((m, d), F32),
        compiler_params=_cparams(("parallel",)),
        name="out_proj",
    )(x, ya, yb, yc, w)


def _ffn_kernel(h_ref, g_ref, wg_ref, wu_ref, wd_ref, o_ref, hn_ref, acc_ref):
    f = pl.program_id(1)

    @pl.when(f == 0)
    def _():
        x = h_ref[...]
        y = x * lax.rsqrt(jnp.mean(x * x, axis=-1, keepdims=True) + NORM_EPS)
        hn_ref[...] = (y * g_ref[...]).astype(BF16)
        acc_ref[...] = jnp.zeros_like(acc_ref)

    hn = hn_ref[...]
    gate = jnp.dot(hn, wg_ref[...], preferred_element_type=F32)
    up = jnp.dot(hn, wu_ref[...], preferred_element_type=F32)
    act = (gate / (1.0 + jnp.exp(-gate))) * up
    acc_ref[...] += jnp.dot(act.astype(BF16), wd_ref[...], preferred_element_type=F32)

    @pl.when(f == pl.num_programs(1) - 1)
    def _():
        o_ref[...] = h_ref[...] + acc_ref[...]


def ffn(h, g, wg, wu, wd, layer, tm, tf):
    m, d = h.shape
    nf = wg.shape[2]
    return pl.pallas_call(
        _ffn_kernel,
        grid=(m // tm, nf // tf),
        in_specs=[pl.BlockSpec((tm, d), lambda i, f: (i, 0)),
                  pl.BlockSpec((1, d), lambda i, f: (0, 0)),
                  pl.BlockSpec((None, d, tf), lambda i, f: (layer, 0, f)),
                  pl.BlockSpec((None, d, tf), lambda i, f: (layer, 0, f)),
                  pl.BlockSpec((None, tf, d), lambda i, f: (layer, f, 0))],
        out_specs=pl.BlockSpec((tm, d), lambda i, f: (i, 0)),
        out_shape=jax.ShapeDtypeStruct((m, d), F32),
        scratch_shapes=[pltpu.VMEM((tm, d), BF16), pltpu.VMEM((tm, d), F32)],
        compiler_params=_cparams(("parallel", "arbitrary")),
        name="ffn",
    )(h, g.reshape(1, d), wg, wu, wd)


def _rms_kernel(x_ref, g_ref, o_ref):
    x = x_ref[...]
    o_ref[...] = x * lax.rsqrt(jnp.mean(x * x, axis=-1, keepdims=True) + NORM_EPS) * g_ref[...]


def rms_final(x, g, tm):
    m, d = x.shape
    return pl.pallas_call(
        _rms_kernel,
        grid=(m // tm,),
        in_specs=[pl.BlockSpec((tm, d), lambda i: (i, 0)), pl.BlockSpec((1, d), lambda i: (0, 0))],
        out_specs=pl.BlockSpec((tm, d), lambda i: (i, 0)),
        out_shape=jax.ShapeDtypeStruct((m, d), F32),
        compiler_params=_cparams(("parallel",)),
        name="rms_final",
    )(x, g.reshape(1, d))


def _bucket_np(dist):
    max_exact = N_BUCKETS // 2
    n = np.maximum(dist, 0)
    nf = np.maximum(n, 1).astype(np.float64)
    large = max_exact + (np.log(nf / max_exact) / math.log(MAX_DISTANCE / max_exact)
                         * (N_BUCKETS - max_exact)).astype(np.int32)
    return np.where(n < max_exact, n, np.minimum(large, N_BUCKETS - 1)).astype(np.int32)


def _bias_tiles_kernel(rb_ref, bp_ref, bs_ref, op_ref, os_ref):
    for d in range(2):
        bk = bp_ref[d]
        for h in range(H_A):
            t = jnp.zeros(bk.shape, F32)
            for b in range(N_BUCKETS):
                t = jnp.where(bk == b, rb_ref[b, h], t)
            op_ref[d, h] = t
    for d in range(3):
        bk = bs_ref[d]
        for h in range(H_A):
            t = jnp.zeros(bk.shape, F32)
            for b in range(N_BUCKETS):
                t = jnp.where(bk == b, rb_ref[b, h], t)
            os_ref[d, h * SUBLANES:(h + 1) * SUBLANES, :] = t


def bias_tiles(rel_bias, dec_seq):
    r = np.arange(LANES)[:, None]
    c = np.arange(LANES)[None, :]
    bp = np.stack([_bucket_np(r - c), _bucket_np(LANES + r - c)])
    t = np.arange(dec_seq)[:, None]
    bs = np.stack([_bucket_np(np.full((dec_seq, LANES), 4 * MAX_DISTANCE)),
                   _bucket_np(PAGE + t - c), _bucket_np(t - c)])
    vm = pl.BlockSpec(memory_space=pltpu.VMEM)
    return pl.pallas_call(
        _bias_tiles_kernel,
        in_specs=[pl.BlockSpec(memory_space=pltpu.SMEM), vm, vm],
        out_specs=[vm, vm],
        out_shape=[jax.ShapeDtypeStruct((2, H_A, LANES, LANES), F32),
                   jax.ShapeDtypeStruct((3, H_A * dec_seq, LANES), F32)],
        name="bias_tiles",
    )(rel_bias, jnp.asarray(bp), jnp.asarray(bs))


def _score_key(s):
    s = jnp.where(s == 0.0, 0.0, s)
    bits = lax.bitcast_convert_type(s, I32)
    return jnp.where(bits < 0, bits ^ 0x7FFFFFFF, bits)


class _Pairs:
    def __init__(self, n_pairs):
        self.n_pairs = n_pairs


def _chunk_loop(n_chunks, body, init):
    if isinstance(n_chunks, _Pairs):
        return lax.fori_loop(0, n_chunks.n_pairs, lambda m, c: body(2 * m + 1, body(2 * m, c)), init)
    return lax.fori_loop(0, n_chunks, body, init)


def _kth_largest(sc_ref, n_chunks, kf, shape):
    def count_ge(cand):
        ge = (sc_ref[...] >= cand[None]).astype(F32)
        q = -(-n_chunks // 4)
        a = ((jnp.sum(ge[0:q], axis=0) + jnp.sum(ge[q:2 * q], axis=0))
             + (jnp.sum(ge[2 * q:3 * q], axis=0) + jnp.sum(ge[3 * q:], axis=0)))
        return jnp.sum(a, axis=1, keepdims=True)

    zero = jnp.zeros(shape, I32)
    t0 = jnp.where(count_ge(zero) >= kf, zero, jnp.full(shape, INT_MIN, I32))

    def bit_body(it, t):
        cand = t | jnp.left_shift(jnp.int32(1), 30 - it)
        return jnp.where(count_ge(cand) >= kf, cand, t)

    return lax.fori_loop(0, 31, bit_body, t0)


def _kth_largest_t(sct_ref, n_chunks, kf, tq):
    groups = LANES // SUBLANES

    def count_ge(cand):
        def body(kc, a):
            ge = sct_ref[kc].reshape(groups, SUBLANES, tq) >= cand[None]
            return a + jnp.sum(ge.astype(F32), axis=0)

        a = _chunk_loop(n_chunks, body, jnp.zeros((SUBLANES, tq), F32))
        return jnp.sum(a, axis=0, keepdims=True)

    zero = jnp.zeros((SUBLANES, tq), I32)
    t0 = jnp.where(count_ge(zero) >= kf, zero, jnp.full((SUBLANES, tq), INT_MIN, I32))

    def bit_body(it, t):
        cand = t | jnp.left_shift(jnp.int32(1), 30 - it)
        return jnp.where(count_ge(cand) >= kf, cand, t)

    t = lax.fori_loop(0, 31, bit_body, t0)
    hi = jnp.transpose(jnp.broadcast_to((t[0:1] >> 16).astype(F32), (LANES, tq)))
    lo = jnp.transpose(jnp.broadcast_to((t[0:1] & 0xFFFF).astype(F32), (LANES, tq)))
    return (hi.astype(I32) << 16) | lo.astype(I32)


def _select_mask(sc_ref, mb_ref, n_chunks, kf, thr, shape, causal_fn):
    def cnt_body(kc, carry):
        key = sc_ref[kc]
        return carry[0] + (key > thr).astype(F32), carry[1] + (key >= thr).astype(F32)

    zeros = jnp.zeros(shape, F32)
    if isinstance(n_chunks, int):
        keys = sc_ref[...]
        cgt = jnp.sum((keys > thr[None]).astype(F32), axis=0)
        cge = jnp.sum((keys >= thr[None]).astype(F32), axis=0)
    else:
        cgt, cge = _chunk_loop(n_chunks, cnt_body, (zeros, zeros))
    need = kf - jnp.sum(cgt, axis=1, keepdims=True)
    cge = jnp.sum(cge, axis=1, keepdims=True)

    def exact_k():
        def body(kc, carry):
            mb_ref[kc] = jnp.where((sc_ref[kc] >= thr) & causal_fn(kc), 0.0, NEG)
            return carry

        if isinstance(n_chunks, int):
            lax.fori_loop(0, n_chunks, body, 0, unroll=8)
        else:
            _chunk_loop(n_chunks, body, 0)

    def with_ties():
        rr = lax.broadcasted_iota(I32, (LANES, LANES), 0)
        cc = lax.broadcasted_iota(I32, (LANES, LANES), 1)
        tri = (rr <= cc).astype(BF16)

        def body(kc, offs):
            key = sc_ref[kc]
            eq = key == thr
            pre = jnp.dot(eq.astype(BF16), tri, preferred_element_type=F32) + offs
            sel = (key > thr) | (eq & (pre <= need))
            mb_ref[kc] = jnp.where(sel & causal_fn(kc), 0.0, NEG)
            return jnp.broadcast_to(pre[:, LANES - 1:LANES], shape)

        _chunk_loop(n_chunks, body, zeros)

    lax.cond(jnp.max(jnp.abs(cge - kf)) > 0.5, with_ties, exact_k)


def _dsa_prompt_kernel(topk, q_ref, iq_ref, iw_ref, k_ref, v_ref, ik_ref, bias_ref, rb_ref, o_ref,
                       kb, vb, ikd, qm, iqm, iwb, sc, sct, mb, lgs, acc, m_s, l_s):
    i = pl.program_id(1)
    tq = q_ref.shape[0]
    shape = (tq, LANES)
    nt = (((1,), (1,)), ((), ()))

    @pl.when(i == 0)
    def _():
        kb[...] = k_ref[...].astype(BF16)
        vb[...] = v_ref[...].astype(BF16)
        ik = ik_ref[:, 0:D_IDX].astype(BF16)
        ikd[...] = jnp.concatenate([ik, ik], axis=1)

    row = lax.broadcasted_iota(I32, shape, 0) + i * tq
    col = lax.broadcasted_iota(I32, shape, 1)
    lane_lo = col < HD_A
    sub = tq // LANES
    n_chunks = _Pairs((i + 1) * (sub // 2))
    iw = iw_ref[:, D_IDX:D_IDX + H_IDX]
    for h in range(H_A):
        ps = slice((h // 2) * LANES, (h // 2 + 1) * LANES)
        keep = lane_lo if h % 2 == 0 else jnp.logical_not(lane_lo)
        qm[h] = jnp.where(keep, q_ref[:, ps] * (HD_A ** -0.5), 0.0).astype(BF16)
        iqm[h] = jnp.where(keep, iq_ref[:, ps], 0.0).astype(BF16)
        iwb[h] = jnp.broadcast_to(iw[:, h:h + 1], shape)

    def causal(kc):
        return (col + kc * LANES) <= row

    def score_body(kc, carry):
        ikc = ikd[pl.ds(pl.multiple_of(kc * LANES, LANES), LANES), :]
        ss = [lax.dot_general(iqm[h], ikc, nt, preferred_element_type=F32) for h in range(H_IDX)]
        ws = [iwb[h] * jnp.maximum(ss[h], 0.0) for h in range(H_IDX)]
        tot = ((ws[0] + ws[1]) + (ws[2] + ws[3])) + ((ws[4] + ws[5]) + (ws[6] + ws[7]))
        tot = jnp.where(causal(kc), tot * IDX_SCALE, NEG)
        sc[kc] = _score_key(tot)
        sct[kc] = _score_key(jnp.transpose(tot))
        return carry

    _chunk_loop(n_chunks, score_body, 0)
    kf = float(topk)
    thr = _kth_largest_t(sct, n_chunks, kf, tq)
    _select_mask(sc, mb, n_chunks, kf, thr, shape, causal)

    m_s[...] = jnp.full(m_s.shape, NEG, F32)

    def bias_tile(h, kc):
        tiles = []
        for s in range(sub):
            d = i * sub + s - kc
            tiles.append(jnp.where(d == 0, bias_ref[0, h],
                                   jnp.where(d == 1, bias_ref[1, h], rb_ref[N_BUCKETS - 1, h])))
        return jnp.concatenate(tiles, axis=0)

    def logit_body(m, carry):
        off = pl.multiple_of(m * 2 * LANES, 2 * LANES)
        mb2 = jnp.concatenate([mb[2 * m], mb[2 * m + 1]], axis=1)
        for h in range(H_A):
            ps = slice((h // 2) * LANES, (h // 2 + 1) * LANES)
            lg = lax.dot_general(qm[h], kb[pl.ds(off, 2 * LANES), ps], nt, preferred_element_type=F32)
            lg = lg + jnp.concatenate([bias_tile(h, 2 * m), bias_tile(h, 2 * m + 1)], axis=1) + mb2
            lgs[h, 2 * m] = lg[:, 0:LANES]
            lgs[h, 2 * m + 1] = lg[:, LANES:]
            m_s[h] = jnp.maximum(m_s[h], jnp.maximum(lg[:, 0:LANES], lg[:, LANES:]))
        return carry

    lax.fori_loop(0, (i + 1) * (sub // 2), logit_body, 0)
    for h in range(H_A):
        m_s[h] = jnp.broadcast_to(jnp.max(m_s[h], axis=1, keepdims=True), shape)
    l_s[...] = jnp.zeros(l_s.shape, F32)
    acc[...] = jnp.zeros(acc.shape, F32)

    def attn_body(kc, carry):
        off = pl.multiple_of(kc * LANES, LANES)
        for h in range(H_A):
            ps = slice((h // 2) * LANES, (h // 2 + 1) * LANES)
            p = jnp.exp(lgs[h, kc] - m_s[h])
            l_s[h] += p
            acc[h] += jnp.dot(p.astype(BF16), vb[pl.ds(off, LANES), ps], preferred_element_type=F32)
        return carry

    _chunk_loop(n_chunks, attn_body, 0)
    for j in range(H_A // 2):
        lo = acc[2 * j] / jnp.sum(l_s[2 * j], axis=1, keepdims=True)
        hi = acc[2 * j + 1] / jnp.sum(l_s[2 * j + 1], axis=1, keepdims=True)
        o_ref[:, j * LANES:(j + 1) * LANES] = jnp.where(lane_lo, lo, hi).astype(o_ref.dtype)


def dsa_prompt(proj, bias_p, rel_bias, tq=2 * LANES):
    b, s, _ = proj.shape
    assert tq % (2 * LANES) == 0 and s % tq == 0
    once = pl.Buffered(1)
    topk = min(TOPK_MAX, s // 4)
    nc = s // LANES
    kern = functools.partial(_dsa_prompt_kernel, topk)
    return pl.pallas_call(
        kern,
        grid=(b, s // tq),
        in_specs=[pl.BlockSpec((None, tq, A_W), lambda bi, i: (bi, i, CB_Q)),
                  pl.BlockSpec((None, tq, A_W), lambda bi, i: (bi, i, CB_IQ)),
                  pl.BlockSpec((None, tq, LANES), lambda bi, i: (bi, i, CB_IK)),
                  pl.BlockSpec((None, s, A_W), lambda bi, i: (bi, 0, CB_K), pipeline_mode=once),
                  pl.BlockSpec((None, s, A_W), lambda bi, i: (bi, 0, CB_V), pipeline_mode=once),
                  pl.BlockSpec((None, s, LANES), lambda bi, i: (bi, 0, CB_IK), pipeline_mode=once),
                  pl.BlockSpec(bias_p.shape, lambda bi, i: (0, 0, 0, 0), pipeline_mode=once),
                  pl.BlockSpec(memory_space=pltpu.SMEM)],
        out_specs=pl.BlockSpec((None, tq, A_W), lambda bi, i: (bi, i, 0)),
        out_shape=jax.ShapeDtypeStruct((b, s, A_W), BF16),
        scratch_shapes=[pltpu.VMEM((s, A_W), BF16), pltpu.VMEM((s, A_W), BF16), pltpu.VMEM((s, LANES), BF16),
                        pltpu.VMEM((H_A, tq, LANES), BF16), pltpu.VMEM((H_IDX, tq, LANES), BF16),
                        pltpu.VMEM((H_IDX, tq, LANES), F32),
                        pltpu.VMEM((nc, tq, LANES), I32), pltpu.VMEM((nc, LANES, tq), I32),
                        pltpu.VMEM((nc, tq, LANES), F32),
                        pltpu.VMEM((H_A, nc, tq, LANES), F32),
                        pltpu.VMEM((H_A, tq, LANES), F32), pltpu.VMEM((H_A, tq, LANES), F32),
                        pltpu.VMEM((H_A, tq, LANES), F32)],
        compiler_params=_cparams(("parallel", "arbitrary")),
        name="dsa_prompt",
    )(proj, proj, proj, proj, proj, proj, bias_p, rel_bias)


def _dsa_sample_select_kernel(pps, n_pages, topk, pt_ref, iq_ref, ikiw_ref, *rest):
    page_refs = rest[:pps]
    mb_ref = rest[pps]
    iq2, wb, sc = rest[pps + 1:]
    j = pl.program_id(1)
    t = iq_ref.shape[0]
    shape = (t, LANES)
    nc = n_pages + 1

    @pl.when(j == 0)
    def _():
        iq = iq_ref[...]
        ikiw = ikiw_ref[...]
        for h in range(H_IDX):
            iq2[h * t:(h + 1) * t, :] = iq[:, h * D_IDX:(h + 1) * D_IDX].astype(BF16)
            wb[h * t:(h + 1) * t, :] = jnp.broadcast_to(ikiw[:, D_IDX + h:D_IDX + h + 1], shape)

    def scores(s):
        n = s.shape[1]
        s = jnp.maximum(s, 0.0) * jnp.tile(wb[...], (1, n // LANES))
        return jnp.sum(s.reshape(H_IDX, t, n), axis=0) * IDX_SCALE

    ikt = jnp.concatenate([r[...] for r in page_refs], axis=1).astype(BF16)
    keys = _score_key(scores(jnp.dot(iq2[...], ikt, preferred_element_type=F32)))
    for u in range(pps):
        sc[j * pps + u] = keys[:, u * LANES:(u + 1) * LANES]

    @pl.when(j == pl.num_programs(1) - 1)
    def _():
        row = lax.broadcasted_iota(I32, shape, 0)
        col = lax.broadcasted_iota(I32, shape, 1)
        ik_new = jnp.concatenate([ikiw_ref[:, 0:D_IDX], jnp.zeros((LANES - t, D_IDX), F32)], axis=0)
        s_new = lax.dot_general(iq2[...], ik_new.astype(BF16), (((1,), (1,)), ((), ())), preferred_element_type=F32)
        s_new = jnp.where(col <= row, scores(s_new), NEG)
        sc[n_pages] = jnp.where(col < t, _score_key(s_new), INT_MIN)
        kf = float(topk)
        thr = _kth_largest(sc, nc, kf, shape)

        def causal(kc):
            return (kc < n_pages) | (col <= row)

        _select_mask(sc, mb_ref, nc, kf, thr, shape, causal)


def _dsa_sample_attn_kernel(pps, n_pages, pt_ref, q_ref, kn_ref, vn_ref, mb_ref, mbn_ref, bias_ref, *rest):
    k_refs = rest[:pps]
    v_refs = rest[pps:2 * pps]
    o_ref = rest[2 * pps]
    qbd, acc, m_s, l_s = rest[2 * pps + 1:]
    j = pl.program_id(1)
    t = q_ref.shape[0]
    rows = H_A * t

    def blockdiag(x):
        r = lax.broadcasted_iota(I32, (rows, A_W), 0) // t
        c = lax.broadcasted_iota(I32, (rows, A_W), 1) // HD_A
        return jnp.where(r == c, jnp.tile(x, (H_A, 1)), 0.0)

    @pl.when(j == 0)
    def _():
        qbd[...] = blockdiag(q_ref[...] * (HD_A ** -0.5)).astype(BF16)
        m_s[...] = jnp.full(m_s.shape, NEG, F32)
        l_s[...] = jnp.zeros(l_s.shape, F32)
        acc[...] = jnp.zeros(acc.shape, F32)

    def step(lg, pv_fn):
        m_old = m_s[...]
        m_new = jnp.maximum(m_old, jnp.max(lg, axis=1, keepdims=True))
        p = jnp.exp(lg - m_new)
        alpha = jnp.exp(m_old - m_new)
        l_s[...] = alpha * l_s[...] + jnp.sum(p, axis=1, keepdims=True)
        acc[...] = alpha * acc[...] + pv_fn(p.astype(BF16))
        m_s[...] = m_new

    kt = jnp.concatenate([r[...].reshape(A_W, PAGE) for r in k_refs], axis=1).astype(BF16)
    vt = jnp.concatenate([r[...].reshape(A_W, PAGE) for r in v_refs], axis=1).astype(BF16)
    last = j == pl.num_programs(1) - 1
    bias = jnp.concatenate([bias_ref[0]] * (pps - 1) + [jnp.where(last, bias_ref[1], bias_ref[0])], axis=1)
    mbc = jnp.concatenate([mb_ref[u] for u in range(pps)], axis=1)
    lg = jnp.dot(qbd[...], kt, preferred_element_type=F32) + bias + jnp.tile(mbc, (H_A, 1))
    step(lg, lambda p: lax.dot_general(p, vt, (((1,), (1,)), ((), ())), preferred_element_type=F32))

    @pl.when(last)
    def _():
        pad = jnp.zeros((LANES - t, A_W), F32)
        kn = jnp.concatenate([kn_ref[...], pad], axis=0).astype(BF16)
        vn = jnp.concatenate([vn_ref[...], pad], axis=0).astype(BF16)
        lgn = lax.dot_general(qbd[...], kn, (((1,), (1,)), ((), ())), preferred_element_type=F32)
        step(lgn + bias_ref[2] + jnp.tile(mbn_ref[0], (H_A, 1)),
             lambda p: jnp.dot(p, vn, preferred_element_type=F32))
        res = blockdiag_sum(acc[...] / l_s[...], t)
        o_ref[...] = res.astype(o_ref.dtype)


def blockdiag_sum(x, t):
    c = lax.broadcasted_iota(I32, (t, A_W), 1) // HD_A
    out = jnp.zeros((t, A_W), F32)
    for h in range(H_A):
        out = out + jnp.where(c == h, x[h * t:(h + 1) * t, :], 0.0)
    return out


def dsa_sample(proj, layer, cache_k, cache_v, cache_idx_k, page_table, bias_s, pps_sel=64, pps_att=32):
    db, t, _ = proj.shape
    n_pages = page_table.shape[1]
    nc = n_pages + 1
    topk = min(TOPK_MAX, (n_pages * PAGE + t) // 4)
    n_pool = cache_k.shape[1]
    ck = jnp.transpose(cache_k, (0, 1, 3, 4, 2))
    cv = jnp.transpose(cache_v, (0, 1, 3, 4, 2))
    cik = jnp.swapaxes(cache_idx_k, 2, 3)
    pt = page_table.reshape(-1)

    def page_spec(dims, pps, u):
        zeros = (0,) * len(dims)
        return pl.BlockSpec((None, None) + dims,
                            lambda b, j, ptr: (layer, ptr[b * n_pages + j * pps + u]) + zeros)

    sel = pl.pallas_call(
        functools.partial(_dsa_sample_select_kernel, pps_sel, n_pages, topk),
        grid_spec=pltpu.PrefetchScalarGridSpec(
            num_scalar_prefetch=1,
            grid=(db, n_pages // pps_sel),
            in_specs=[pl.BlockSpec((None, t, A_W), lambda b, j, ptr: (b, 0, CB_IQ)),
                      pl.BlockSpec((None, t, LANES), lambda b, j, ptr: (b, 0, CB_IK))]
                     + [page_spec((D_IDX, PAGE), pps_sel, u) for u in range(pps_sel)],
            out_specs=pl.BlockSpec((None, nc, t, LANES), lambda b, j, ptr: (b, 0, 0, 0)),
            scratch_shapes=[pltpu.VMEM((H_IDX * t, D_IDX), BF16), pltpu.VMEM((H_IDX * t, LANES), F32),
                            pltpu.VMEM((nc, t, LANES), I32)]),
        out_shape=jax.ShapeDtypeStruct((db, nc, t, LANES), F32),
        compiler_params=_cparams(("parallel", "arbitrary")),
        name="dsa_sample_select",
    )(pt, proj, proj, *([cik] * pps_sel))

    return pl.pallas_call(
        functools.partial(_dsa_sample_attn_kernel, pps_att, n_pages),
        grid_spec=pltpu.PrefetchScalarGridSpec(
            num_scalar_prefetch=1,
            grid=(db, n_pages // pps_att),
            in_specs=[pl.BlockSpec((None, t, A_W), lambda b, j, ptr: (b, 0, CB_Q)),
                      pl.BlockSpec((None, t, A_W), lambda b, j, ptr: (b, 0, CB_K)),
                      pl.BlockSpec((None, t, A_W), lambda b, j, ptr: (b, 0, CB_V)),
                      pl.BlockSpec((None, pps_att, t, LANES), lambda b, j, ptr: (b, j, 0, 0)),
                      pl.BlockSpec((None, 1, t, LANES), lambda b, j, ptr: (b, n_pages, 0, 0)),
                      pl.BlockSpec(bias_s.shape, lambda b, j, ptr: (0, 0, 0))]
                     + [page_spec((H_A, HD_A, PAGE), pps_att, u) for u in range(pps_att)] * 2,
            out_specs=pl.BlockSpec((None, t, A_W), lambda b, j, ptr: (b, 0, 0)),
            scratch_shapes=[pltpu.VMEM((H_A * t, A_W), BF16), pltpu.VMEM((H_A * t, A_W), F32),
                            pltpu.VMEM((H_A * t, 1), F32), pltpu.VMEM((H_A * t, 1), F32)]),
        out_shape=jax.ShapeDtypeStruct((db, t, A_W), BF16),
        compiler_params=_cparams(("parallel", "arbitrary")),
        name="dsa_sample_attn",
    )(pt, proj, proj, proj, sel, sel, bias_s, *([ck] * pps_att), *([cv] * pps_att))


def _mm(a, b):
    return jnp.dot(a.astype(BF16), b.astype(BF16), preferred_element_type=F32)


def _mm_nt(a, b):
    return lax.dot_general(a.astype(BF16), b.astype(BF16), (((1,), (1,)), ((), ())), preferred_element_type=F32)


def _split3(x):
    hi = x.astype(BF16)
    r1 = x - hi.astype(F32)
    mid = r1.astype(BF16)
    lo = (r1 - mid.astype(F32)).astype(BF16)
    return hi, mid, lo


def _rwkv_kernel(has_state, t_valid, r_ref, k_ref, v_ref, wa_ref, g_ref, sr_ref, sk_ref, sv_ref, swa_ref, sg_ref,
                 mr_ref, mk_ref, mv_ref, mwa_ref, mg_ref, w2_ref, a2_ref, g2_ref, vec_ref, *rest):
    if has_state:
        s0_ref, y_ref, so_ref, pr, pk, pv, pwa, pg, st = rest
    else:
        y_ref, so_ref, pr, pk, pv, pwa, pg, st = rest
    c = pl.program_id(1)
    C = RW_C
    nb, tb = r_ref.shape[0], r_ref.shape[1]
    n_pairs = H_B // 2
    zero_blk = jnp.zeros((HD_B, HD_B), F32)

    @pl.when(c == 0)
    def _():
        if has_state:
            for bi in range(nb):
                for p in range(n_pairs):
                    st[bi * n_pairs + p] = jnp.concatenate(
                        [jnp.concatenate([s0_ref[bi, 2 * p], zero_blk], axis=1),
                         jnp.concatenate([zero_blk, s0_ref[bi, 2 * p + 1]], axis=1)], axis=0)
        else:
            st[...] = jnp.zeros(st.shape, F32)

    def shifted(bi, p_ref, prev_scr, shift_ref, mu_ref):
        p = p_ref[bi]
        if tb < C:
            p = jnp.concatenate([p, jnp.zeros((C - tb, p.shape[1]), F32)], axis=0)
        prev = jnp.where(c == 0, shift_ref[bi], prev_scr[bi])
        rowi = lax.broadcasted_iota(I32, p.shape, 0)
        ps = jnp.where(rowi == 0, prev, pltpu.roll(p, 1, 0))
        prev_scr[bi] = p[C - 1:C, :]
        return p + (ps - p) * mu_ref[...]

    vec = vec_ref[...]
    w0, a0, kkp, kap, rkp, lnw, lnb = (vec[n:n + 1, :] for n in range(7))
    r2 = lax.broadcasted_iota(I32, (2 * C, 2 * C), 0)
    c2 = lax.broadcasted_iota(I32, (2 * C, 2 * C), 1)
    same_head = (r2 // C) == (c2 // C)
    head_ones = same_head.astype(BF16)
    strict = same_head & ((c2 % C) < (r2 % C))
    incl = same_head & ((c2 % C) <= (r2 % C))

    def pairs(x):
        return jnp.concatenate([x[:, p * LANES:(p + 1) * LANES] for p in range(n_pairs)], axis=0)

    def unpairs(x):
        return jnp.concatenate([x[p * C:(p + 1) * C] for p in range(n_pairs)], axis=1)

    def head_sum(x):
        xs = pairs(x)
        hi = xs.astype(BF16)
        mid = (xs - hi.astype(F32)).astype(BF16)
        return unpairs(jnp.dot(hi, head_ones, preferred_element_type=F32)
                       + jnp.dot(mid, head_ones, preferred_element_type=F32))

    rc = lax.broadcasted_iota(I32, (C, C), 0)
    cc = lax.broadcasted_iota(I32, (C, C), 1)
    tril = (cc <= rc).astype(BF16)

    def prep(bi):
        xr = shifted(bi, r_ref, pr, sr_ref, mr_ref)
        xk = shifted(bi, k_ref, pk, sk_ref, mk_ref)
        xv = shifted(bi, v_ref, pv, sv_ref, mv_ref)
        xwa = shifted(bi, wa_ref, pwa, swa_ref, mwa_ref)
        xg = shifted(bi, g_ref, pg, sg_ref, mg_ref)
        zw = w0 + _mm(jnp.tanh(xwa[:, 0:W_LORA]), w2_ref[...])
        w_log = -(jnp.maximum(-zw, 0.0) + jnp.log(1.0 + jnp.exp(-jnp.abs(zw)))) - 0.5
        ld = -jnp.exp(w_log)
        za = a0 + _mm(xwa[:, W_LORA:W_LORA + A_LORA], a2_ref[...])
        a = 1.0 / (1.0 + jnp.exp(-za))
        g = _mm(1.0 / (1.0 + jnp.exp(-xg)), g2_ref[...])
        kk = xk * kkp
        kk = kk / jnp.maximum(jnp.sqrt(head_sum(kk * kk)), 1e-12)
        k2 = xk * (1.0 + (a - 1.0) * kap)
        bonus_w = head_sum(xr * k2 * rkp)
        if t_valid < C:
            valid = lax.broadcasted_iota(I32, (C, B_W), 0) < t_valid
            ld = jnp.where(valid, ld, 0.0)
            xv = jnp.where(valid, xv, 0.0)
            kk = jnp.where(valid, kk, 0.0)
            k2m = jnp.where(valid, k2, 0.0)
        else:
            k2m = k2
        hi, mid, lo = _split3(ld)
        lgc = (jnp.dot(tril, hi, preferred_element_type=F32) + jnp.dot(tril, mid, preferred_element_type=F32)
               + jnp.dot(tril, lo, preferred_element_type=F32))
        lg_end = lgc[C - 1:C, :]
        ginv = jnp.exp(-lgc)
        e_end = jnp.exp(lg_end - lgc)
        kb_ = kk * a
        return dict(al=kk * jnp.exp(lgc - ld), be=kb_ * ginv, kt=k2m * ginv, rt=xr * jnp.exp(lgc),
                    bp=kb_ * e_end, kp=k2m * e_end, xv=xv, g_end=jnp.exp(lg_end), bonus_w=bonus_w, g=g)

    seqs = [prep(bi) for bi in range(nb)]
    lane_lo = lax.broadcasted_iota(I32, (C, LANES), 1) < HD_B

    def stack(name, q):
        bi, p = divmod(q, n_pairs)
        xp = seqs[bi][name][:, p * LANES:(p + 1) * LANES]
        return jnp.concatenate([jnp.where(lane_lo, xp, 0.0), jnp.where(lane_lo, 0.0, xp)], axis=0)

    P = range(nb * n_pairs)
    bf = lambda x: x.astype(BF16)
    lhs = [bf(jnp.concatenate([stack('al', p), stack('rt', p)], axis=0)) for p in P]
    vs = [bf(stack('xv', p)) for p in P]
    g3 = [_mm_nt(lhs[p], jnp.concatenate([bf(stack('be', p)), bf(stack('kt', p)), bf(st[p])], axis=0)) for p in P]
    gb = [g3[p][:, 0:2 * C] for p in P]
    gk = [g3[p][:, 2 * C:4 * C] for p in P]
    p0 = [g3[p][:, 4 * C:] for p in P]
    qn = [bf(jnp.where(strict, -gb[p][0:2 * C], 0.0)) for p in P]
    u = [p0[p][0:2 * C] + _mm(jnp.where(strict, gk[p][0:2 * C], 0.0), vs[p]) for p in P]
    n_levels = int(math.log2(C))
    for m in range(n_levels):
        if m + 1 < n_levels:
            qu = [_mm(qn[p], jnp.concatenate([qn[p], bf(u[p])], axis=1)) for p in P]
            u = [u[p] + qu[p][:, 2 * C:] for p in P]
            qn = [bf(qu[p][:, 0:2 * C]) for p in P]
        else:
            u = [u[p] + _mm(qn[p], u[p]) for p in P]
    m2 = [jnp.concatenate([bf(jnp.where(incl, -gb[p][2 * C:], 0.0)), bf(jnp.where(incl, gk[p][2 * C:], 0.0))], axis=1)
          for p in P]
    uv = [jnp.concatenate([bf(u[p]), vs[p]], axis=0) for p in P]
    yp = [p0[p][2 * C:] + _mm(m2[p], uv[p]) for p in P]
    upd = [lax.dot_general(jnp.concatenate([bf(-u[p]), vs[p]], axis=0),
                           bf(jnp.concatenate([stack('bp', p), stack('kp', p)], axis=0)),
                           (((0,), (0,)), ((), ())), preferred_element_type=F32) for p in P]
    for q in P:
        bi, p = divmod(q, n_pairs)
        st[q] = st[q] * seqs[bi]['g_end'][:, p * LANES:(p + 1) * LANES] + upd[q]
    for bi in range(nb):
        sq = seqs[bi]
        y = jnp.concatenate([yp[q][0:C] + yp[q][C:2 * C] for q in range(bi * n_pairs, (bi + 1) * n_pairs)],
                            axis=1)
        mu = head_sum(y) * (1.0 / HD_B)
        dy = y - mu
        var = head_sum(dy * dy) * (1.0 / HD_B)
        yn = dy * lax.rsqrt(var + GN_EPS) * lnw + lnb
        out = (yn + sq['bonus_w'] * sq['xv']) * sq['g']
        y_ref[bi] = out[0:tb].astype(y_ref.dtype)

    @pl.when(c == pl.num_programs(1) - 1)
    def _():
        for bi in range(nb):
            for p in range(n_pairs):
                s = st[bi * n_pairs + p]
                so_ref[bi, 2 * p] = s[0:HD_B, 0:HD_B]
                so_ref[bi, 2 * p + 1] = s[HD_B:, HD_B:]


def rwkv_mix(proj, shift_full, mu_full, w2, a2, g2p, vecs, s0):
    b, t, _ = proj.shape
    tb = min(t, RW_C)
    nch = t // tb
    has_state = s0 is not None
    cbs = [(B_W, OFF_R // B_W), (B_W, OFF_R // B_W + 1), (B_W, OFF_R // B_W + 2), (LANES, CB_WA), (256, CB_G)]

    nb = RW_SEQS_PER_STEP
    assert b % nb == 0
    in_specs = ([pl.BlockSpec((nb, tb, w), functools.partial(lambda cb, bi, c: (bi, c, cb), cb)) for w, cb in cbs]
                + [pl.BlockSpec((nb, 1, w), functools.partial(lambda cb, bi, c: (bi, 0, cb), cb)) for w, cb in cbs]
                + [pl.BlockSpec((1, w), functools.partial(lambda cb, bi, c: (0, cb), cb)) for w, cb in cbs]
                + [pl.BlockSpec(w2.shape, lambda bi, c: (0, 0)),
                   pl.BlockSpec(a2.shape, lambda bi, c: (0, 0)),
                   pl.BlockSpec(g2p.shape, lambda bi, c: (0, 0)),
                   pl.BlockSpec(vecs.shape, lambda bi, c: (0, 0))])
    args = [proj] * 5 + [shift_full] * 5 + [mu_full] * 5 + [w2, a2, g2p, vecs]
    if has_state:
        in_specs.append(pl.BlockSpec((nb, H_B, HD_B, HD_B), lambda bi, c: (bi, 0, 0, 0)))
        args.append(s0)
    return pl.pallas_call(
        functools.partial(_rwkv_kernel, has_state, tb),
        grid=(b // nb, nch),
        in_specs=in_specs,
        out_specs=[pl.BlockSpec((nb, tb, B_W), lambda bi, c: (bi, c, 0)),
                   pl.BlockSpec((nb, H_B, HD_B, HD_B), lambda bi, c: (bi, 0, 0, 0))],
        out_shape=[jax.ShapeDtypeStruct((b, t, B_W), BF16),
                   jax.ShapeDtypeStruct((b, H_B, HD_B, HD_B), F32)],
        scratch_shapes=[pltpu.VMEM((nb, 1, B_W), F32)] * 3
                       + [pltpu.VMEM((nb, 1, LANES), F32), pltpu.VMEM((nb, 1, 256), F32),
                          pltpu.VMEM((nb * H_B // 2, 2 * HD_B, 2 * HD_B), F32)],
        compiler_params=_cparams(("parallel", "arbitrary")),
        name="rwkv_mix",
    )(*args)


def _gelu(x):
    return 0.5 * x * (1.0 + jnp.tanh(math.sqrt(2.0 / math.pi) * (x + 0.044715 * (x * x * x))))


def _gmlp_kernel(pu_ref, pv_ref, lnw_ref, lnb_ref, ws_ref, bst_ref, y_ref, gv_ref):
    tc = pu_ref.shape[0]
    u = _gelu(pu_ref[...])
    vf = _gelu(pv_ref[...])
    mu = jnp.mean(vf, axis=1, keepdims=True)
    var = jnp.mean(jnp.square(vf - mu), axis=1, keepdims=True)
    v = (vf - mu) * lax.rsqrt(var + 1e-5) * lnw_ref[...] + lnb_ref[...]
    gv_ref[...] = v
    ri = lax.broadcasted_iota(I32, (CHUNK, CHUNK), 0)
    ci = lax.broadcasted_iota(I32, (CHUNK, CHUNK), 1)
    if tc < CHUNK:
        v = jnp.concatenate([v, jnp.zeros((CHUNK - tc, C_W), F32)], axis=0)
    vb = v.astype(BF16)
    for g in range(G_C):
        sl = slice(g * CD, (g + 1) * CD)
        wm = jnp.where(ci <= ri, ws_ref[g], 0.0).astype(BF16)
        sp = jnp.dot(wm, vb[:, sl], preferred_element_type=F32)[0:tc] + bst_ref[0:tc, g:g + 1]
        y_ref[:, sl] = (u[:, sl] * sp).astype(y_ref.dtype)


def gmlp_mix(proj, lnw, lnb, ws, bst):
    b, t, _ = proj.shape
    tc = min(t, CHUNK)
    return pl.pallas_call(
        _gmlp_kernel,
        grid=(b, t // tc),
        in_specs=[pl.BlockSpec((None, tc, C_W), lambda bi, c: (bi, c, CB_U)),
                  pl.BlockSpec((None, tc, C_W), lambda bi, c: (bi, c, CB_PV)),
                  pl.BlockSpec((1, C_W), lambda bi, c: (0, 0)),
                  pl.BlockSpec((1, C_W), lambda bi, c: (0, 0)),
                  pl.BlockSpec(ws.shape, lambda bi, c: (0, 0, 0)),
                  pl.BlockSpec(bst.shape, lambda bi, c: (0, 0))],
        out_specs=[pl.BlockSpec((None, tc, C_W), lambda bi, c: (bi, c, 0)),
                   pl.BlockSpec((None, tc, C_W), lambda bi, c: (bi, c, 0))],
        out_shape=[jax.ShapeDtypeStruct((b, t, C_W), BF16), jax.ShapeDtypeStruct((b, t, C_W), F32)],
        compiler_params=_cparams(("parallel", "parallel")),
        name="gmlp_mix",
    )(proj, proj, lnw.reshape(1, C_W), lnb.reshape(1, C_W), ws, bst)


def _layer(x, lw, attend, shift_full, s0, tm):
    b, t, d = x.shape
    x2 = x.reshape(b * t, d)
    l = lw['layer']
    proj = norm_matmul(x2, lw['norm_mix'], lw['w_in'], l, min(2 * tm, b * t), 512).reshape(b, t, PROJ_P)
    ya = attend(proj)
    yb, wkv = rwkv_mix(proj, shift_full, lw['mu_full'], lw['w2'], lw['a2'], lw['g2p'], lw['vecs'], s0)
    yc, gv = gmlp_mix(proj, lw['gmlp_ln_w'], lw['gmlp_ln_b'], lw['gmlp_ws'], lw['gmlp_bst'])
    h = out_proj(x2, ya.reshape(b * t, A_W), yb.reshape(b * t, B_W), yc.reshape(b * t, C_W), lw['w_out'], l, tm)
    y = ffn(h, lw['norm_ffn'], lw['ffn_gate'], lw['ffn_up'], lw['ffn_down'], l, tm, 512)
    k = proj[:, :, A_W:2 * A_W].reshape(b, t, H_A, HD_A)
    v = proj[:, :, 2 * A_W:3 * A_W].reshape(b, t, H_A, HD_A)
    ik = proj[:, :, OFF_IK:OFF_IK + D_IDX]
    shift_new = _rwkv_cols(proj[:, t - 1])
    return y.reshape(b, t, d), k, v, ik, shift_new, wkv, gv


def kernel(x_prompt, x_sample, cache_k, cache_v, cache_idx_k, state_rwkv_shift, state_rwkv_wkv, page_table, rel_bias, norm_mix, w_in, rwkv_mu, rwkv_w0, rwkv_w2, rwkv_a0, rwkv_a2, rwkv_g2, rwkv_kk, rwkv_ka, rwkv_rk, rwkv_ln_w, rwkv_ln_b, gmlp_ln_w, gmlp_ln_b, gmlp_ws, gmlp_b, w_out, norm_ffn, ffn_gate, ffn_up, ffn_down, norm_final):
    depth = w_in.shape[0]
    nbp, seq, d = x_prompt.shape
    db, dec_seq, _ = x_sample.shape
    assert d == D_MODEL and seq % 512 == 0 and dec_seq == SUBLANES and w_in.shape[2] == ORIG_GM + 2 * C_W

    w_in_p = _pad_cols(jnp.swapaxes(w_in, 1, 2), axis=1).astype(BF16)
    w_out_b = w_out.astype(BF16)
    wg_b, wu_b, wd_b = ffn_gate.astype(BF16), ffn_up.astype(BF16), ffn_down.astype(BF16)
    mu_full = _pad_rwkv_cols(rwkv_mu)
    g2p = jnp.concatenate([rwkv_g2, jnp.zeros((depth, 256 - G_LORA, B_W), F32)], axis=1)
    vecs = jnp.stack([rwkv_w0, rwkv_a0, rwkv_kk, rwkv_ka, rwkv_rk, rwkv_ln_w, rwkv_ln_b,
                      jnp.zeros_like(rwkv_w0)], axis=1)
    bst = jnp.swapaxes(gmlp_b, 1, 2)
    shift_s = _pad_rwkv_cols(state_rwkv_shift)[:, :, None, :]
    shift_p = jnp.zeros((nbp, 1, PROJ_P), F32)
    bias_p, bias_s = bias_tiles(rel_bias, dec_seq)

    xp, xs = x_prompt, x_sample
    outs = [[] for _ in range(11)]
    for l in range(depth):
        lw = {'layer': l, 'norm_mix': norm_mix[l], 'w_in': w_in_p, 'mu_full': mu_full[l:l + 1], 'w2': rwkv_w2[l],
              'a2': rwkv_a2[l], 'g2p': g2p[l], 'vecs': vecs[l], 'gmlp_ln_w': gmlp_ln_w[l],
              'gmlp_ln_b': gmlp_ln_b[l], 'gmlp_ws': gmlp_ws[l], 'gmlp_bst': bst[l], 'w_out': w_out_b,
              'norm_ffn': norm_ffn[l], 'ffn_gate': wg_b, 'ffn_up': wu_b, 'ffn_down': wd_b}
        attend_p = functools.partial(dsa_prompt, bias_p=bias_p, rel_bias=rel_bias)
        xp, kp, vp, ikp, shp, wkvp, _ = _layer(xp, lw, attend_p, shift_p, None, 512)
        attend_s = functools.partial(dsa_sample, layer=l, cache_k=cache_k, cache_v=cache_v,
                                     cache_idx_k=cache_idx_k, page_table=page_table, bias_s=bias_s)
        xs, k_s, v_s, ik_s, sh_s, wkv_s, gv_s = _layer(xs, lw, attend_s, shift_s[l], state_rwkv_wkv[l],
                                                       db * dec_seq)
        for lst, val in zip(outs, (kp, vp, ikp, shp, wkvp, k_s, v_s, ik_s, sh_s, wkv_s, gv_s)):
            lst.append(val)
    y_prompt = rms_final(xp.reshape(nbp * seq, d), norm_final, 512).reshape(nbp, seq, d)
    y_sample = rms_final(xs.reshape(db * dec_seq, d), norm_final, db * dec_seq).reshape(db, dec_seq, d)
    return (y_prompt, y_sample) + tuple(jnp.stack(o) for o in outs)
```
